```python
import jax
import jax.numpy as jnp
from jax import lax
import numpy as np

D_MODEL = 1024
BATCH = 4
SEQ = 8192
DEPTH = 2

GRID_W = 64
CTX_LEN = 256
HEAD_DIM = 64
NA_HEADS = 6
NA_WIN_R = 8
NA_WIN_C = 16
RET_HEADS = 6
RET_CHUNK = 128
POOL_WINDOWS = (2, 4, 8, 16)
POOL_GROUP = 64
NA_WIDTH = NA_HEADS * HEAD_DIM
RET_WIDTH = RET_HEADS * HEAD_DIM
POOL_WIDTH = POOL_GROUP * len(POOL_WINDOWS)
D_MIX = NA_WIDTH + RET_WIDTH + POOL_WIDTH
O_NA_Q = 0
O_NA_K = O_NA_Q + NA_WIDTH
O_NA_V = O_NA_K + NA_WIDTH
O_RET_Q = O_NA_V + NA_WIDTH
O_RET_K = O_RET_Q + RET_WIDTH
O_RET_V = O_RET_K + RET_WIDTH
O_RET_G = O_RET_V + RET_WIDTH
O_POOL = O_RET_G + RET_WIDTH
D_PROJ = O_POOL + POOL_WIDTH
ROPE_BASE = 10000.0
ROPE_PAIRS = HEAD_DIM // 4
PEER_HEADS = 8
PEER_NKEYS = 128
PEER_KEY_DIM = 128
PEER_TOPK = 16
PEER_BLOCK = 128
N_EXPERTS = PEER_NKEYS * PEER_NKEYS
NORM_EPS = 1e-6
NEG_INF = -1e30

kernel_name = 'hybrid_na_retention_pool_peer_dit'


def rms_norm(x, w):
    xf = x.astype(jnp.float32)
    y = xf * lax.rsqrt(jnp.mean(xf * xf, axis=-1, keepdims=True) + NORM_EPS)
    return (y * w.astype(jnp.float32)).astype(x.dtype)


def modulate(h, shift, scale):
    return h * (1 + scale) + shift


def split_heads(t):
    B, T, _ = t.shape
    return t.reshape(B, T, -1, HEAD_DIM).transpose(0, 2, 1, 3)


def merge_heads(t):
    B, H, T, d = t.shape
    return t.transpose(0, 2, 1, 3).reshape(B, T, H * d)


def rope_2d_tables(S):
    t = jnp.arange(S)
    row = (t // GRID_W).astype(jnp.float32)
    col = (t % GRID_W).astype(jnp.float32)
    inv = ROPE_BASE ** (-jnp.arange(ROPE_PAIRS, dtype=jnp.float32) / ROPE_PAIRS)
    ang = jnp.concatenate([row[:, None] * inv, col[:, None] * inv], axis=-1)
    return jnp.cos(ang), jnp.sin(ang)


def _rotate(xh, cos, sin):
    x1, x2 = xh[..., :ROPE_PAIRS], xh[..., ROPE_PAIRS:]
    return jnp.concatenate([x1 * cos - x2 * sin, x1 * sin + x2 * cos], axis=-1)


def apply_rope_2d(x, cos, sin):
    cos = cos.astype(x.dtype)
    sin = sin.astype(x.dtype)
    half = 2 * ROPE_PAIRS
    return jnp.concatenate([
        _rotate(x[..., :half], cos[:, :ROPE_PAIRS], sin[:, :ROPE_PAIRS]),
        _rotate(x[..., half:], cos[:, ROPE_PAIRS:], sin[:, ROPE_PAIRS:])], axis=-1)


def neighbourhood_attention(q, k, v, k_ctx, v_ctx, rpb):
    B, H, S, d = q.shape
    rows = S // GRID_W
    wr = min(NA_WIN_R, rows)
    qg = q.reshape(B, H, rows, GRID_W, d)
    kg = k.reshape(B, H, rows, GRID_W, d)
    vg = v.reshape(B, H, rows, GRID_W, d)
    r = jnp.arange(rows)
    r_start = jnp.clip(r - wr // 2, 0, rows - wr)
    ridx = r_start[:, None] + jnp.arange(wr)[None, :]
    k_blk = kg[:, :, ridx]
    v_blk = vg[:, :, ridx]
    cidx = jnp.arange(GRID_W)
    c_start = jnp.clip(cidx - NA_WIN_C // 2, 0, GRID_W - NA_WIN_C)
    col_ok = (cidx[None, :] >= c_start[:, None]) & (cidx[None, :] < c_start[:, None] + NA_WIN_C)
    roff = ridx - r[:, None] + (NA_WIN_R - 1)
    coff = jnp.clip(cidx[None, :] - cidx[:, None] + (NA_WIN_C - 1), 0, 2 * NA_WIN_C - 2)
    bias = rpb[:, roff[:, None, :, None], coff[None, :, None, :]].astype(jnp.float32)
    scale = HEAD_DIM ** -0.5
    s_win = jnp.einsum('bhrqd,bhrjkd->bhrqjk', qg, k_blk).astype(jnp.float32) * scale + bias
    s_win = jnp.where(col_ok[:, None, :], s_win, NEG_INF)
    s_ctx = jnp.einsum('bhrqd,bhcd->bhrqc', qg, k_ctx).astype(jnp.float32) * scale
    n_win = wr * GRID_W
    s = jnp.concatenate([s_win.reshape(B, H, rows, GRID_W, n_win), s_ctx], axis=-1)
    p = jax.nn.softmax(s, axis=-1).astype(v.dtype)
    p_win = p[..., :n_win].reshape(B, H, rows, GRID_W, wr, GRID_W)
    p_ctx = p[..., n_win:]
    out = (jnp.einsum('bhrqjk,bhrjkd->bhrqd', p_win, v_blk)
           + jnp.einsum('bhrqc,bhcd->bhrqd', p_ctx, v_ctx))
    return out.reshape(B, H, S, d)


def context_attention(q, k, v):
    s = jnp.einsum('bhqd,bhkd->bhqk', q, k).astype(jnp.float32) * HEAD_DIM ** -0.5
    p = jax.nn.softmax(s, axis=-1).astype(v.dtype)
    return jnp.einsum('bhqk,bhkd->bhqd', p, v)


def retention_scan(q, k, v, log_gamma, state0):
    B, H, T, d = q.shape
    nc = T // RET_CHUNK

    def chunks(t):
        return t.astype(jnp.float32).reshape(B, H, nc, RET_CHUNK, d).transpose(2, 0, 1, 3, 4)

    pos = jnp.arange(RET_CHUNK, dtype=jnp.float32)
    diff = pos[:, None] - pos[None, :]
    decay = jnp.where(diff[None] >= 0,
                      jnp.exp(jnp.maximum(diff, 0.0)[None] * log_gamma[:, None, None]), 0.0)
    xi = jnp.exp((pos + 1.0)[None, :] * log_gamma[:, None])[..., None]
    zeta = jnp.exp((RET_CHUNK - 1.0 - pos)[None, :] * log_gamma[:, None])[..., None]
    chunk_decay = jnp.exp(RET_CHUNK * log_gamma)[:, None, None]

    def step(R, qkv):
        qc, kc, vc = qkv
        inner = jnp.einsum('bhnm,bhmd->bhnd', jnp.einsum('bhnd,bhmd->bhnm', qc, kc) * decay, vc)
        cross = jnp.einsum('bhnd,bhde->bhne', qc * xi, R)
        R = chunk_decay * R + jnp.einsum('bhmd,bhme->bhde', kc * zeta, vc)
        return R, inner + cross

    _, y = lax.scan(step, state0, (chunks(q), chunks(k), chunks(v)))
    return y.transpose(1, 2, 0, 3, 4).reshape(B, H, T, d)


def retention_bidir(q, k, v, lg_fwd, lg_bwd, state_fwd, state_bwd):
    y_f = retention_scan(q, k, v, lg_fwd, state_fwd)
    y_b = retention_scan(jnp.flip(q, 2), jnp.flip(k, 2), jnp.flip(v, 2), lg_bwd, state_bwd)
    return y_f + jnp.flip(y_b, 2)


def context_states(k, v, lg_fwd, lg_bwd):
    C = k.shape[2]
    m = jnp.arange(C, dtype=jnp.float32)
    w_f = jnp.exp((C - 1.0 - m)[None, :] * lg_fwd[:, None])
    w_b = jnp.exp(m[None, :] * lg_bwd[:, None])
    kf = k.astype(jnp.float32)
    vf = v.astype(jnp.float32)
    R_f = jnp.einsum('bhmd,hm,bhme->bhde', kf, w_f, vf)
    R_b = jnp.einsum('bhmd,hm,bhme->bhde', kf, w_b, vf)
    return R_f, R_b


def retention_output(y, gate, gn_w):
    mu = jnp.mean(y, axis=-1, keepdims=True)
    var = jnp.mean(jnp.square(y - mu), axis=-1, keepdims=True)
    yn = merge_heads((y - mu) * lax.rsqrt(var + NORM_EPS)) * gn_w.astype(jnp.float32)
    return (yn * jax.nn.silu(gate.astype(jnp.float32))).astype(gate.dtype)


def multiscale_pool(p, w_pool, scale):
    B, T, _ = p.shape
    t = jnp.arange(T)
    pf = p.astype(jnp.float32)
    outs = []
    for g, w in enumerate(POOL_WINDOWS):
        xg = pf[..., g * POOL_GROUP:(g + 1) * POOL_GROUP]
        cs = jnp.concatenate([jnp.zeros((B, 1, POOL_GROUP), jnp.float32), jnp.cumsum(xg, axis=1)], axis=1)
        lo = jnp.clip(t - w // 2, 0, T)
        hi = jnp.clip(t + w // 2, 0, T)
        mean = (cs[:, hi] - cs[:, lo]) / (hi - lo).astype(jnp.float32)[:, None]
        outs.append((mean - xg).astype(p.dtype) @ w_pool[g])
    return jnp.concatenate(outs, axis=-1) * scale


def peer_ffn(h, wq, keys, u, v):
    B, T, D = h.shape
    xb = h.reshape(-1, PEER_BLOCK, D)

    def block(xt):
        q = (xt @ wq).reshape(PEER_BLOCK, PEER_HEADS, 2, PEER_KEY_DIM)
        s_a = jnp.einsum('thd,hkd->thk', q[:, :, 0], keys[0]).astype(jnp.float32)
        s_b = jnp.einsum('thd,hkd->thk', q[:, :, 1], keys[1]).astype(jnp.float32)
        va, ia = lax.top_k(s_a, PEER_TOPK)
        vb, ib = lax.top_k(s_b, PEER_TOPK)
        cand = (va[..., :, None] + vb[..., None, :]).reshape(PEER_BLOCK, PEER_HEADS, PEER_TOPK * PEER_TOPK)
        sc, ci = lax.top_k(cand, PEER_TOPK)
        idx = (jnp.take_along_axis(ia, ci // PEER_TOPK, axis=-1) * PEER_NKEYS
               + jnp.take_along_axis(ib, ci % PEER_TOPK, axis=-1))
        gate = jax.nn.softmax(sc, axis=-1).astype(xt.dtype)
        act = jax.nn.gelu(jnp.einsum('td,thkd->thk', xt, u[idx]), approximate=False)
        return jnp.einsum('thk,thkd->td', gate * act, v[idx])

    return lax.map(block, xb).reshape(B, T, D)


def setup_inputs(seed: int = 0) -> dict:
    key = jax.random.key(seed)
    ks = jax.random.split(key, 21)
    f32 = jnp.float32

    def nrm(k, shape, scale):
        return jax.random.normal(k, shape, f32) * scale

    base_logit = jnp.log(2.0 ** (5.0 + jnp.arange(RET_HEADS, dtype=f32)) - 1.0)
    return {
        'x': nrm(ks[0], (BATCH, SEQ, D_MODEL), 1.0),
        'c': nrm(ks[1], (BATCH, D_MODEL), 1.0),
        'ctx': nrm(ks[2], (BATCH, CTX_LEN, D_MODEL), 1.0),
        'c_ctx': nrm(ks[3], (D_MODEL,), 1.0),
        'norm1_w': 1.0 + nrm(ks[4], (DEPTH, D_MODEL), 0.02),
        'norm2_w': 1.0 + nrm(ks[5], (DEPTH, D_MODEL), 0.02),
        'w_ada': nrm(ks[6], (DEPTH, D_MODEL, 6 * D_MODEL), 0.5 * D_MODEL ** -0.5),
        'b_ada': nrm(ks[7], (DEPTH, 6 * D_MODEL), 0.01),
        'w_in': nrm(ks[8], (DEPTH, D_MODEL, D_PROJ), D_MODEL ** -0.5),
        'w_out': nrm(ks[9], (DEPTH, D_MIX, D_MODEL), D_MIX ** -0.5),
        'na_rpb': nrm(ks[10], (DEPTH, NA_HEADS, 2 * NA_WIN_R - 1, 2 * NA_WIN_C - 1), 0.02),
        'ret_decay_fwd': base_logit + nrm(ks[11], (DEPTH, RET_HEADS), 0.01),
        'ret_decay_bwd': base_logit + nrm(ks[12], (DEPTH, RET_HEADS), 0.01),
        'ret_gn_w': 1.0 + nrm(ks[13], (DEPTH, RET_WIDTH), 0.02),
        'pool_w': nrm(ks[14], (DEPTH, len(POOL_WINDOWS), POOL_GROUP, POOL_GROUP), POOL_GROUP ** -0.5),
        'pool_scale': 1.0 + nrm(ks[15], (DEPTH, POOL_WIDTH), 0.02),
        'peer_wq': nrm(ks[16], (DEPTH, D_MODEL, PEER_HEADS * 2 * PEER_KEY_DIM), D_MODEL ** -0.5),
        'peer_keys': nrm(ks[17], (DEPTH, 2, PEER_HEADS, PEER_NKEYS, PEER_KEY_DIM), PEER_KEY_DIM ** -0.5),
        'peer_u': nrm(ks[18], (DEPTH, N_EXPERTS, D_MODEL), D_MODEL ** -0.5),
        'peer_v': nrm(ks[19], (DEPTH, N_EXPERTS, D_MODEL), 0.5),
        'final_norm_w': 1.0 + nrm(ks[20], (D_MODEL,), 0.02),
    }


def reference(x, c, ctx, c_ctx, norm1_w, norm2_w, w_ada, b_ada, w_in, w_out, na_rpb,
              ret_decay_fwd, ret_decay_bwd, ret_gn_w, pool_w, pool_scale,
              peer_wq, peer_keys, peer_u, peer_v, final_norm_w):
    B, S, _ = x.shape
    cos, sin = rope_2d_tables(S)
    c_act = jax.nn.silu(c)
    cc_act = jax.nn.silu(c_ctx)
    for l in range(DEPTH):
        last = l == DEPTH - 1
        wi = w_in[l]
        mod_x = (c_act @ w_ada[l] + b_ada[l])[:, None, :]
        mod_c = cc_act @ w_ada[l] + b_ada[l]
        sh1, sc1, g1, sh2, sc2, g2 = jnp.split(mod_x, 6, axis=-1)
        csh1, csc1, cg1, csh2, csc2, cg2 = jnp.split(mod_c, 6, axis=-1)
        lg_f = jax.nn.log_sigmoid(ret_decay_fwd[l].astype(jnp.float32))
        lg_b = jax.nn.log_sigmoid(ret_decay_bwd[l].astype(jnp.float32))

        hc = modulate(rms_norm(ctx, norm1_w[l]), csh1, csc1)
        if last:
            kv_na_c = hc @ wi[:, O_NA_K:O_RET_Q]
            kv_ret_c = hc @ wi[:, O_RET_K:O_RET_G]
        else:
            pc = hc @ wi
            kv_na_c = pc[..., O_NA_K:O_RET_Q]
            kv_ret_c = pc[..., O_RET_K:O_RET_G]
        kc_na = split_heads(kv_na_c[..., :NA_WIDTH])
        vc_na = split_heads(kv_na_c[..., NA_WIDTH:])
        kc_ret = split_heads(kv_ret_c[..., :RET_WIDTH]) * HEAD_DIM ** -0.5
        vc_ret = split_heads(kv_ret_c[..., RET_WIDTH:])
        R_f, R_b = context_states(kc_ret, vc_ret, lg_f, lg_b)

        hx = modulate(rms_norm(x, norm1_w[l]), sh1, sc1)
        px = hx @ wi
        q_na = split_heads(px[..., O_NA_Q:O_NA_K])
        k_na = split_heads(px[..., O_NA_K:O_NA_V])
        v_na = split_heads(px[..., O_NA_V:O_RET_Q])
        q_ret = apply_rope_2d(split_heads(px[..., O_RET_Q:O_RET_K]), cos, sin)
        k_ret = apply_rope_2d(split_heads(px[..., O_RET_K:O_RET_V]), cos, sin) * HEAD_DIM ** -0.5
        v_ret = split_heads(px[..., O_RET_V:O_RET_G])
        na_out = neighbourhood_attention(q_na, k_na, v_na, kc_na, vc_na, na_rpb[l])
        ret_out = retention_output(retention_bidir(q_ret, k_ret, v_ret, lg_f, lg_b, R_f, R_b),
                                   px[..., O_RET_G:O_POOL], ret_gn_w[l])
        pool_out = multiscale_pool(px[..., O_POOL:], pool_w[l], pool_scale[l])
        mix = jnp.concatenate([merge_heads(na_out), ret_out, pool_out], axis=-1) @ w_out[l]
        x = x + g1 * mix
        x = x + g2 * peer_ffn(modulate(rms_norm(x, norm2_w[l]), sh2, sc2),
                              peer_wq[l], peer_keys[l], peer_u[l], peer_v[l])

        if not last:
            zero_state = jnp.zeros((B, RET_HEADS, HEAD_DIM, HEAD_DIM), jnp.float32)
            na_c = context_attention(split_heads(pc[..., O_NA_Q:O_NA_K]), kc_na, vc_na)
            ret_c = retention_output(
                retention_bidir(split_heads(pc[..., O_RET_Q:O_RET_K]), kc_ret, vc_ret,
                                lg_f, lg_b, zero_state, zero_state),
                pc[..., O_RET_G:O_POOL], ret_gn_w[l])
            pool_c = multiscale_pool(pc[..., O_POOL:], pool_w[l], pool_scale[l])
            mix_c = jnp.concatenate([merge_heads(na_c), ret_c, pool_c], axis=-1) @ w_out[l]
            ctx = ctx + cg1 * mix_c
            ctx = ctx + cg2 * peer_ffn(modulate(rms_norm(ctx, norm2_w[l]), csh2, csc2),
                                       peer_wq[l], peer_keys[l], peer_u[l], peer_v[l])
    return rms_norm(x, final_norm_w)
```

```python
import functools

import jax
import jax.numpy as jnp
from jax import lax
from jax.experimental import pallas as pl
from jax.experimental.pallas import tpu as pltpu
from jax.experimental.pallas import tpu_sc as plsc

D_MODEL = 1024
GRID_W = 64
HEAD_DIM = 64
NA_HEADS = 6
NA_WIN_R = 8
NA_WIN_C = 16
RET_HEADS = 6
RET_CHUNK = 128
POOL_WINDOWS = (2, 4, 8, 16)
POOL_GROUP = 64
NA_WIDTH = NA_HEADS * HEAD_DIM
RET_WIDTH = RET_HEADS * HEAD_DIM
POOL_WIDTH = POOL_GROUP * len(POOL_WINDOWS)
O_RET_G = 3 * NA_WIDTH + 3 * RET_WIDTH
O_POOL = O_RET_G + RET_WIDTH
D_PROJ = O_POOL + POOL_WIDTH
ROPE_BASE = 10000.0
ROPE_PAIRS = HEAD_DIM // 4
PEER_HEADS = 8
PEER_NKEYS = 128
PEER_KEY_DIM = 128
PEER_TOPK = 16
N_SEL = PEER_HEADS * PEER_TOPK
NORM_EPS = 1e-6
NEG_INF = -1e30

LANES = 128
VMEM_LIMIT_BYTES = 56 * 1024 * 1024

F32 = jnp.float32
BF16 = jnp.bfloat16
HIGHEST = lax.Precision.HIGHEST
HEAD_PAIRS = NA_HEADS // 2
TOK_BLOCK = 128


def _cparams(*sem):
    return pltpu.CompilerParams(dimension_semantics=sem, vmem_limit_bytes=VMEM_LIMIT_BYTES)


def _dot_nt(a, b, precision=None):
    return lax.dot_general(a, b, (((1,), (1,)), ((), ())), precision=precision,
                           preferred_element_type=F32)


def _rms_mod(x, nw, shift, scale):
    y = x * lax.rsqrt(jnp.mean(x * x, axis=-1, keepdims=True) + NORM_EPS)
    return (y * nw) * (1.0 + scale) + shift


def _mod_kernel(c_ref, w_ref, b_ref, o_ref):
    c = c_ref[...]
    a = c * jax.nn.sigmoid(c)
    o_ref[0] = jnp.dot(a, w_ref[0], precision=HIGHEST, preferred_element_type=F32) + b_ref[0]


def _modulation(cvec, w_ada, b_ada):
    depth, d, n = w_ada.shape
    tn = 1536
    return pl.pallas_call(
        _mod_kernel,
        grid=(depth, n // tn),
        in_specs=[pl.BlockSpec((8, d), lambda l, j: (0, 0)),
                  pl.BlockSpec((1, d, tn), lambda l, j: (l, 0, j)),
                  pl.BlockSpec((1, 1, tn), lambda l, j: (l, 0, j))],
        out_specs=pl.BlockSpec((1, 8, tn), lambda l, j: (l, 0, j)),
        out_shape=jax.ShapeDtypeStruct((depth, 8, n), F32),
        compiler_params=_cparams("parallel", "parallel"),
        name="adaln_mod",
    )(cvec, w_ada, b_ada.reshape(depth, 1, n))


def _inproj_kernel(x_ref, nw_ref, sh_ref, sc_ref, w_ref, o_ref, p_ref):
    h = _rms_mod(x_ref[0], nw_ref[...], sh_ref[0], sc_ref[0])
    r = jnp.dot(h.astype(BF16), w_ref[...], preferred_element_type=F32)
    o_ref[0] = r[:, :O_POOL]
    p_ref[0] = r[:, O_POOL:]


def _in_proj(x, nw, shift, scale, w_bf16):
    b, t, d = x.shape
    tm = min(512, t)
    return pl.pallas_call(
        _inproj_kernel,
        grid=(b, t // tm),
        in_specs=[pl.BlockSpec((1, tm, d), lambda i, j: (i, j, 0)),
                  pl.BlockSpec((1, d), lambda i, j: (0, 0)),
                  pl.BlockSpec((1, 1, d), lambda i, j: (i, 0, 0)),
                  pl.BlockSpec((1, 1, d), lambda i, j: (i, 0, 0)),
                  pl.BlockSpec((d, D_PROJ), lambda i, j: (0, 0))],
        out_specs=[pl.BlockSpec((1, tm, O_POOL), lambda i, j: (i, j, 0)),
                   pl.BlockSpec((1, tm, POOL_WIDTH), lambda i, j: (i, j, 0))],
        out_shape=[jax.ShapeDtypeStruct((b, t, O_POOL), F32),
                   jax.ShapeDtypeStruct((b, t, POOL_WIDTH), F32)],
        compiler_params=_cparams("parallel", "parallel"),
        name="in_proj",
    )(x, nw.reshape(1, d), shift, scale, w_bf16)


def _softmax_pv(s_list, v_list):
    m = s_list[0].max(axis=-1, keepdims=True)
    for s in s_list[1:]:
        m = jnp.maximum(m, s.max(axis=-1, keepdims=True))
    num = None
    den = None
    for s, v in zip(s_list, v_list):
        p = jnp.exp(s - m)
        pv = jnp.dot(p, v, preferred_element_type=F32)
        ps = p.sum(axis=-1, keepdims=True)
        num = pv if num is None else num + pv
        den = ps if den is None else den + ps
    return num / den


def _na_kernel(q_ref, k_ref, v_ref, kc_ref, vc_ref, bias_ref, o_ref, *, rows, rb):
    i = pl.program_id(2)
    lane = lax.broadcasted_iota(jnp.int32, (1, LANES), 1)
    first = lane < HEAD_DIM
    kc = kc_ref[0]
    vc = vc_ref[0]
    scale = HEAD_DIM ** -0.5
    nk = NA_WIN_R * GRID_W

    def body(rr, carry):
        r = i * rb + rr
        rs = jnp.clip(r - NA_WIN_R // 2, 0, rows - NA_WIN_R)
        delta = rs - r + (NA_WIN_R - 1)
        q = q_ref[0, pl.ds(pl.multiple_of(rr * GRID_W, GRID_W), GRID_W), :]
        k = k_ref[0, pl.ds(pl.multiple_of(rs * GRID_W, GRID_W), nk), :]
        v = v_ref[0, pl.ds(pl.multiple_of(rs * GRID_W, GRID_W), nk), :]
        outs = []
        for h in range(2):
            qm = jnp.where(first if h == 0 else jnp.logical_not(first), q, 0.0)
            s = _dot_nt(qm, k) * scale + bias_ref[0, h, delta]
            sc = _dot_nt(qm, kc) * scale
            outs.append(_softmax_pv([s, sc], [v, vc]))
        o_ref[0, pl.ds(pl.multiple_of(rr * GRID_W, GRID_W), GRID_W), :] = jnp.where(first, outs[0], outs[1])
        return carry

    lax.fori_loop(0, rb, body, 0)


def _na_attention(px, pc, bias8):
    b, s, _ = px.shape
    c = pc.shape[1]
    rows = s // GRID_W
    assert rows >= NA_WIN_R and s % GRID_W == 0
    rb = min(8, rows)
    nq, nkb, nvb = 0, HEAD_PAIRS, 2 * HEAD_PAIRS
    return pl.pallas_call(
        functools.partial(_na_kernel, rows=rows, rb=rb),
        grid=(b, HEAD_PAIRS, rows // rb),
        in_specs=[pl.BlockSpec((1, rb * GRID_W, LANES), lambda bi, hp, i: (bi, i, nq + hp)),
                  pl.BlockSpec((1, s, LANES), lambda bi, hp, i: (bi, 0, nkb + hp)),
                  pl.BlockSpec((1, s, LANES), lambda bi, hp, i: (bi, 0, nvb + hp)),
                  pl.BlockSpec((1, c, LANES), lambda bi, hp, i: (bi, 0, nkb + hp)),
                  pl.BlockSpec((1, c, LANES), lambda bi, hp, i: (bi, 0, nvb + hp)),
                  pl.BlockSpec((1, 2, NA_WIN_R, GRID_W, NA_WIN_R * GRID_W),
                               lambda bi, hp, i: (hp, 0, 0, 0, 0))],
        out_specs=pl.BlockSpec((1, rb * GRID_W, LANES), lambda bi, hp, i: (bi, i, hp)),
        out_shape=jax.ShapeDtypeStruct((b, s, NA_WIDTH), F32),
        compiler_params=_cparams("parallel", "parallel", "arbitrary"),
        name="na_attention",
    )(px, px, px, pc, pc, bias8)


def _ctx_attn_kernel(q_ref, k_ref, v_ref, o_ref):
    lane = lax.broadcasted_iota(jnp.int32, (1, LANES), 1)
    first = lane < HEAD_DIM
    q = q_ref[0]
    k = k_ref[0]
    v = v_ref[0]
    outs = []
    for h in range(2):
        qm = jnp.where(first if h == 0 else jnp.logical_not(first), q, 0.0)
        s = _dot_nt(qm, k) * HEAD_DIM ** -0.5
        outs.append(_softmax_pv([s], [v]))
    o_ref[0] = jnp.where(first, outs[0], outs[1])


def _ctx_attention(pc):
    b, c, _ = pc.shape
    return pl.pallas_call(
        _ctx_attn_kernel,
        grid=(b, HEAD_PAIRS),
        in_specs=[pl.BlockSpec((1, c, LANES), lambda bi, hp: (bi, 0, hp)),
                  pl.BlockSpec((1, c, LANES), lambda bi, hp: (bi, 0, HEAD_PAIRS + hp)),
                  pl.BlockSpec((1, c, LANES), lambda bi, hp: (bi, 0, 2 * HEAD_PAIRS + hp))],
        out_specs=pl.BlockSpec((1, c, LANES), lambda bi, hp: (bi, 0, hp)),
        out_shape=jax.ShapeDtypeStruct((b, c, NA_WIDTH), F32),
        compiler_params=_cparams("parallel", "parallel"),
        name="ctx_attention",
    )(pc, pc, pc)


def _ret_kernel(lg_ref, qf_ref, kf_ref, vf_ref, qb_ref, kb_ref, vb_ref,
                cf_ref, sf_ref, cb_ref, sb_ref, rf0_ref, rb0_ref,
                yf_ref, yb_ref, rfo_ref, rbo_ref, rf_scr, rb_scr):
    hp = pl.program_id(1)
    c = pl.program_id(2)
    nc = pl.num_programs(2)
    cs = RET_CHUNK

    @pl.when(c == 0)
    def _():
        rf_scr[...] = rf0_ref[0, 0]
        rb_scr[...] = rb0_ref[0, 0]

    lane = lax.broadcasted_iota(jnp.int32, (1, LANES), 1)
    first = lane < HEAD_DIM
    low = (lane % (2 * ROPE_PAIRS)) < ROPE_PAIRS
    pos = lax.broadcasted_iota(jnp.int32, (cs, 1), 0).astype(F32)
    ii = lax.broadcasted_iota(jnp.int32, (cs, cs), 0)
    jj = lax.broadcasted_iota(jnp.int32, (cs, cs), 1)
    diff = (ii - jj).astype(F32)
    same_head = (ii < HEAD_DIM) == (jj < HEAD_DIM)
    scale = HEAD_DIM ** -0.5

    def rope(x, cos, sin):
        swapped = jnp.where(low, pltpu.roll(x, LANES - ROPE_PAIRS, 1), pltpu.roll(x, ROPE_PAIRS, 1))
        return x * cos + swapped * sin

    def direction(d, q_ref, k_ref, v_ref, cos_ref, sin_ref, r_scr, y_ref):
        lg0 = lg_ref[d, 2 * hp]
        lg1 = lg_ref[d, 2 * hp + 1]
        lgv = jnp.where(first, lg0, lg1)
        cos = cos_ref[...]
        sin = sin_ref[...]
        q = rope(q_ref[0], cos, sin)
        k = rope(k_ref[0], cos, sin) * scale
        v = v_ref[0]
        outs = []
        for h, lg in ((0, lg0), (1, lg1)):
            if d == 0:
                dec = jnp.where(diff >= 0, jnp.exp(jnp.maximum(diff, 0.0) * lg), 0.0)
            else:
                dec = jnp.where(diff <= 0, jnp.exp(jnp.maximum(-diff, 0.0) * lg), 0.0)
            qm = jnp.where(first if h == 0 else jnp.logical_not(first), q, 0.0)
            outs.append(jnp.dot(_dot_nt(qm, k) * dec, v, preferred_element_type=F32))
        inner = jnp.where(first, outs[0], outs[1])
        if d == 0:
            xi = jnp.exp((pos + 1.0) * lgv)
            zeta = jnp.exp((cs - 1.0 - pos) * lgv)
        else:
            xi = jnp.exp((cs - pos) * lgv)
            zeta = jnp.exp(pos * lgv)
        r = r_scr[...]
        y_ref[0] = inner + jnp.dot(q * xi, r, preferred_element_type=F32)
        kv = lax.dot_general(k * zeta, v, (((0,), (0,)), ((), ())), preferred_element_type=F32)
        r_scr[...] = jnp.exp(cs * lgv) * r + jnp.where(same_head, kv, 0.0)

    direction(0, qf_ref, kf_ref, vf_ref, cf_ref, sf_ref, rf_scr, yf_ref)
    direction(1, qb_ref, kb_ref, vb_ref, cb_ref, sb_ref, rb_scr, yb_ref)

    @pl.when(c == nc - 1)
    def _():
        rfo_ref[0, 0] = rf_scr[...]
        rbo_ref[0, 0] = rb_scr[...]


def _retention(p, lg, cos_t, sin_t, rf0, rb0):
    b, t, _ = p.shape
    nc = t // RET_CHUNK
    qo, ko, vo = 3 * HEAD_PAIRS, 4 * HEAD_PAIRS, 5 * HEAD_PAIRS
    blk = (1, RET_CHUNK, LANES)
    fwd = lambda o: pl.BlockSpec(blk, lambda bi, hp, c: (bi, c, o + hp))
    bwd = lambda o: pl.BlockSpec(blk, lambda bi, hp, c: (bi, nc - 1 - c, o + hp))
    tab_f = pl.BlockSpec((RET_CHUNK, LANES), lambda bi, hp, c: (c, 0))
    tab_b = pl.BlockSpec((RET_CHUNK, LANES), lambda bi, hp, c: (nc - 1 - c, 0))
    st = pl.BlockSpec((1, 1, LANES, LANES), lambda bi, hp, c: (bi, hp, 0, 0))
    return pl.pallas_call(
        _ret_kernel,
        grid=(b, HEAD_PAIRS, nc),
        in_specs=[pl.BlockSpec(memory_space=pltpu.SMEM),
                  fwd(qo), fwd(ko), fwd(vo), bwd(qo), bwd(ko), bwd(vo),
                  tab_f, tab_f, tab_b, tab_b, st, st],
        out_specs=[pl.BlockSpec(blk, lambda bi, hp, c: (bi, c, hp)),
                   pl.BlockSpec(blk, lambda bi, hp, c: (bi, nc - 1 - c, hp)),
                   st, st],
        out_shape=[jax.ShapeDtypeStruct((b, t, RET_WIDTH), F32),
                   jax.ShapeDtypeStruct((b, t, RET_WIDTH), F32),
                   jax.ShapeDtypeStruct((b, HEAD_PAIRS, LANES, LANES), F32),
                   jax.ShapeDtypeStruct((b, HEAD_PAIRS, LANES, LANES), F32)],
        scratch_shapes=[pltpu.VMEM((LANES, LANES), F32), pltpu.VMEM((LANES, LANES), F32)],
        compiler_params=_cparams("parallel", "parallel", "arbitrary"),
        name="retention",
    )(lg, p, p, p, p, p, p, cos_t, sin_t, cos_t, sin_t, rf0, rb0)


def _pool_kernel(prev_ref, cur_ref, next_ref, w_ref, s_ref, o_ref, scr, *, t_total, tp):
    i = pl.program_id(1)
    n = pl.num_programs(1)
    halo = POOL_WINDOWS[-1] // 2
    x = cur_ref[0]
    scr[0:halo, :] = jnp.where(i > 0, prev_ref[0], 0.0)
    scr[halo:halo + tp, :] = x
    scr[halo + tp:2 * halo + tp, :] = jnp.where(i < n - 1, next_ref[0], 0.0)
    t = i * tp + lax.broadcasted_iota(jnp.int32, (tp, 1), 0)
    lane = lax.broadcasted_iota(jnp.int32, (1, POOL_WIDTH), 1)

    def shifted(s):
        return scr[halo + s:halo + s + tp, :]

    acc = None
    mean = None
    done = 0
    for g, w in enumerate(POOL_WINDOWS):
        half = w // 2
        for s in list(range(-half, -done)) + list(range(done, half)):
            sh = x if s == 0 else shifted(s)
            acc = sh if acc is None else acc + sh
        done = half
        cnt = (jnp.minimum(t + half, t_total) - jnp.maximum(t - half, 0)).astype(F32)
        mg = acc / cnt
        mean = mg if mean is None else jnp.where(lane >= g * POOL_GROUP, mg, mean)
    dlt = (mean - x).astype(BF16)
    o_ref[0] = jnp.dot(dlt, w_ref[...], preferred_element_type=F32) * s_ref[...]


def _pool(pin, w_bd_bf16, scale):
    b, t, _ = pin.shape
    tp = min(1024, t)
    halo = POOL_WINDOWS[-1] // 2
    nh = tp // halo
    last = t // halo - 1
    return pl.pallas_call(
        functools.partial(_pool_kernel, t_total=t, tp=tp),
        grid=(b, t // tp),
        in_specs=[pl.BlockSpec((1, halo, POOL_WIDTH), lambda bi, i: (bi, jnp.maximum(i * nh - 1, 0), 0)),
                  pl.BlockSpec((1, tp, POOL_WIDTH), lambda bi, i: (bi, i, 0)),
                  pl.BlockSpec((1, halo, POOL_WIDTH), lambda bi, i: (bi, jnp.minimum((i + 1) * nh, last), 0)),
                  pl.BlockSpec((POOL_WIDTH, POOL_WIDTH), lambda bi, i: (0, 0)),
                  pl.BlockSpec((1, POOL_WIDTH), lambda bi, i: (0, 0))],
        out_specs=pl.BlockSpec((1, tp, POOL_WIDTH), lambda bi, i: (bi, i, 0)),
        out_shape=jax.ShapeDtypeStruct((b, t, POOL_WIDTH), F32),
        scratch_shapes=[pltpu.VMEM((tp + 2 * halo, POOL_WIDTH), F32)],
        compiler_params=_cparams("parallel", "parallel"),
        name="multiscale_pool",
    )(pin, pin, pin, w_bd_bf16, scale.reshape(1, POOL_WIDTH))


def _out_kernel(na_ref, yf_ref, yb_ref, g_ref, pool_ref, x_ref, g1_ref, gnw_ref, avg_ref, wo_ref, o_ref):
    y = yf_ref[0] + yb_ref[0]
    avg = avg_ref[...]
    mu = jnp.dot(y, avg, precision=HIGHEST, preferred_element_type=F32)
    d = y - mu
    var = jnp.dot(d * d, avg, precision=HIGHEST, preferred_element_type=F32)
    yn = d * lax.rsqrt(var + NORM_EPS) * gnw_ref[...]
    g = g_ref[0]
    ret = yn * (g * jax.nn.sigmoid(g))
    mix = jnp.dot(na_ref[0].astype(BF16), wo_ref[0:NA_WIDTH, :], preferred_element_type=F32)
    mix += jnp.dot(ret.astype(BF16), wo_ref[NA_WIDTH:NA_WIDTH + RET_WIDTH, :], preferred_element_type=F32)
    mix += jnp.dot(pool_ref[0].astype(BF16), wo_ref[NA_WIDTH + RET_WIDTH:, :], preferred_element_type=F32)
    o_ref[0] = x_ref[0] + g1_ref[0] * mix


def _out_proj(na, yf, yb, p, pool, x, g1, gn_w, avg, wo_bf16):
    b, t, d = x.shape
    tm = min(512, t)
    row = lambda w: pl.BlockSpec((1, tm, w), lambda i, j: (i, j, 0))
    return pl.pallas_call(
        _out_kernel,
        grid=(b, t // tm),
        in_specs=[row(NA_WIDTH), row(RET_WIDTH), row(RET_WIDTH),
                  pl.BlockSpec((1, tm, RET_WIDTH), lambda i, j: (i, j, O_RET_G // RET_WIDTH)),
                  row(POOL_WIDTH), row(d),
                  pl.BlockSpec((1, 1, d), lambda i, j: (i, 0, 0)),
                  pl.BlockSpec((1, RET_WIDTH), lambda i, j: (0, 0)),
                  pl.BlockSpec((RET_WIDTH, RET_WIDTH), lambda i, j: (0, 0)),
                  pl.BlockSpec((d, d), lambda i, j: (0, 0))],
        out_specs=row(d),
        out_shape=jax.ShapeDtypeStruct((b, t, d), F32),
        compiler_params=_cparams("parallel", "parallel"),
        name="out_proj",
    )(na, yf, yb, p, pool, x, g1, gn_w.reshape(1, RET_WIDTH), avg, wo_bf16)


def _topk_rows(s, payload=None):
    n, m = s.shape
    row = lax.broadcasted_iota(jnp.int32, (n, m), 0)
    krow = lax.broadcasted_iota(jnp.int32, (PEER_TOPK, m), 0)
    vals = jnp.zeros((PEER_TOPK, m), F32)
    idxs = jnp.zeros((PEER_TOPK, m), jnp.int32)
    for k in range(PEER_TOPK):
        mx = jnp.max(s, axis=0, keepdims=True)
        am = jnp.min(jnp.where(s == mx, row, n), axis=0, keepdims=True)
        sel = row == am
        pick = am if payload is None else jnp.sum(jnp.where(sel, payload, 0), axis=0, keepdims=True)
        vals = jnp.where(krow == k, mx, vals)
        idxs = jnp.where(krow == k, pick, idxs)
        s = jnp.where(sel, -jnp.inf, s)
    return vals, idxs


def _route_kernel(x_ref, nw_ref, sh_ref, sc_ref, wq_ref, keys_ref, idx_ref, gate_ref, q_scr, g_scr, i_scr, *, tm):
    h = _rms_mod(x_ref[0], nw_ref[...], sh_ref[0], sc_ref[0])
    q = jnp.dot(h, wq_ref[...], precision=HIGHEST, preferred_element_type=F32)
    for j in range(2 * PEER_HEADS):
        q_scr[j] = q[:, j * PEER_KEY_DIM:(j + 1) * PEER_KEY_DIM]

    def head(hh, carry):
        sa = _dot_nt(keys_ref[0, hh], q_scr[2 * hh], precision=HIGHEST)
        sb = _dot_nt(keys_ref[1, hh], q_scr[2 * hh + 1], precision=HIGHEST)
        va, ia = _topk_rows(sa)
        vb, ib = _topk_rows(sb)
        cand = jnp.concatenate([va[i:i + 1, :] + vb for i in range(PEER_TOPK)], axis=0)
        eid = jnp.concatenate([ia[i:i + 1, :] * PEER_NKEYS + ib for i in range(PEER_TOPK)], axis=0)
        sc, ei = _topk_rows(cand, eid)
        e = jnp.exp(sc - sc[0:1, :])
        off = pl.multiple_of(hh * PEER_TOPK, PEER_TOPK)
        g_scr[pl.ds(off, PEER_TOPK), :] = e / jnp.sum(e, axis=0, keepdims=True)
        i_scr[pl.ds(off, PEER_TOPK), :] = ei
        return carry

    lax.fori_loop(0, PEER_HEADS, head, 0)
    gate_ref[0] = g_scr[...].T
    for j in range(tm // TOK_BLOCK):
        idx_ref[j] = i_scr[:, j * TOK_BLOCK:(j + 1) * TOK_BLOCK]


def _route(x, nw, shift, scale, wq, keys):
    b, t, d = x.shape
    tm = min(256, t)
    nt = t // tm
    per = tm // TOK_BLOCK
    nq = wq.shape[1]
    return pl.pallas_call(
        functools.partial(_route_kernel, tm=tm),
        grid=(b, nt),
        in_specs=[pl.BlockSpec((1, tm, d), lambda i, j: (i, j, 0)),
                  pl.BlockSpec((1, d), lambda i, j: (0, 0)),
                  pl.BlockSpec((1, 1, d), lambda i, j: (i, 0, 0)),
                  pl.BlockSpec((1, 1, d), lambda i, j: (i, 0, 0)),
                  pl.BlockSpec((d, nq), lambda i, j: (0, 0)),
                  pl.BlockSpec((2, PEER_HEADS, PEER_NKEYS, PEER_KEY_DIM), lambda i, j: (0, 0, 0, 0))],
        out_specs=[pl.BlockSpec((per, N_SEL, TOK_BLOCK), lambda i, j: (i * nt + j, 0, 0)),
                   pl.BlockSpec((1, tm, N_SEL), lambda i, j: (i, j, 0))],
        out_shape=[jax.ShapeDtypeStruct((b * t // TOK_BLOCK, N_SEL, TOK_BLOCK), jnp.int32),
                   jax.ShapeDtypeStruct((b, t, N_SEL), F32)],
        scratch_shapes=[pltpu.VMEM((2 * PEER_HEADS, tm, PEER_KEY_DIM), F32),
                        pltpu.VMEM((N_SEL, tm), F32),
                        pltpu.VMEM((N_SEL, tm), jnp.int32)],
        compiler_params=_cparams("parallel", "parallel"),
        name="peer_route",
    )(x, nw.reshape(1, d), shift, scale, wq, keys)


def _sc_gather(table, idx, window=32):
    n = idx.shape[0]
    d = table.shape[1]
    info = plsc.get_sparse_core_info()
    nc, ns = info.num_cores, info.num_subcores
    per_w = n // (nc * ns)
    assert per_w * nc * ns == n and per_w % window == 0
    mesh = plsc.VectorSubcoreMesh(core_axis_name="c", subcore_axis_name="s")

    @functools.partial(
        pl.kernel,
        out_type=jax.ShapeDtypeStruct((n, d), table.dtype),
        mesh=mesh,
        scratch_types=[pltpu.VMEM((window,), jnp.int32),
                       pltpu.VMEM((window, d), table.dtype),
                       pltpu.SemaphoreType.DMA],
    )
    def gather(t_hbm, i_hbm, o_hbm, idx_v, rows_v, sem):
        wid = lax.axis_index("s") * nc + lax.axis_index("c")
        base = wid * per_w

        @pl.loop(0, per_w // window)
        def _(w):
            off = pl.multiple_of(base + w * window, 8)
            pltpu.sync_copy(i_hbm.at[pl.ds(off, window)], idx_v)
            pltpu.async_copy(t_hbm.at[idx_v], rows_v, sem).wait()
            pltpu.sync_copy(rows_v, o_hbm.at[pl.ds(off, window)])

    return gather(table, idx)


def _expert_kernel(gu_ref, gv_ref, x_ref, gate_ref, nw_ref, sh_ref, sc_ref, g2_ref, fw_ref, o_ref,
                   h_scr, acc_scr, *, jb, final):
    jc = pl.program_id(1)

    @pl.when(jc == 0)
    def _():
        h_scr[...] = _rms_mod(x_ref[...], nw_ref[...], sh_ref[0], sc_ref[0])
        acc_scr[...] = jnp.zeros_like(acc_scr)

    h = h_scr[...]
    gate = gate_ref[...]
    lane = lax.broadcasted_iota(jnp.int32, (1, N_SEL), 1)
    acc = acc_scr[...]
    for jj in range(jb):
        a = jnp.sum(gu_ref[0, jj] * h, axis=-1, keepdims=True)
        gcol = jnp.sum(jnp.where(lane == jc * jb + jj, gate, 0.0), axis=-1, keepdims=True)
        act = 0.5 * a * (1.0 + lax.erf(a * (2.0 ** -0.5)))
        acc = acc + (gcol * act) * gv_ref[0, jj]
    acc_scr[...] = acc

    @pl.when(jc == pl.num_programs(1) - 1)
    def _():
        y = x_ref[...] + g2_ref[0] * acc
        if final:
            y = y * lax.rsqrt(jnp.mean(y * y, axis=-1, keepdims=True) + NORM_EPS) * fw_ref[...]
        o_ref[...] = y


def _experts(gu, gv, x2, gate2, nw, shift, scale, g2, final_w, blocks_per_batch, final):
    nblk, nsel, tb, d = gu.shape
    jb = 8
    vec = pl.BlockSpec((1, 1, d), lambda i, j: (0, 0, 0))
    return pl.pallas_call(
        functools.partial(_expert_kernel, jb=jb, final=final),
        grid=(nblk, nsel // jb),
        in_specs=[pl.BlockSpec((1, jb, tb, d), lambda i, j: (i, j, 0, 0)),
                  pl.BlockSpec((1, jb, tb, d), lambda i, j: (i, j, 0, 0)),
                  pl.BlockSpec((tb, d), lambda i, j: (i, 0)),
                  pl.BlockSpec((tb, nsel), lambda i, j: (i, 0)),
                  pl.BlockSpec((1, d), lambda i, j: (0, 0)),
                  vec, vec, vec,
                  pl.BlockSpec((1, d), lambda i, j: (0, 0))],
        out_specs=pl.BlockSpec((tb, d), lambda i, j: (i, 0)),
        out_shape=jax.ShapeDtypeStruct((nblk * tb, d), F32),
        scratch_shapes=[pltpu.VMEM((tb, d), F32), pltpu.VMEM((tb, d), F32)],
        compiler_params=_cparams("parallel", "arbitrary"),
        name="peer_experts",
    )(gu, gv, x2, gate2, nw.reshape(1, d), shift, scale, g2, final_w.reshape(1, d))


def _peer_residual(x, nw, shift, scale, g2, wq, keys, u, v, final_w, final):
    b, t, d = x.shape
    idx, gate = _route(x, nw, shift, scale, wq, keys)
    bpb = t // TOK_BLOCK
    outs = []
    for bi in range(b):
        flat = idx[bi * bpb:(bi + 1) * bpb].reshape(-1)
        gu = _sc_gather(u, flat).reshape(bpb, N_SEL, TOK_BLOCK, d)
        gv = _sc_gather(v, flat).reshape(bpb, N_SEL, TOK_BLOCK, d)
        outs.append(_experts(gu, gv, x[bi], gate[bi], nw, shift[bi:bi + 1], scale[bi:bi + 1], g2[bi:bi + 1],
                             final_w, bpb, final))
    return jnp.stack(outs, axis=0)


def _na_bias_table(rpb):
    cq = jnp.arange(GRID_W)
    coff = jnp.clip(cq[None, :] - cq[:, None] + (NA_WIN_C - 1), 0, 2 * NA_WIN_C - 2)
    c_start = jnp.clip(cq - NA_WIN_C // 2, 0, GRID_W - NA_WIN_C)
    ok = (cq[None, :] >= c_start[:, None]) & (cq[None, :] < c_start[:, None] + NA_WIN_C)
    roff = jnp.arange(NA_WIN_R)[:, None] + jnp.arange(NA_WIN_R)[None, :]
    bias = rpb.astype(F32)[:, roff][..., coff]
    bias = jnp.where(ok[None, None, None], bias, NEG_INF)
    bias = bias.transpose(0, 1, 3, 2, 4).reshape(NA_HEADS, NA_WIN_R, GRID_W, NA_WIN_R * GRID_W)
    return bias.reshape(HEAD_PAIRS, 2, NA_WIN_R, GRID_W, NA_WIN_R * GRID_W)


def _rope_tables(s):
    t = jnp.arange(s)
    row = (t // GRID_W).astype(F32)
    col = (t % GRID_W).astype(F32)
    inv = ROPE_BASE ** (-jnp.arange(ROPE_PAIRS, dtype=F32) / ROPE_PAIRS)
    cr, sr = jnp.cos(row[:, None] * inv), jnp.sin(row[:, None] * inv)
    cc, sc = jnp.cos(col[:, None] * inv), jnp.sin(col[:, None] * inv)
    cos = jnp.concatenate([cr, cr, cc, cc], axis=-1)
    sin = jnp.concatenate([-sr, sr, -sc, sc], axis=-1)
    return jnp.tile(cos, (1, 2)), jnp.tile(sin, (1, 2))


def _block_diag(blocks):
    n, a, bb = blocks.shape
    eye = jnp.eye(n, dtype=blocks.dtype)
    return (eye[:, None, :, None] * blocks[:, :, None, :]).reshape(n * a, n * bb)


def kernel(x, c, ctx, c_ctx, norm1_w, norm2_w, w_ada, b_ada, w_in, w_out, na_rpb, ret_decay_fwd, ret_decay_bwd, ret_gn_w, pool_w, pool_scale, peer_wq, peer_keys, peer_u, peer_v, final_norm_w):
    b, s, d = x.shape
    depth = w_in.shape[0]
    clen = ctx.shape[1]
    cvec = jnp.concatenate([c, c_ctx[None, :], jnp.zeros((8 - b - 1, d), F32)], axis=0)
    mod = _modulation(cvec, w_ada, b_ada)
    cos_x, sin_x = _rope_tables(s)
    cos_c, sin_c = jnp.ones((clen, LANES), F32), jnp.zeros((clen, LANES), F32)
    avg = _block_diag(jnp.full((RET_HEADS, HEAD_DIM, HEAD_DIM), 1.0 / HEAD_DIM, F32))
    zero_state = jnp.zeros((b, HEAD_PAIRS, LANES, LANES), F32)

    for l in range(depth):
        last = l == depth - 1
        mx = mod[l, :b].reshape(b, 1, 6, d)
        mc = jnp.broadcast_to(mod[l, b].reshape(1, 1, 6, d), (b, 1, 6, d))
        sh1, sc1, g1, sh2, sc2, g2 = [mx[:, :, i] for i in range(6)]
        csh1, csc1, cg1, csh2, csc2, cg2 = [mc[:, :, i] for i in range(6)]
        lg = jnp.stack([jax.nn.log_sigmoid(ret_decay_fwd[l].astype(F32)),
                        jax.nn.log_sigmoid(ret_decay_bwd[l].astype(F32))], axis=0)
        wi = w_in[l].astype(BF16)
        wo = w_out[l].astype(BF16)
        wpool = _block_diag(pool_w[l]).astype(BF16)
        bias8 = _na_bias_table(na_rpb[l])

        pc, pcp = _in_proj(ctx, norm1_w[l], csh1, csc1, wi)
        ycf, ycb, r_f, r_b = _retention(pc, lg, cos_c, sin_c, zero_state, zero_state)

        px, pxp = _in_proj(x, norm1_w[l], sh1, sc1, wi)
        na = _na_attention(px, pc, bias8)
        yf, yb, _, _ = _retention(px, lg, cos_x, sin_x, r_f, r_b)
        pool = _pool(pxp, wpool, pool_scale[l])
        x = _out_proj(na, yf, yb, px, pool, x, g1, ret_gn_w[l], avg, wo)
        x = _peer_residual(x, norm2_w[l], sh2, sc2, g2, peer_wq[l], peer_keys[l], peer_u[l], peer_v[l],
                           final_norm_w, last)

        if not last:
            na_c = _ctx_attention(pc)
            pool_c = _pool(pcp, wpool, pool_scale[l])
            ctx = _out_proj(na_c, ycf, ycb, pc, pool_c, ctx, cg1, ret_gn_w[l], avg, wo)
            ctx = _peer_residual(ctx, norm2_w[l], csh2, csc2, cg2, peer_wq[l], peer_keys[l], peer_u[l],
                                 peer_v[l], final_norm_w, False)
    return x
```

```python
import functools

import jax
import jax.numpy as jnp
from jax import lax
from jax.experimental import pallas as pl
from jax.experimental.pallas import tpu as pltpu

D_MODEL = 1024
GRID_W = 64
HEAD_DIM = 64
NA_HEADS = 6
NA_WIN_R = 8
NA_WIN_C = 16
RET_HEADS = 6
RET_CHUNK = 128
POOL_WINDOWS = (2, 4, 8, 16)
POOL_GROUP = 64
NA_WIDTH = NA_HEADS * HEAD_DIM
RET_WIDTH = RET_HEADS * HEAD_DIM
POOL_WIDTH = POOL_GROUP * len(POOL_WINDOWS)
O_RET_G = 3 * NA_WIDTH + 3 * RET_WIDTH
O_POOL = O_RET_G + RET_WIDTH
D_PROJ = O_POOL + POOL_WIDTH
ROPE_BASE = 10000.0
ROPE_PAIRS = HEAD_DIM // 4
PEER_HEADS = 8
PEER_NKEYS = 128
PEER_KEY_DIM = 128
PEER_TOPK = 16
N_SEL = PEER_HEADS * PEER_TOPK
NORM_EPS = 1e-6
NEG_INF = -1e30

LANES = 128
VMEM_LIMIT_BYTES = 56 * 1024 * 1024

F32 = jnp.float32
BF16 = jnp.bfloat16
HIGHEST = lax.Precision.HIGHEST
HEAD_PAIRS = NA_HEADS // 2
TOK_BLOCK = 128
CHUNKS = D_MODEL // LANES
TABLE_ROWS_PER_EXPERT = CHUNKS // 2


def _cparams(*sem):
    return pltpu.CompilerParams(dimension_semantics=sem, vmem_limit_bytes=VMEM_LIMIT_BYTES)


def _dot_nt(a, b, precision=None):
    return lax.dot_general(a, b, (((1,), (1,)), ((), ())), precision=precision,
                           preferred_element_type=F32)


def _rms_mod(x, nw, shift, scale):
    y = x * lax.rsqrt(jnp.mean(x * x, axis=-1, keepdims=True) + NORM_EPS)
    return (y * nw) * (1.0 + scale) + shift


def _mod_kernel(c_ref, w_ref, b_ref, o_ref):
    c = c_ref[...]
    a = c * jax.nn.sigmoid(c)
    o_ref[0] = jnp.dot(a, w_ref[0], precision=HIGHEST, preferred_element_type=F32) + b_ref[0]


def _modulation(cvec, w_ada, b_ada):
    depth, d, n = w_ada.shape
    tn = 1536
    return pl.pallas_call(
        _mod_kernel,
        grid=(depth, n // tn),
        in_specs=[pl.BlockSpec((8, d), lambda l, j: (0, 0)),
                  pl.BlockSpec((1, d, tn), lambda l, j: (l, 0, j)),
                  pl.BlockSpec((1, 1, tn), lambda l, j: (l, 0, j))],
        out_specs=pl.BlockSpec((1, 8, tn), lambda l, j: (l, 0, j)),
        out_shape=jax.ShapeDtypeStruct((depth, 8, n), F32),
        compiler_params=_cparams("parallel", "parallel"),
        name="adaln_mod",
    )(cvec, w_ada, b_ada.reshape(depth, 1, n))


def _inproj_kernel(x_ref, nw_ref, sh_ref, sc_ref, w_ref, o_ref, p_ref):
    h = _rms_mod(x_ref[0], nw_ref[...], sh_ref[0], sc_ref[0])
    r = jnp.dot(h.astype(BF16), w_ref[...], preferred_element_type=F32)
    o_ref[0] = r[:, :O_POOL]
    p_ref[0] = r[:, O_POOL:]


def _in_proj(x, nw, shift, scale, w_bf16):
    b, t, d = x.shape
    tm = min(512, t)
    return pl.pallas_call(
        _inproj_kernel,
        grid=(b, t // tm),
        in_specs=[pl.BlockSpec((1, tm, d), lambda i, j: (i, j, 0)),
                  pl.BlockSpec((1, d), lambda i, j: (0, 0)),
                  pl.BlockSpec((1, 1, d), lambda i, j: (i, 0, 0)),
                  pl.BlockSpec((1, 1, d), lambda i, j: (i, 0, 0)),
                  pl.BlockSpec((d, D_PROJ), lambda i, j: (0, 0))],
        out_specs=[pl.BlockSpec((1, tm, O_POOL), lambda i, j: (i, j, 0)),
                   pl.BlockSpec((1, tm, POOL_WIDTH), lambda i, j: (i, j, 0))],
        out_shape=[jax.ShapeDtypeStruct((b, t, O_POOL), F32),
                   jax.ShapeDtypeStruct((b, t, POOL_WIDTH), F32)],
        compiler_params=_cparams("parallel", "parallel"),
        name="in_proj",
    )(x, nw.reshape(1, d), shift, scale, w_bf16)


def _softmax_pv(s_list, v_list):
    m = s_list[0].max(axis=-1, keepdims=True)
    for s in s_list[1:]:
        m = jnp.maximum(m, s.max(axis=-1, keepdims=True))
    num = None
    den = None
    for s, v in zip(s_list, v_list):
        p = jnp.exp(s - m)
        pv = jnp.dot(p, v, preferred_element_type=F32)
        ps = p.sum(axis=-1, keepdims=True)
        num = pv if num is None else num + pv
        den = ps if den is None else den + ps
    return num / den


def _na_kernel(q_ref, k_ref, v_ref, kc_ref, vc_ref, bias_ref, o_ref, *, rows, rb):
    i = pl.program_id(2)
    lane = lax.broadcasted_iota(jnp.int32, (1, LANES), 1)
    first = lane < HEAD_DIM
    kc = kc_ref[0]
    vc = vc_ref[0]
    scale = HEAD_DIM ** -0.5
    nk = NA_WIN_R * GRID_W

    def body(rr, carry):
        r = i * rb + rr
        rs = jnp.clip(r - NA_WIN_R // 2, 0, rows - NA_WIN_R)
        delta = rs - r + (NA_WIN_R - 1)
        q = q_ref[0, pl.ds(pl.multiple_of(rr * GRID_W, GRID_W), GRID_W), :]
        k = k_ref[0, pl.ds(pl.multiple_of(rs * GRID_W, GRID_W), nk), :]
        v = v_ref[0, pl.ds(pl.multiple_of(rs * GRID_W, GRID_W), nk), :]
        outs = []
        for h in range(2):
            qm = jnp.where(first if h == 0 else jnp.logical_not(first), q, 0.0)
            s = _dot_nt(qm, k) * scale + bias_ref[0, h, delta]
            sc = _dot_nt(qm, kc) * scale
            outs.append(_softmax_pv([s, sc], [v, vc]))
        o_ref[0, pl.ds(pl.multiple_of(rr * GRID_W, GRID_W), GRID_W), :] = jnp.where(first, outs[0], outs[1])
        return carry

    lax.fori_loop(0, rb, body, 0)


def _na_attention(px, pc, bias8):
    b, s, _ = px.shape
    c = pc.shape[1]
    rows = s // GRID_W
    assert rows >= NA_WIN_R and s % GRID_W == 0
    rb = min(8, rows)
    nq, nkb, nvb = 0, HEAD_PAIRS, 2 * HEAD_PAIRS
    return pl.pallas_call(
        functools.partial(_na_kernel, rows=rows, rb=rb),
        grid=(b, HEAD_PAIRS, rows // rb),
        in_specs=[pl.BlockSpec((1, rb * GRID_W, LANES), lambda bi, hp, i: (bi, i, nq + hp)),
                  pl.BlockSpec((1, s, LANES), lambda bi, hp, i: (bi, 0, nkb + hp)),
                  pl.BlockSpec((1, s, LANES), lambda bi, hp, i: (bi, 0, nvb + hp)),
                  pl.BlockSpec((1, c, LANES), lambda bi, hp, i: (bi, 0, nkb + hp)),
                  pl.BlockSpec((1, c, LANES), lambda bi, hp, i: (bi, 0, nvb + hp)),
                  pl.BlockSpec((1, 2, NA_WIN_R, GRID_W, NA_WIN_R * GRID_W),
                               lambda bi, hp, i: (hp, 0, 0, 0, 0))],
        out_specs=pl.BlockSpec((1, rb * GRID_W, LANES), lambda bi, hp, i: (bi, i, hp)),
        out_shape=jax.ShapeDtypeStruct((b, s, NA_WIDTH), F32),
        compiler_params=_cparams("parallel", "parallel", "arbitrary"),
        name="na_attention",
    )(px, px, px, pc, pc, bias8)


def _ctx_attn_kernel(q_ref, k_ref, v_ref, o_ref):
    lane = lax.broadcasted_iota(jnp.int32, (1, LANES), 1)
    first = lane < HEAD_DIM
    q = q_ref[0]
    k = k_ref[0]
    v = v_ref[0]
    outs = []
    for h in range(2):
        qm = jnp.where(first if h == 0 else jnp.logical_not(first), q, 0.0)
        s = _dot_nt(qm, k) * HEAD_DIM ** -0.5
        outs.append(_softmax_pv([s], [v]))
    o_ref[0] = jnp.where(first, outs[0], outs[1])


def _ctx_attention(pc):
    b, c, _ = pc.shape
    return pl.pallas_call(
        _ctx_attn_kernel,
        grid=(b, HEAD_PAIRS),
        in_specs=[pl.BlockSpec((1, c, LANES), lambda bi, hp: (bi, 0, hp)),
                  pl.BlockSpec((1, c, LANES), lambda bi, hp: (bi, 0, HEAD_PAIRS + hp)),
                  pl.BlockSpec((1, c, LANES), lambda bi, hp: (bi, 0, 2 * HEAD_PAIRS + hp))],
        out_specs=pl.BlockSpec((1, c, LANES), lambda bi, hp: (bi, 0, hp)),
        out_shape=jax.ShapeDtypeStruct((b, c, NA_WIDTH), F32),
        compiler_params=_cparams("parallel", "parallel"),
        name="ctx_attention",
    )(pc, pc, pc)


def _ret_kernel(lg_ref, qf_ref, kf_ref, vf_ref, qb_ref, kb_ref, vb_ref,
                cf_ref, sf_ref, cb_ref, sb_ref, rf0_ref, rb0_ref,
                yf_ref, yb_ref, rfo_ref, rbo_ref, rf_scr, rb_scr):
    hp = pl.program_id(1)
    c = pl.program_id(2)
    nc = pl.num_programs(2)
    cs = RET_CHUNK

    @pl.when(c == 0)
    def _():
        rf_scr[...] = rf0_ref[0, 0]
        rb_scr[...] = rb0_ref[0, 0]

    lane = lax.broadcasted_iota(jnp.int32, (1, LANES), 1)
    first = lane < HEAD_DIM
    low = (lane % (2 * ROPE_PAIRS)) < ROPE_PAIRS
    pos = lax.broadcasted_iota(jnp.int32, (cs, 1), 0).astype(F32)
    ii = lax.broadcasted_iota(jnp.int32, (cs, cs), 0)
    jj = lax.broadcasted_iota(jnp.int32, (cs, cs), 1)
    diff = (ii - jj).astype(F32)
    same_head = (ii < HEAD_DIM) == (jj < HEAD_DIM)
    scale = HEAD_DIM ** -0.5

    def rope(x, cos, sin):
        swapped = jnp.where(low, pltpu.roll(x, LANES - ROPE_PAIRS, 1), pltpu.roll(x, ROPE_PAIRS, 1))
        return x * cos + swapped * sin

    def direction(d, q_ref, k_ref, v_ref, cos_ref, sin_ref, r_scr, y_ref):
        lg0 = lg_ref[d, 2 * hp]
        lg1 = lg_ref[d, 2 * hp + 1]
        lgv = jnp.where(first, lg0, lg1)
        cos = cos_ref[...]
        sin = sin_ref[...]
        q = rope(q_ref[0], cos, sin)
        k = rope(k_ref[0], cos, sin) * scale
        v = v_ref[0]
        outs = []
        for h, lg in ((0, lg0), (1, lg1)):
            if d == 0:
                dec = jnp.where(diff >= 0, jnp.exp(jnp.maximum(diff, 0.0) * lg), 0.0)
            else:
                dec = jnp.where(diff <= 0, jnp.exp(jnp.maximum(-diff, 0.0) * lg), 0.0)
            qm = jnp.where(first if h == 0 else jnp.logical_not(first), q, 0.0)
            outs.append(jnp.dot(_dot_nt(qm, k) * dec, v, preferred_element_type=F32))
        inner = jnp.where(first, outs[0], outs[1])
        if d == 0:
            xi = jnp.exp((pos + 1.0) * lgv)
            zeta = jnp.exp((cs - 1.0 - pos) * lgv)
        else:
            xi = jnp.exp((cs - pos) * lgv)
            zeta = jnp.exp(pos * lgv)
        r = r_scr[...]
        y_ref[0] = inner + jnp.dot(q * xi, r, preferred_element_type=F32)
        kv = lax.dot_general(k * zeta, v, (((0,), (0,)), ((), ())), preferred_element_type=F32)
        r_scr[...] = jnp.exp(cs * lgv) * r + jnp.where(same_head, kv, 0.0)

    direction(0, qf_ref, kf_ref, vf_ref, cf_ref, sf_ref, rf_scr, yf_ref)
    direction(1, qb_ref, kb_ref, vb_ref, cb_ref, sb_ref, rb_scr, yb_ref)

    @pl.when(c == nc - 1)
    def _():
        rfo_ref[0, 0] = rf_scr[...]
        rbo_ref[0, 0] = rb_scr[...]


def _retention(p, lg, cos_t, sin_t, rf0, rb0):
    b, t, _ = p.shape
    nc = t // RET_CHUNK
    qo, ko, vo = 3 * HEAD_PAIRS, 4 * HEAD_PAIRS, 5 * HEAD_PAIRS
    blk = (1, RET_CHUNK, LANES)
    fwd = lambda o: pl.BlockSpec(blk, lambda bi, hp, c: (bi, c, o + hp))
    bwd = lambda o: pl.BlockSpec(blk, lambda bi, hp, c: (bi, nc - 1 - c, o + hp))
    tab_f = pl.BlockSpec((RET_CHUNK, LANES), lambda bi, hp, c: (c, 0))
    tab_b = pl.BlockSpec((RET_CHUNK, LANES), lambda bi, hp, c: (nc - 1 - c, 0))
    st = pl.BlockSpec((1, 1, LANES, LANES), lambda bi, hp, c: (bi, hp, 0, 0))
    return pl.pallas_call(
        _ret_kernel,
        grid=(b, HEAD_PAIRS, nc),
        in_specs=[pl.BlockSpec(memory_space=pltpu.SMEM),
                  fwd(qo), fwd(ko), fwd(vo), bwd(qo), bwd(ko), bwd(vo),
                  tab_f, tab_f, tab_b, tab_b, st, st],
        out_specs=[pl.BlockSpec(blk, lambda bi, hp, c: (bi, c, hp)),
                   pl.BlockSpec(blk, lambda bi, hp, c: (bi, nc - 1 - c, hp)),
                   st, st],
        out_shape=[jax.ShapeDtypeStruct((b, t, RET_WIDTH), F32),
                   jax.ShapeDtypeStruct((b, t, RET_WIDTH), F32),
                   jax.ShapeDtypeStruct((b, HEAD_PAIRS, LANES, LANES), F32),
                   jax.ShapeDtypeStruct((b, HEAD_PAIRS, LANES, LANES), F32)],
        scratch_shapes=[pltpu.VMEM((LANES, LANES), F32), pltpu.VMEM((LANES, LANES), F32)],
        compiler_params=_cparams("parallel", "parallel", "arbitrary"),
        name="retention",
    )(lg, p, p, p, p, p, p, cos_t, sin_t, cos_t, sin_t, rf0, rb0)


def _pool_kernel(prev_ref, cur_ref, next_ref, w_ref, s_ref, o_ref, scr, *, t_total, tp):
    i = pl.program_id(1)
    n = pl.num_programs(1)
    halo = POOL_WINDOWS[-1] // 2
    x = cur_ref[0]
    scr[0:halo, :] = jnp.where(i > 0, prev_ref[0], 0.0)
    scr[halo:halo + tp, :] = x
    scr[halo + tp:2 * halo + tp, :] = jnp.where(i < n - 1, next_ref[0], 0.0)
    t = i * tp + lax.broadcasted_iota(jnp.int32, (tp, 1), 0)
    lane = lax.broadcasted_iota(jnp.int32, (1, POOL_WIDTH), 1)

    def shifted(s):
        return scr[halo + s:halo + s + tp, :]

    acc = None
    mean = None
    done = 0
    for g, w in enumerate(POOL_WINDOWS):
        half = w // 2
        for s in list(range(-half, -done)) + list(range(done, half)):
            sh = x if s == 0 else shifted(s)
            acc = sh if acc is None else acc + sh
        done = half
        cnt = (jnp.minimum(t + half, t_total) - jnp.maximum(t - half, 0)).astype(F32)
        mg = acc / cnt
        mean = mg if mean is None else jnp.where(lane >= g * POOL_GROUP, mg, mean)
    dlt = (mean - x).astype(BF16)
    o_ref[0] = jnp.dot(dlt, w_ref[...], preferred_element_type=F32) * s_ref[...]


def _pool(pin, w_bd_bf16, scale):
    b, t, _ = pin.shape
    tp = min(1024, t)
    halo = POOL_WINDOWS[-1] // 2
    nh = tp // halo
    last = t // halo - 1
    return pl.pallas_call(
        functools.partial(_pool_kernel, t_total=t, tp=tp),
        grid=(b, t // tp),
        in_specs=[pl.BlockSpec((1, halo, POOL_WIDTH), lambda bi, i: (bi, jnp.maximum(i * nh - 1, 0), 0)),
                  pl.BlockSpec((1, tp, POOL_WIDTH), lambda bi, i: (bi, i, 0)),
                  pl.BlockSpec((1, halo, POOL_WIDTH), lambda bi, i: (bi, jnp.minimum((i + 1) * nh, last), 0)),
                  pl.BlockSpec((POOL_WIDTH, POOL_WIDTH), lambda bi, i: (0, 0)),
                  pl.BlockSpec((1, POOL_WIDTH), lambda bi, i: (0, 0))],
        out_specs=pl.BlockSpec((1, tp, POOL_WIDTH), lambda bi, i: (bi, i, 0)),
        out_shape=jax.ShapeDtypeStruct((b, t, POOL_WIDTH), F32),
        scratch_shapes=[pltpu.VMEM((tp + 2 * halo, POOL_WIDTH), F32)],
        compiler_params=_cparams("parallel", "parallel"),
        name="multiscale_pool",
    )(pin, pin, pin, w_bd_bf16, scale.reshape(1, POOL_WIDTH))


def _out_kernel(na_ref, yf_ref, yb_ref, g_ref, pool_ref, x_ref, g1_ref, gnw_ref, avg_ref, wo_ref, o_ref):
    y = yf_ref[0] + yb_ref[0]
    avg = avg_ref[...]
    mu = jnp.dot(y, avg, precision=HIGHEST, preferred_element_type=F32)
    d = y - mu
    var = jnp.dot(d * d, avg, precision=HIGHEST, preferred_element_type=F32)
    yn = d * lax.rsqrt(var + NORM_EPS) * gnw_ref[...]
    g = g_ref[0]
    ret = yn * (g * jax.nn.sigmoid(g))
    mix = jnp.dot(na_ref[0].astype(BF16), wo_ref[0:NA_WIDTH, :], preferred_element_type=F32)
    mix += jnp.dot(ret.astype(BF16), wo_ref[NA_WIDTH:NA_WIDTH + RET_WIDTH, :], preferred_element_type=F32)
    mix += jnp.dot(pool_ref[0].astype(BF16), wo_ref[NA_WIDTH + RET_WIDTH:, :], preferred_element_type=F32)
    o_ref[0] = x_ref[0] + g1_ref[0] * mix


def _out_proj(na, yf, yb, p, pool, x, g1, gn_w, avg, wo_bf16):
    b, t, d = x.shape
    tm = min(512, t)
    row = lambda w: pl.BlockSpec((1, tm, w), lambda i, j: (i, j, 0))
    return pl.pallas_call(
        _out_kernel,
        grid=(b, t // tm),
        in_specs=[row(NA_WIDTH), row(RET_WIDTH), row(RET_WIDTH),
                  pl.BlockSpec((1, tm, RET_WIDTH), lambda i, j: (i, j, O_RET_G // RET_WIDTH)),
                  row(POOL_WIDTH), row(d),
                  pl.BlockSpec((1, 1, d), lambda i, j: (i, 0, 0)),
                  pl.BlockSpec((1, RET_WIDTH), lambda i, j: (0, 0)),
                  pl.BlockSpec((RET_WIDTH, RET_WIDTH), lambda i, j: (0, 0)),
                  pl.BlockSpec((d, d), lambda i, j: (0, 0))],
        out_specs=row(d),
        out_shape=jax.ShapeDtypeStruct((b, t, d), F32),
        compiler_params=_cparams("parallel", "parallel"),
        name="out_proj",
    )(na, yf, yb, p, pool, x, g1, gn_w.reshape(1, RET_WIDTH), avg, wo_bf16)


def _topk_rows(s, payload=None):
    n, m = s.shape
    row = lax.broadcasted_iota(jnp.int32, (n, m), 0)
    krow = lax.broadcasted_iota(jnp.int32, (PEER_TOPK, m), 0)
    vals = jnp.zeros((PEER_TOPK, m), F32)
    idxs = jnp.zeros((PEER_TOPK, m), jnp.int32)
    for k in range(PEER_TOPK):
        mx = jnp.max(s, axis=0, keepdims=True)
        am = jnp.min(jnp.where(s == mx, row, n), axis=0, keepdims=True)
        sel = row == am
        pick = am if payload is None else jnp.sum(jnp.where(sel, payload, 0), axis=0, keepdims=True)
        vals = jnp.where(krow == k, mx, vals)
        idxs = jnp.where(krow == k, pick, idxs)
        s = jnp.where(sel, -jnp.inf, s)
    return vals, idxs


def _route_kernel(x_ref, nw_ref, sh_ref, sc_ref, wq_ref, keys_ref, idx_ref, gate_ref, q_scr, g_scr, i_scr, *, tm):
    h = _rms_mod(x_ref[0], nw_ref[...], sh_ref[0], sc_ref[0])
    q = jnp.dot(h, wq_ref[...], precision=HIGHEST, preferred_element_type=F32)
    for j in range(2 * PEER_HEADS):
        q_scr[j] = q[:, j * PEER_KEY_DIM:(j + 1) * PEER_KEY_DIM]

    def head(hh, carry):
        sa = _dot_nt(keys_ref[0, hh], q_scr[2 * hh], precision=HIGHEST)
        sb = _dot_nt(keys_ref[1, hh], q_scr[2 * hh + 1], precision=HIGHEST)
        va, ia = _topk_rows(sa)
        vb, ib = _topk_rows(sb)
        cand = jnp.concatenate([va[i:i + 1, :] + vb for i in range(PEER_TOPK)], axis=0)
        eid = jnp.concatenate([ia[i:i + 1, :] * PEER_NKEYS + ib for i in range(PEER_TOPK)], axis=0)
        sc, ei = _topk_rows(cand, eid)
        e = jnp.exp(sc - sc[0:1, :])
        off = pl.multiple_of(hh * PEER_TOPK, PEER_TOPK)
        g_scr[pl.ds(off, PEER_TOPK), :] = e / jnp.sum(e, axis=0, keepdims=True)
        i_scr[pl.ds(off, PEER_TOPK), :] = ei
        return carry

    lax.fori_loop(0, PEER_HEADS, head, 0)
    gate_ref[0] = g_scr[...].T
    idx_ref[0] = i_scr[...].T * TABLE_ROWS_PER_EXPERT


def _route(x, nw, shift, scale, wq, keys):
    b, t, d = x.shape
    tm = min(256, t)
    nt = t // tm
    nq = wq.shape[1]
    return pl.pallas_call(
        functools.partial(_route_kernel, tm=tm),
        grid=(b, nt),
        in_specs=[pl.BlockSpec((1, tm, d), lambda i, j: (i, j, 0)),
                  pl.BlockSpec((1, d), lambda i, j: (0, 0)),
                  pl.BlockSpec((1, 1, d), lambda i, j: (i, 0, 0)),
                  pl.BlockSpec((1, 1, d), lambda i, j: (i, 0, 0)),
                  pl.BlockSpec((d, nq), lambda i, j: (0, 0)),
                  pl.BlockSpec((2, PEER_HEADS, PEER_NKEYS, PEER_KEY_DIM), lambda i, j: (0, 0, 0, 0))],
        out_specs=[pl.BlockSpec((1, tm, N_SEL), lambda i, j: (i, j, 0)),
                   pl.BlockSpec((1, tm, N_SEL), lambda i, j: (i, j, 0))],
        out_shape=[jax.ShapeDtypeStruct((b, t, N_SEL), jnp.int32),
                   jax.ShapeDtypeStruct((b, t, N_SEL), F32)],
        scratch_shapes=[pltpu.VMEM((2 * PEER_HEADS, tm, PEER_KEY_DIM), F32),
                        pltpu.VMEM((N_SEL, tm), F32),
                        pltpu.VMEM((N_SEL, tm), jnp.int32)],
        compiler_params=_cparams("parallel", "parallel"),
        name="peer_route",
    )(x, nw.reshape(1, d), shift, scale, wq, keys)


def _pack_table(tab):
    n, d = tab.shape
    bits = lax.bitcast_convert_type(tab.astype(BF16), jnp.uint16).astype(jnp.uint32)
    words = bits[:, :d // 2] | (bits[:, d // 2:] << 16)
    return words.reshape(n * TABLE_ROWS_PER_EXPERT, LANES)


def _unpack(slab):
    lo = lax.bitcast_convert_type(slab << 16, F32)
    hi = lax.bitcast_convert_type(slab & jnp.uint32(0xFFFF0000), F32)
    return lo, hi


def _peer_act_kernel(idx_ref, tab_ref, x_ref, nw_ref, sh_ref, sc_ref, gate_ref, w_ref, h_scr, act_scr, *, tb):
    h = _rms_mod(x_ref[...], nw_ref[...], sh_ref[0], sc_ref[0])
    for c in range(CHUNKS):
        h_scr[pl.ds(c, tb, stride=CHUNKS), :] = h[:, c * LANES:(c + 1) * LANES]
    lane = lax.broadcasted_iota(jnp.int32, (1, N_SEL), 1)
    half = CHUNKS // 2

    def token(t, carry):
        ht = h_scr[pl.ds(pl.multiple_of(t * CHUNKS, CHUNKS), CHUNKS), :]
        hlo = ht[0:half]
        hhi = ht[half:CHUNKS]
        acc = jnp.zeros((half, N_SEL), F32)
        for s in range(N_SEL):
            row = pl.multiple_of(idx_ref[t * N_SEL + s], TABLE_ROWS_PER_EXPERT)
            lo, hi = _unpack(tab_ref[pl.ds(row, TABLE_ROWS_PER_EXPERT), :])
            dot = jnp.sum(lo * hlo + hi * hhi, axis=-1, keepdims=True)
            acc = jnp.where(lane == s, dot, acc)
        act_scr[pl.ds(t, 1), :] = jnp.sum(acc, axis=0, keepdims=True)
        return carry

    lax.fori_loop(0, tb, token, 0)
    a = act_scr[...]
    w_ref[...] = gate_ref[...] * (0.5 * a * (1.0 + lax.erf(a * (2.0 ** -0.5))))


def _peer_out_kernel(idx_ref, w_ref, tab_ref, x_ref, g2_ref, fw_ref, o_ref, p_scr, *, tb, final):
    half = CHUNKS // 2

    def token(t, carry):
        accs = [jnp.zeros((half, LANES), F32) for _ in range(4)]
        for s in range(N_SEL):
            row = pl.multiple_of(idx_ref[t * N_SEL + s], TABLE_ROWS_PER_EXPERT)
            w = w_ref[t * N_SEL + s]
            lo, hi = _unpack(tab_ref[pl.ds(row, TABLE_ROWS_PER_EXPERT), :])
            k = 2 * (s % 2)
            accs[k] = accs[k] + w * lo
            accs[k + 1] = accs[k + 1] + w * hi
        p_scr[pl.ds(pl.multiple_of(t * CHUNKS, CHUNKS), half), :] = accs[0] + accs[2]
        p_scr[pl.ds(pl.multiple_of(t * CHUNKS + half, half), half), :] = accs[1] + accs[3]
        return carry

    lax.fori_loop(0, tb, token, 0)
    peer = jnp.concatenate([p_scr[pl.ds(c, tb, stride=CHUNKS), :] for c in range(CHUNKS)], axis=1)
    y = x_ref[...] + g2_ref[0] * peer
    if final:
        y = y * lax.rsqrt(jnp.mean(y * y, axis=-1, keepdims=True) + NORM_EPS) * fw_ref[...]
    o_ref[...] = y


def _peer_residual(x, nw, shift, scale, g2, wq, keys, u, v, final_w, final):
    b, t, d = x.shape
    assert d == CHUNKS * LANES
    idx, gate = _route(x, nw, shift, scale, wq, keys)
    n = b * t
    tb = TOK_BLOCK
    per_batch = t // tb
    x2 = x.reshape(n, d)
    idx1 = idx.reshape(n * N_SEL)
    smem_blk = pl.BlockSpec((tb * N_SEL,), lambda i: (i,), memory_space=pltpu.SMEM)
    table = pl.BlockSpec((u.shape[0] * TABLE_ROWS_PER_EXPERT, LANES), lambda i: (0, 0),
                         pipeline_mode=pl.Buffered(1))
    rows = pl.BlockSpec((tb, d), lambda i: (i, 0))
    sel = pl.BlockSpec((tb, N_SEL), lambda i: (i, 0))
    vec = pl.BlockSpec((1, 1, d), lambda i: (i // per_batch, 0, 0))
    const = pl.BlockSpec((1, d), lambda i: (0, 0))
    w = pl.pallas_call(
        functools.partial(_peer_act_kernel, tb=tb),
        grid=(n // tb,),
        in_specs=[smem_blk, table, rows, const, vec, vec, sel],
        out_specs=sel,
        out_shape=jax.ShapeDtypeStruct((n, N_SEL), F32),
        scratch_shapes=[pltpu.VMEM((tb * CHUNKS, LANES), F32), pltpu.VMEM((tb, N_SEL), F32)],
        compiler_params=_cparams("arbitrary"),
        name="peer_act",
    )(idx1, _pack_table(u), x2, nw.reshape(1, d), shift, scale, gate.reshape(n, N_SEL))
    y = pl.pallas_call(
        functools.partial(_peer_out_kernel, tb=tb, final=final),
        grid=(n // tb,),
        in_specs=[smem_blk, smem_blk, table, rows, vec, const],
        out_specs=rows,
        out_shape=jax.ShapeDtypeStruct((n, d), F32),
        scratch_shapes=[pltpu.VMEM((tb * CHUNKS, LANES), F32)],
        compiler_params=_cparams("arbitrary"),
        name="peer_out",
    )(idx1, w.reshape(n * N_SEL), _pack_table(v), x2, g2, final_w.reshape(1, d))
    return y.reshape(b, t, d)


def _na_bias_table(rpb):
    cq = jnp.arange(GRID_W)
    coff = jnp.clip(cq[None, :] - cq[:, None] + (NA_WIN_C - 1), 0, 2 * NA_WIN_C - 2)
    c_start = jnp.clip(cq - NA_WIN_C // 2, 0, GRID_W - NA_WIN_C)
    ok = (cq[None, :] >= c_start[:, None]) & (cq[None, :] < c_start[:, None] + NA_WIN_C)
    roff = jnp.arange(NA_WIN_R)[:, None] + jnp.arange(NA_WIN_R)[None, :]
    bias = rpb.astype(F32)[:, roff][..., coff]
    bias = jnp.where(ok[None, None, None], bias, NEG_INF)
    bias = bias.transpose(0, 1, 3, 2, 4).reshape(NA_HEADS, NA_WIN_R, GRID_W, NA_WIN_R * GRID_W)
    return bias.reshape(HEAD_PAIRS, 2, NA_WIN_R, GRID_W, NA_WIN_R * GRID_W)


def _rope_tables(s):
    t = jnp.arange(s)
    row = (t // GRID_W).astype(F32)
    col = (t % GRID_W).astype(F32)
    inv = ROPE_BASE ** (-jnp.arange(ROPE_PAIRS, dtype=F32) / ROPE_PAIRS)
    cr, sr = jnp.cos(row[:, None] * inv), jnp.sin(row[:, None] * inv)
    cc, sc = jnp.cos(col[:, None] * inv), jnp.sin(col[:, None] * inv)
    cos = jnp.concatenate([cr, cr, cc, cc], axis=-1)
    sin = jnp.concatenate([-sr, sr, -sc, sc], axis=-1)
    return jnp.tile(cos, (1, 2)), jnp.tile(sin, (1, 2))


def _block_diag(blocks):
    n, a, bb = blocks.shape
    eye = jnp.eye(n, dtype=blocks.dtype)
    return (eye[:, None, :, None] * blocks[:, :, None, :]).reshape(n * a, n * bb)


def kernel(x, c, ctx, c_ctx, norm1_w, norm2_w, w_ada, b_ada, w_in, w_out, na_rpb, ret_decay_fwd, ret_decay_bwd, ret_gn_w, pool_w, pool_scale, peer_wq, peer_keys, peer_u, peer_v, final_norm_w):
    b, s, d = x.shape
    depth = w_in.shape[0]
    clen = ctx.shape[1]
    cvec = jnp.concatenate([c, c_ctx[None, :], jnp.zeros((8 - b - 1, d), F32)], axis=0)
    mod = _modulation(cvec, w_ada, b_ada)
    cos_x, sin_x = _rope_tables(s)
    cos_c, sin_c = jnp.ones((clen, LANES), F32), jnp.zeros((clen, LANES), F32)
    avg = _block_diag(jnp.full((RET_HEADS, HEAD_DIM, HEAD_DIM), 1.0 / HEAD_DIM, F32))
    zero_state = jnp.zeros((b, HEAD_PAIRS, LANES, LANES), F32)

    for l in range(depth):
        last = l == depth - 1
        mx = mod[l, :b].reshape(b, 1, 6, d)
        mc = jnp.broadcast_to(mod[l, b].reshape(1, 1, 6, d), (b, 1, 6, d))
        sh1, sc1, g1, sh2, sc2, g2 = [mx[:, :, i] for i in range(6)]
        csh1, csc1, cg1, csh2, csc2, cg2 = [mc[:, :, i] for i in range(6)]
        lg = jnp.stack([jax.nn.log_sigmoid(ret_decay_fwd[l].astype(F32)),
                        jax.nn.log_sigmoid(ret_decay_bwd[l].astype(F32))], axis=0)
        wi = w_in[l].astype(BF16)
        wo = w_out[l].astype(BF16)
        wpool = _block_diag(pool_w[l]).astype(BF16)
        bias8 = _na_bias_table(na_rpb[l])

        pc, pcp = _in_proj(ctx, norm1_w[l], csh1, csc1, wi)
        ycf, ycb, r_f, r_b = _retention(pc, lg, cos_c, sin_c, zero_state, zero_state)

        px, pxp = _in_proj(x, norm1_w[l], sh1, sc1, wi)
        na = _na_attention(px, pc, bias8)
        yf, yb, _, _ = _retention(px, lg, cos_x, sin_x, r_f, r_b)
        pool = _pool(pxp, wpool, pool_scale[l])
        x = _out_proj(na, yf, yb, px, pool, x, g1, ret_gn_w[l], avg, wo)
        x = _peer_residual(x, norm2_w[l], sh2, sc2, g2, peer_wq[l], peer_keys[l], peer_u[l], peer_v[l],
                           final_norm_w, last)

        if not last:
            na_c = _ctx_attention(pc)
            pool_c = _pool(pcp, wpool, pool_scale[l])
            ctx = _out_proj(na_c, ycf, ycb, pc, pool_c, ctx, cg1, ret_gn_w[l], avg, wo)
            ctx = _peer_residual(ctx, norm2_w[l], csh2, csc2, cg2, peer_wq[l], peer_keys[l], peer_u[l],
                                 peer_v[l], final_norm_w, False)
    return x
```

```python
import functools

import jax
import jax.numpy as jnp
from jax import lax
from jax.experimental import pallas as pl
from jax.experimental.pallas import tpu as pltpu

D_MODEL = 1024
GRID_W = 64
HEAD_DIM = 64
NA_HEADS = 6
NA_WIN_R = 8
NA_WIN_C = 16
RET_HEADS = 6
RET_CHUNK = 128
POOL_WINDOWS = (2, 4, 8, 16)
POOL_GROUP = 64
NA_WIDTH = NA_HEADS * HEAD_DIM
RET_WIDTH = RET_HEADS * HEAD_DIM
POOL_WIDTH = POOL_GROUP * len(POOL_WINDOWS)
O_RET_G = 3 * NA_WIDTH + 3 * RET_WIDTH
O_POOL = O_RET_G + RET_WIDTH
D_PROJ = O_POOL + POOL_WIDTH
ROPE_BASE = 10000.0
ROPE_PAIRS = HEAD_DIM // 4
PEER_HEADS = 8
PEER_NKEYS = 128
PEER_KEY_DIM = 128
PEER_TOPK = 16
N_SEL = PEER_HEADS * PEER_TOPK
NORM_EPS = 1e-6
NEG_INF = -1e30

LANES = 128
VMEM_LIMIT_BYTES = 56 * 1024 * 1024

F32 = jnp.float32
BF16 = jnp.bfloat16
HIGHEST = lax.Precision.HIGHEST
HEAD_PAIRS = NA_HEADS // 2
TOK_BLOCK = 128
CHUNKS = D_MODEL // LANES
TABLE_ROWS_PER_EXPERT = CHUNKS // 2


def _cparams(*sem):
    return pltpu.CompilerParams(dimension_semantics=sem, vmem_limit_bytes=VMEM_LIMIT_BYTES)


def _dot_nt(a, b, precision=None):
    return lax.dot_general(a, b, (((1,), (1,)), ((), ())), precision=precision,
                           preferred_element_type=F32)


def _rms_mod(x, nw, shift, scale):
    y = x * lax.rsqrt(jnp.mean(x * x, axis=-1, keepdims=True) + NORM_EPS)
    return (y * nw) * (1.0 + scale) + shift


def _mod_kernel(c_ref, w_ref, b_ref, o_ref):
    c = c_ref[...]
    a = c * jax.nn.sigmoid(c)
    o_ref[0] = jnp.dot(a, w_ref[0], precision=HIGHEST, preferred_element_type=F32) + b_ref[0]


def _modulation(cvec, w_ada, b_ada):
    depth, d, n = w_ada.shape
    tn = 1536
    return pl.pallas_call(
        _mod_kernel,
        grid=(depth, n // tn),
        in_specs=[pl.BlockSpec((8, d), lambda l, j: (0, 0)),
                  pl.BlockSpec((1, d, tn), lambda l, j: (l, 0, j)),
                  pl.BlockSpec((1, 1, tn), lambda l, j: (l, 0, j))],
        out_specs=pl.BlockSpec((1, 8, tn), lambda l, j: (l, 0, j)),
        out_shape=jax.ShapeDtypeStruct((depth, 8, n), F32),
        compiler_params=_cparams("parallel", "parallel"),
        name="adaln_mod",
    )(cvec, w_ada, b_ada.reshape(depth, 1, n))


def _inproj_kernel(x_ref, nw_ref, sh_ref, sc_ref, w_ref, o_ref, p_ref):
    h = _rms_mod(x_ref[0], nw_ref[...], sh_ref[0], sc_ref[0])
    r = jnp.dot(h.astype(BF16), w_ref[...], preferred_element_type=F32)
    o_ref[0] = r[:, :O_POOL]
    p_ref[0] = r[:, O_POOL:]


def _in_proj(x, nw, shift, scale, w_bf16):
    b, t, d = x.shape
    tm = min(512, t)
    return pl.pallas_call(
        _inproj_kernel,
        grid=(b, t // tm),
        in_specs=[pl.BlockSpec((1, tm, d), lambda i, j: (i, j, 0)),
                  pl.BlockSpec((1, d), lambda i, j: (0, 0)),
                  pl.BlockSpec((1, 1, d), lambda i, j: (i, 0, 0)),
                  pl.BlockSpec((1, 1, d), lambda i, j: (i, 0, 0)),
                  pl.BlockSpec((d, D_PROJ), lambda i, j: (0, 0))],
        out_specs=[pl.BlockSpec((1, tm, O_POOL), lambda i, j: (i, j, 0)),
                   pl.BlockSpec((1, tm, POOL_WIDTH), lambda i, j: (i, j, 0))],
        out_shape=[jax.ShapeDtypeStruct((b, t, O_POOL), F32),
                   jax.ShapeDtypeStruct((b, t, POOL_WIDTH), F32)],
        compiler_params=_cparams("parallel", "parallel"),
        name="in_proj",
    )(x, nw.reshape(1, d), shift, scale, w_bf16)


def _softmax_pv(s_list, v_list):
    m = s_list[0].max(axis=-1, keepdims=True)
    for s in s_list[1:]:
        m = jnp.maximum(m, s.max(axis=-1, keepdims=True))
    num = None
    den = None
    for s, v in zip(s_list, v_list):
        p = jnp.exp(s - m)
        pv = jnp.dot(p, v, preferred_element_type=F32)
        ps = p.sum(axis=-1, keepdims=True)
        num = pv if num is None else num + pv
        den = ps if den is None else den + ps
    return num / den


def _na_kernel(q_ref, k_ref, v_ref, kc_ref, vc_ref, bias_ref, o_ref, *, rows, rb):
    i = pl.program_id(2)
    lane = lax.broadcasted_iota(jnp.int32, (1, LANES), 1)
    first = lane < HEAD_DIM
    kc = kc_ref[0]
    vc = vc_ref[0]
    scale = HEAD_DIM ** -0.5
    nk = NA_WIN_R * GRID_W

    def body(rr, carry):
        r = i * rb + rr
        rs = jnp.clip(r - NA_WIN_R // 2, 0, rows - NA_WIN_R)
        delta = rs - r + (NA_WIN_R - 1)
        q = q_ref[0, pl.ds(pl.multiple_of(rr * GRID_W, GRID_W), GRID_W), :]
        k = k_ref[0, pl.ds(pl.multiple_of(rs * GRID_W, GRID_W), nk), :]
        v = v_ref[0, pl.ds(pl.multiple_of(rs * GRID_W, GRID_W), nk), :]
        outs = []
        for h in range(2):
            qm = jnp.where(first if h == 0 else jnp.logical_not(first), q, 0.0)
            s = _dot_nt(qm, k) * scale + bias_ref[0, h, delta]
            sc = _dot_nt(qm, kc) * scale
            outs.append(_softmax_pv([s, sc], [v, vc]))
        o_ref[0, pl.ds(pl.multiple_of(rr * GRID_W, GRID_W), GRID_W), :] = jnp.where(first, outs[0], outs[1])
        return carry

    lax.fori_loop(0, rb, body, 0)


def _na_attention(px, pc, bias8):
    b, s, _ = px.shape
    c = pc.shape[1]
    rows = s // GRID_W
    assert rows >= NA_WIN_R and s % GRID_W == 0
    rb = min(8, rows)
    nq, nkb, nvb = 0, HEAD_PAIRS, 2 * HEAD_PAIRS
    return pl.pallas_call(
        functools.partial(_na_kernel, rows=rows, rb=rb),
        grid=(b, HEAD_PAIRS, rows // rb),
        in_specs=[pl.BlockSpec((1, rb * GRID_W, LANES), lambda bi, hp, i: (bi, i, nq + hp)),
                  pl.BlockSpec((1, s, LANES), lambda bi, hp, i: (bi, 0, nkb + hp)),
                  pl.BlockSpec((1, s, LANES), lambda bi, hp, i: (bi, 0, nvb + hp)),
                  pl.BlockSpec((1, c, LANES), lambda bi, hp, i: (bi, 0, nkb + hp)),
                  pl.BlockSpec((1, c, LANES), lambda bi, hp, i: (bi, 0, nvb + hp)),
                  pl.BlockSpec((1, 2, NA_WIN_R, GRID_W, NA_WIN_R * GRID_W),
                               lambda bi, hp, i: (hp, 0, 0, 0, 0))],
        out_specs=pl.BlockSpec((1, rb * GRID_W, LANES), lambda bi, hp, i: (bi, i, hp)),
        out_shape=jax.ShapeDtypeStruct((b, s, NA_WIDTH), F32),
        compiler_params=_cparams("parallel", "parallel", "arbitrary"),
        name="na_attention",
    )(px, px, px, pc, pc, bias8)


def _ctx_attn_kernel(q_ref, k_ref, v_ref, o_ref):
    lane = lax.broadcasted_iota(jnp.int32, (1, LANES), 1)
    first = lane < HEAD_DIM
    q = q_ref[0]
    k = k_ref[0]
    v = v_ref[0]
    outs = []
    for h in range(2):
        qm = jnp.where(first if h == 0 else jnp.logical_not(first), q, 0.0)
        s = _dot_nt(qm, k) * HEAD_DIM ** -0.5
        outs.append(_softmax_pv([s], [v]))
    o_ref[0] = jnp.where(first, outs[0], outs[1])


def _ctx_attention(pc):
    b, c, _ = pc.shape
    return pl.pallas_call(
        _ctx_attn_kernel,
        grid=(b, HEAD_PAIRS),
        in_specs=[pl.BlockSpec((1, c, LANES), lambda bi, hp: (bi, 0, hp)),
                  pl.BlockSpec((1, c, LANES), lambda bi, hp: (bi, 0, HEAD_PAIRS + hp)),
                  pl.BlockSpec((1, c, LANES), lambda bi, hp: (bi, 0, 2 * HEAD_PAIRS + hp))],
        out_specs=pl.BlockSpec((1, c, LANES), lambda bi, hp: (bi, 0, hp)),
        out_shape=jax.ShapeDtypeStruct((b, c, NA_WIDTH), F32),
        compiler_params=_cparams("parallel", "parallel"),
        name="ctx_attention",
    )(pc, pc, pc)


def _ret_kernel(lg_ref, qf_ref, kf_ref, vf_ref, qb_ref, kb_ref, vb_ref,
                cf_ref, sf_ref, cb_ref, sb_ref, rf0_ref, rb0_ref,
                yf_ref, yb_ref, rfo_ref, rbo_ref, rf_scr, rb_scr):
    hp = pl.program_id(1)
    c = pl.program_id(2)
    nc = pl.num_programs(2)
    cs = RET_CHUNK

    @pl.when(c == 0)
    def _():
        rf_scr[...] = rf0_ref[0, 0]
        rb_scr[...] = rb0_ref[0, 0]

    lane = lax.broadcasted_iota(jnp.int32, (1, LANES), 1)
    first = lane < HEAD_DIM
    low = (lane % (2 * ROPE_PAIRS)) < ROPE_PAIRS
    pos = lax.broadcasted_iota(jnp.int32, (cs, 1), 0).astype(F32)
    ii = lax.broadcasted_iota(jnp.int32, (cs, cs), 0)
    jj = lax.broadcasted_iota(jnp.int32, (cs, cs), 1)
    diff = (ii - jj).astype(F32)
    same_head = (ii < HEAD_DIM) == (jj < HEAD_DIM)
    scale = HEAD_DIM ** -0.5

    def rope(x, cos, sin):
        swapped = jnp.where(low, pltpu.roll(x, LANES - ROPE_PAIRS, 1), pltpu.roll(x, ROPE_PAIRS, 1))
        return x * cos + swapped * sin

    def direction(d, q_ref, k_ref, v_ref, cos_ref, sin_ref, r_scr, y_ref):
        lg0 = lg_ref[d, 2 * hp]
        lg1 = lg_ref[d, 2 * hp + 1]
        lgv = jnp.where(first, lg0, lg1)
        cos = cos_ref[...]
        sin = sin_ref[...]
        q = rope(q_ref[0], cos, sin)
        k = rope(k_ref[0], cos, sin) * scale
        v = v_ref[0]
        outs = []
        for h, lg in ((0, lg0), (1, lg1)):
            if d == 0:
                dec = jnp.where(diff >= 0, jnp.exp(jnp.maximum(diff, 0.0) * lg), 0.0)
            else:
                dec = jnp.where(diff <= 0, jnp.exp(jnp.maximum(-diff, 0.0) * lg), 0.0)
            qm = jnp.where(first if h == 0 else jnp.logical_not(first), q, 0.0)
            outs.append(jnp.dot(_dot_nt(qm, k) * dec, v, preferred_element_type=F32))
        inner = jnp.where(first, outs[0], outs[1])
        if d == 0:
            xi = jnp.exp((pos + 1.0) * lgv)
            zeta = jnp.exp((cs - 1.0 - pos) * lgv)
        else:
            xi = jnp.exp((cs - pos) * lgv)
            zeta = jnp.exp(pos * lgv)
        r = r_scr[...]
        y_ref[0] = inner + jnp.dot(q * xi, r, preferred_element_type=F32)
        kv = lax.dot_general(k * zeta, v, (((0,), (0,)), ((), ())), preferred_element_type=F32)
        r_scr[...] = jnp.exp(cs * lgv) * r + jnp.where(same_head, kv, 0.0)

    direction(0, qf_ref, kf_ref, vf_ref, cf_ref, sf_ref, rf_scr, yf_ref)
    direction(1, qb_ref, kb_ref, vb_ref, cb_ref, sb_ref, rb_scr, yb_ref)

    @pl.when(c == nc - 1)
    def _():
        rfo_ref[0, 0] = rf_scr[...]
        rbo_ref[0, 0] = rb_scr[...]


def _retention(p, lg, cos_t, sin_t, rf0, rb0):
    b, t, _ = p.shape
    nc = t // RET_CHUNK
    qo, ko, vo = 3 * HEAD_PAIRS, 4 * HEAD_PAIRS, 5 * HEAD_PAIRS
    blk = (1, RET_CHUNK, LANES)
    fwd = lambda o: pl.BlockSpec(blk, lambda bi, hp, c: (bi, c, o + hp))
    bwd = lambda o: pl.BlockSpec(blk, lambda bi, hp, c: (bi, nc - 1 - c, o + hp))
    tab_f = pl.BlockSpec((RET_CHUNK, LANES), lambda bi, hp, c: (c, 0))
    tab_b = pl.BlockSpec((RET_CHUNK, LANES), lambda bi, hp, c: (nc - 1 - c, 0))
    st = pl.BlockSpec((1, 1, LANES, LANES), lambda bi, hp, c: (bi, hp, 0, 0))
    return pl.pallas_call(
        _ret_kernel,
        grid=(b, HEAD_PAIRS, nc),
        in_specs=[pl.BlockSpec(memory_space=pltpu.SMEM),
                  fwd(qo), fwd(ko), fwd(vo), bwd(qo), bwd(ko), bwd(vo),
                  tab_f, tab_f, tab_b, tab_b, st, st],
        out_specs=[pl.BlockSpec(blk, lambda bi, hp, c: (bi, c, hp)),
                   pl.BlockSpec(blk, lambda bi, hp, c: (bi, nc - 1 - c, hp)),
                   st, st],
        out_shape=[jax.ShapeDtypeStruct((b, t, RET_WIDTH), F32),
                   jax.ShapeDtypeStruct((b, t, RET_WIDTH), F32),
                   jax.ShapeDtypeStruct((b, HEAD_PAIRS, LANES, LANES), F32),
                   jax.ShapeDtypeStruct((b, HEAD_PAIRS, LANES, LANES), F32)],
        scratch_shapes=[pltpu.VMEM((LANES, LANES), F32), pltpu.VMEM((LANES, LANES), F32)],
        compiler_params=_cparams("parallel", "parallel", "arbitrary"),
        name="retention",
    )(lg, p, p, p, p, p, p, cos_t, sin_t, cos_t, sin_t, rf0, rb0)


def _pool_kernel(prev_ref, cur_ref, next_ref, w_ref, s_ref, o_ref, scr, *, t_total, tp):
    i = pl.program_id(1)
    n = pl.num_programs(1)
    halo = POOL_WINDOWS[-1] // 2
    x = cur_ref[0]
    scr[0:halo, :] = jnp.where(i > 0, prev_ref[0], 0.0)
    scr[halo:halo + tp, :] = x
    scr[halo + tp:2 * halo + tp, :] = jnp.where(i < n - 1, next_ref[0], 0.0)
    t = i * tp + lax.broadcasted_iota(jnp.int32, (tp, 1), 0)
    lane = lax.broadcasted_iota(jnp.int32, (1, POOL_WIDTH), 1)

    def shifted(s):
        return scr[halo + s:halo + s + tp, :]

    acc = None
    mean = None
    done = 0
    for g, w in enumerate(POOL_WINDOWS):
        half = w // 2
        for s in list(range(-half, -done)) + list(range(done, half)):
            sh = x if s == 0 else shifted(s)
            acc = sh if acc is None else acc + sh
        done = half
        cnt = (jnp.minimum(t + half, t_total) - jnp.maximum(t - half, 0)).astype(F32)
        mg = acc / cnt
        mean = mg if mean is None else jnp.where(lane >= g * POOL_GROUP, mg, mean)
    dlt = (mean - x).astype(BF16)
    o_ref[0] = jnp.dot(dlt, w_ref[...], preferred_element_type=F32) * s_ref[...]


def _pool(pin, w_bd_bf16, scale):
    b, t, _ = pin.shape
    tp = min(1024, t)
    halo = POOL_WINDOWS[-1] // 2
    nh = tp // halo
    last = t // halo - 1
    return pl.pallas_call(
        functools.partial(_pool_kernel, t_total=t, tp=tp),
        grid=(b, t // tp),
        in_specs=[pl.BlockSpec((1, halo, POOL_WIDTH), lambda bi, i: (bi, jnp.maximum(i * nh - 1, 0), 0)),
                  pl.BlockSpec((1, tp, POOL_WIDTH), lambda bi, i: (bi, i, 0)),
                  pl.BlockSpec((1, halo, POOL_WIDTH), lambda bi, i: (bi, jnp.minimum((i + 1) * nh, last), 0)),
                  pl.BlockSpec((POOL_WIDTH, POOL_WIDTH), lambda bi, i: (0, 0)),
                  pl.BlockSpec((1, POOL_WIDTH), lambda bi, i: (0, 0))],
        out_specs=pl.BlockSpec((1, tp, POOL_WIDTH), lambda bi, i: (bi, i, 0)),
        out_shape=jax.ShapeDtypeStruct((b, t, POOL_WIDTH), F32),
        scratch_shapes=[pltpu.VMEM((tp + 2 * halo, POOL_WIDTH), F32)],
        compiler_params=_cparams("parallel", "parallel"),
        name="multiscale_pool",
    )(pin, pin, pin, w_bd_bf16, scale.reshape(1, POOL_WIDTH))


def _out_kernel(na_ref, yf_ref, yb_ref, g_ref, pool_ref, x_ref, g1_ref, gnw_ref, avg_ref, wo_ref, o_ref):
    y = yf_ref[0] + yb_ref[0]
    avg = avg_ref[...]
    mu = jnp.dot(y, avg, precision=HIGHEST, preferred_element_type=F32)
    d = y - mu
    var = jnp.dot(d * d, avg, precision=HIGHEST, preferred_element_type=F32)
    yn = d * lax.rsqrt(var + NORM_EPS) * gnw_ref[...]
    g = g_ref[0]
    ret = yn * (g * jax.nn.sigmoid(g))
    mix = jnp.dot(na_ref[0].astype(BF16), wo_ref[0:NA_WIDTH, :], preferred_element_type=F32)
    mix += jnp.dot(ret.astype(BF16), wo_ref[NA_WIDTH:NA_WIDTH + RET_WIDTH, :], preferred_element_type=F32)
    mix += jnp.dot(pool_ref[0].astype(BF16), wo_ref[NA_WIDTH + RET_WIDTH:, :], preferred_element_type=F32)
    o_ref[0] = x_ref[0] + g1_ref[0] * mix


def _out_proj(na, yf, yb, p, pool, x, g1, gn_w, avg, wo_bf16):
    b, t, d = x.shape
    tm = min(512, t)
    row = lambda w: pl.BlockSpec((1, tm, w), lambda i, j: (i, j, 0))
    return pl.pallas_call(
        _out_kernel,
        grid=(b, t // tm),
        in_specs=[row(NA_WIDTH), row(RET_WIDTH), row(RET_WIDTH),
                  pl.BlockSpec((1, tm, RET_WIDTH), lambda i, j: (i, j, O_RET_G // RET_WIDTH)),
                  row(POOL_WIDTH), row(d),
                  pl.BlockSpec((1, 1, d), lambda i, j: (i, 0, 0)),
                  pl.BlockSpec((1, RET_WIDTH), lambda i, j: (0, 0)),
                  pl.BlockSpec((RET_WIDTH, RET_WIDTH), lambda i, j: (0, 0)),
                  pl.BlockSpec((d, d), lambda i, j: (0, 0))],
        out_specs=row(d),
        out_shape=jax.ShapeDtypeStruct((b, t, d), F32),
        compiler_params=_cparams("parallel", "parallel"),
        name="out_proj",
    )(na, yf, yb, p, pool, x, g1, gn_w.reshape(1, RET_WIDTH), avg, wo_bf16)


def _topk_rows(s, order=None, payload=None):
    n, m = s.shape
    if order is None:
        order = lax.broadcasted_iota(jnp.int32, (n, m), 0)
    krow = lax.broadcasted_iota(jnp.int32, (PEER_TOPK, m), 0)
    vals = jnp.zeros((PEER_TOPK, m), F32)
    idxs = jnp.zeros((PEER_TOPK, m), jnp.int32)
    for k in range(PEER_TOPK):
        mx = jnp.max(s, axis=0, keepdims=True)
        am = jnp.min(jnp.where(s == mx, order, jnp.iinfo(jnp.int32).max), axis=0, keepdims=True)
        sel = order == am
        pick = am if payload is None else jnp.sum(jnp.where(sel, payload, 0), axis=0, keepdims=True)
        vals = jnp.where(krow == k, mx, vals)
        idxs = jnp.where(krow == k, pick, idxs)
        s = jnp.where(sel, -jnp.inf, s)
    return vals, idxs


def _product_candidates(va, ia, vb, ib):
    k = PEER_TOPK
    sub = 8
    m = va.shape[1]
    row = lax.broadcasted_iota(jnp.int32, (sub, m), 0)
    cand, flat, eid = [], [], []

    def add(v, f, e, nvalid):
        cand.append(v if nvalid >= sub else jnp.where(row < nvalid, v, -jnp.inf))
        flat.append(f)
        eid.append(e)

    for i in range(sub):
        add(va[i:i + 1, :] + vb[0:sub, :], i * k + row, ia[i:i + 1, :] * PEER_NKEYS + ib[0:sub, :], k // (i + 1))
    add(va[0:1, :] + vb[sub:k, :], sub + row, ia[0:1, :] * PEER_NKEYS + ib[sub:k, :], sub)
    add(va[sub:k, :] + vb[0:1, :], (sub + row) * k, ia[sub:k, :] * PEER_NKEYS + ib[0:1, :], sub)
    return jnp.concatenate(cand, axis=0), jnp.concatenate(flat, axis=0), jnp.concatenate(eid, axis=0)


def _route_kernel(x_ref, nw_ref, sh_ref, sc_ref, wq_ref, keys_ref, idx_ref, gate_ref, q_scr, g_scr, i_scr, *, tm):
    h = _rms_mod(x_ref[0], nw_ref[...], sh_ref[0], sc_ref[0])
    q = jnp.dot(h, wq_ref[...], precision=HIGHEST, preferred_element_type=F32)
    for j in range(2 * PEER_HEADS):
        q_scr[j] = q[:, j * PEER_KEY_DIM:(j + 1) * PEER_KEY_DIM]

    def head(hh, carry):
        sa = _dot_nt(keys_ref[0, hh], q_scr[2 * hh], precision=HIGHEST)
        sb = _dot_nt(keys_ref[1, hh], q_scr[2 * hh + 1], precision=HIGHEST)
        va, ia = _topk_rows(sa)
        vb, ib = _topk_rows(sb)
        cand, flat, eid = _product_candidates(va, ia, vb, ib)
        sc, ei = _topk_rows(cand, order=flat, payload=eid)
        e = jnp.exp(sc - sc[0:1, :])
        off = pl.multiple_of(hh * PEER_TOPK, PEER_TOPK)
        g_scr[pl.ds(off, PEER_TOPK), :] = e / jnp.sum(e, axis=0, keepdims=True)
        i_scr[pl.ds(off, PEER_TOPK), :] = ei
        return carry

    lax.fori_loop(0, PEER_HEADS, head, 0)
    gate_ref[0] = g_scr[...].T
    idx_ref[0] = i_scr[...].T * TABLE_ROWS_PER_EXPERT


def _route(x, nw, shift, scale, wq, keys):
    b, t, d = x.shape
    tm = min(256, t)
    nt = t // tm
    nq = wq.shape[1]
    return pl.pallas_call(
        functools.partial(_route_kernel, tm=tm),
        grid=(b, nt),
        in_specs=[pl.BlockSpec((1, tm, d), lambda i, j: (i, j, 0)),
                  pl.BlockSpec((1, d), lambda i, j: (0, 0)),
                  pl.BlockSpec((1, 1, d), lambda i, j: (i, 0, 0)),
                  pl.BlockSpec((1, 1, d), lambda i, j: (i, 0, 0)),
                  pl.BlockSpec((d, nq), lambda i, j: (0, 0)),
                  pl.BlockSpec((2, PEER_HEADS, PEER_NKEYS, PEER_KEY_DIM), lambda i, j: (0, 0, 0, 0))],
        out_specs=[pl.BlockSpec((1, tm, N_SEL), lambda i, j: (i, j, 0)),
                   pl.BlockSpec((1, tm, N_SEL), lambda i, j: (i, j, 0))],
        out_shape=[jax.ShapeDtypeStruct((b, t, N_SEL), jnp.int32),
                   jax.ShapeDtypeStruct((b, t, N_SEL), F32)],
        scratch_shapes=[pltpu.VMEM((2 * PEER_HEADS, tm, PEER_KEY_DIM), F32),
                        pltpu.VMEM((N_SEL, tm), F32),
                        pltpu.VMEM((N_SEL, tm), jnp.int32)],
        compiler_params=_cparams("parallel", "parallel"),
        name="peer_route",
    )(x, nw.reshape(1, d), shift, scale, wq, keys)


def _pack_table(tab):
    n, d = tab.shape
    bits = lax.bitcast_convert_type(tab.astype(BF16), jnp.uint16).astype(jnp.uint32)
    words = bits[:, :d // 2] | (bits[:, d // 2:] << 16)
    return words.reshape(n * TABLE_ROWS_PER_EXPERT, LANES)


def _token_loop(tb, token, per_trip):
    def trip(i, carry):
        for j in range(per_trip):
            token(i * per_trip + j)
        return carry

    lax.fori_loop(0, tb // per_trip, trip, 0)


def _unpack(slab):
    lo = lax.bitcast_convert_type(slab << 16, F32)
    hi = lax.bitcast_convert_type(slab & jnp.uint32(0xFFFF0000), F32)
    return lo, hi


def _peer_act_kernel(idx_ref, tab_ref, x_ref, nw_ref, sh_ref, sc_ref, gate_ref, w_ref, h_scr, act_scr, *, tb):
    h = _rms_mod(x_ref[...], nw_ref[...], sh_ref[0], sc_ref[0])
    half = CHUNKS // 2
    for c in range(CHUNKS):
        piece = h[:, c * LANES:(c + 1) * LANES]
        base = (c // half) * CHUNKS + c % half
        h_scr[pl.ds(base, tb, stride=2 * CHUNKS), :] = piece
        h_scr[pl.ds(base + half, tb, stride=2 * CHUNKS), :] = piece
    sub = lax.broadcasted_iota(jnp.int32, (CHUNKS, N_SEL), 0)
    lane = lax.broadcasted_iota(jnp.int32, (CHUNKS, N_SEL), 1)
    slot = lane - PAIR_STRIDE * (sub >= half).astype(jnp.int32)

    def token(t):
        off = pl.multiple_of(t * 2 * CHUNKS, 2 * CHUNKS)
        hlo = h_scr[pl.ds(off, CHUNKS), :]
        hhi = h_scr[pl.ds(off + CHUNKS, CHUNKS), :]
        acc = jnp.zeros((CHUNKS, N_SEL), F32)
        for a, slab in _slab_pairs(tab_ref, idx_ref, t):
            lo, hi = _unpack(slab)
            dot = jnp.sum(lo * hlo + hi * hhi, axis=-1, keepdims=True)
            acc = jnp.where(slot == a, dot, acc)
        act_scr[pl.ds(t, 1), :] = jnp.sum(acc, axis=0, keepdims=True)

    _token_loop(tb, token, 4)
    a = act_scr[...]
    w_ref[...] = gate_ref[...] * (0.5 * a * (1.0 + lax.erf(a * (2.0 ** -0.5))))


PAIR_STRIDE = N_SEL // 4


def _slab_pairs(tab_ref, idx_ref, t):
    views = [idx_ref.at[pl.ds(t * N_SEL + q * PAIR_STRIDE, PAIR_STRIDE)] for q in range(4)]
    for k in range(PAIR_STRIDE):
        rows = [pl.multiple_of(v[k], TABLE_ROWS_PER_EXPERT) for v in views]
        for q in (0, 2):
            yield q * PAIR_STRIDE + k, jnp.concatenate(
                [tab_ref[pl.ds(rows[q], TABLE_ROWS_PER_EXPERT), :],
                 tab_ref[pl.ds(rows[q + 1], TABLE_ROWS_PER_EXPERT), :]], axis=0)


def _peer_out_kernel(idx_ref, w_ref, tab_ref, x_ref, g2_ref, fw_ref, o_ref, p_scr, *, tb, final):
    half = CHUNKS // 2

    def token(t):
        accs = [jnp.zeros((half, LANES), F32) for _ in range(4)]
        for s in range(N_SEL):
            row = pl.multiple_of(idx_ref[t * N_SEL + s], TABLE_ROWS_PER_EXPERT)
            w = w_ref[t * N_SEL + s]
            lo, hi = _unpack(tab_ref[pl.ds(row, TABLE_ROWS_PER_EXPERT), :])
            j = 2 * (s % 2)
            accs[j] = accs[j] + w * lo
            accs[j + 1] = accs[j + 1] + w * hi
        off = pl.multiple_of(t * CHUNKS, CHUNKS)
        p_scr[pl.ds(off, half), :] = accs[0] + accs[2]
        p_scr[pl.ds(pl.multiple_of(off + half, half), half), :] = accs[1] + accs[3]

    _token_loop(tb, token, 1)
    peer = jnp.concatenate([p_scr[pl.ds(c, tb, stride=CHUNKS), :] for c in range(CHUNKS)], axis=1)
    y = x_ref[...] + g2_ref[0] * peer
    if final:
        y = y * lax.rsqrt(jnp.mean(y * y, axis=-1, keepdims=True) + NORM_EPS) * fw_ref[...]
    o_ref[...] = y


def _peer_residual(x, nw, shift, scale, g2, wq, keys, u, v, final_w, final):
    b, t, d = x.shape
    assert d == CHUNKS * LANES
    idx, gate = _route(x, nw, shift, scale, wq, keys)
    n = b * t
    tb = TOK_BLOCK
    per_batch = t // tb
    x2 = x.reshape(n, d)
    idx1 = idx.reshape(n * N_SEL)
    smem_blk = pl.BlockSpec((tb * N_SEL,), lambda i: (i,), memory_space=pltpu.SMEM)
    table = pl.BlockSpec((u.shape[0] * TABLE_ROWS_PER_EXPERT, LANES), lambda i: (0, 0),
                         pipeline_mode=pl.Buffered(1))
    rows = pl.BlockSpec((tb, d), lambda i: (i, 0))
    sel = pl.BlockSpec((tb, N_SEL), lambda i: (i, 0))
    vec = pl.BlockSpec((1, 1, d), lambda i: (i // per_batch, 0, 0))
    const = pl.BlockSpec((1, d), lambda i: (0, 0))
    w = pl.pallas_call(
        functools.partial(_peer_act_kernel, tb=tb),
        grid=(n // tb,),
        in_specs=[smem_blk, table, rows, const, vec, vec, sel],
        out_specs=sel,
        out_shape=jax.ShapeDtypeStruct((n, N_SEL), F32),
        scratch_shapes=[pltpu.VMEM((tb * 2 * CHUNKS, LANES), F32), pltpu.VMEM((tb, N_SEL), F32)],
        compiler_params=_cparams("arbitrary"),
        name="peer_act",
    )(idx1, _pack_table(u), x2, nw.reshape(1, d), shift, scale, gate.reshape(n, N_SEL))
    y = pl.pallas_call(
        functools.partial(_peer_out_kernel, tb=tb, final=final),
        grid=(n // tb,),
        in_specs=[smem_blk, smem_blk, table, rows, vec, const],
        out_specs=rows,
        out_shape=jax.ShapeDtypeStruct((n, d), F32),
        scratch_shapes=[pltpu.VMEM((tb * CHUNKS, LANES), F32)],
        compiler_params=_cparams("arbitrary"),
        name="peer_out",
    )(idx1, w.reshape(n * N_SEL), _pack_table(v), x2, g2, final_w.reshape(1, d))
    return y.reshape(b, t, d)


def _na_bias_table(rpb):
    cq = jnp.arange(GRID_W)
    coff = jnp.clip(cq[None, :] - cq[:, None] + (NA_WIN_C - 1), 0, 2 * NA_WIN_C - 2)
    c_start = jnp.clip(cq - NA_WIN_C // 2, 0, GRID_W - NA_WIN_C)
    ok = (cq[None, :] >= c_start[:, None]) & (cq[None, :] < c_start[:, None] + NA_WIN_C)
    roff = jnp.arange(NA_WIN_R)[:, None] + jnp.arange(NA_WIN_R)[None, :]
    bias = rpb.astype(F32)[:, roff][..., coff]
    bias = jnp.where(ok[None, None, None], bias, NEG_INF)
    bias = bias.transpose(0, 1, 3, 2, 4).reshape(NA_HEADS, NA_WIN_R, GRID_W, NA_WIN_R * GRID_W)
    return bias.reshape(HEAD_PAIRS, 2, NA_WIN_R, GRID_W, NA_WIN_R * GRID_W)


def _rope_tables(s):
    t = jnp.arange(s)
    row = (t // GRID_W).astype(F32)
    col = (t % GRID_W).astype(F32)
    inv = ROPE_BASE ** (-jnp.arange(ROPE_PAIRS, dtype=F32) / ROPE_PAIRS)
    cr, sr = jnp.cos(row[:, None] * inv), jnp.sin(row[:, None] * inv)
    cc, sc = jnp.cos(col[:, None] * inv), jnp.sin(col[:, None] * inv)
    cos = jnp.concatenate([cr, cr, cc, cc], axis=-1)
    sin = jnp.concatenate([-sr, sr, -sc, sc], axis=-1)
    return jnp.tile(cos, (1, 2)), jnp.tile(sin, (1, 2))


def _block_diag(blocks):
    n, a, bb = blocks.shape
    eye = jnp.eye(n, dtype=blocks.dtype)
    return (eye[:, None, :, None] * blocks[:, :, None, :]).reshape(n * a, n * bb)


def kernel(x, c, ctx, c_ctx, norm1_w, norm2_w, w_ada, b_ada, w_in, w_out, na_rpb, ret_decay_fwd, ret_decay_bwd, ret_gn_w, pool_w, pool_scale, peer_wq, peer_keys, peer_u, peer_v, final_norm_w):
    b, s, d = x.shape
    depth = w_in.shape[0]
    clen = ctx.shape[1]
    cvec = jnp.concatenate([c, c_ctx[None, :], jnp.zeros((8 - b - 1, d), F32)], axis=0)
    mod = _modulation(cvec, w_ada, b_ada)
    cos_x, sin_x = _rope_tables(s)
    cos_c, sin_c = jnp.ones((clen, LANES), F32), jnp.zeros((clen, LANES), F32)
    avg = _block_diag(jnp.full((RET_HEADS, HEAD_DIM, HEAD_DIM), 1.0 / HEAD_DIM, F32))
    zero_state = jnp.zeros((b, HEAD_PAIRS, LANES, LANES), F32)

    for l in range(depth):
        last = l == depth - 1
        mx = mod[l, :b].reshape(b, 1, 6, d)
        mc = jnp.broadcast_to(mod[l, b].reshape(1, 1, 6, d), (b, 1, 6, d))
        sh1, sc1, g1, sh2, sc2, g2 = [mx[:, :, i] for i in range(6)]
        csh1, csc1, cg1, csh2, csc2, cg2 = [mc[:, :, i] for i in range(6)]
        lg = jnp.stack([jax.nn.log_sigmoid(ret_decay_fwd[l].astype(F32)),
                        jax.nn.log_sigmoid(ret_decay_bwd[l].astype(F32))], axis=0)
        wi = w_in[l].astype(BF16)
        wo = w_out[l].astype(BF16)
        wpool = _block_diag(pool_w[l]).astype(BF16)
        bias8 = _na_bias_table(na_rpb[l])

        pc, pcp = _in_proj(ctx, norm1_w[l], csh1, csc1, wi)
        ycf, ycb, r_f, r_b = _retention(pc, lg, cos_c, sin_c, zero_state, zero_state)

        px, pxp = _in_proj(x, norm1_w[l], sh1, sc1, wi)
        na = _na_attention(px, pc, bias8)
        yf, yb, _, _ = _retention(px, lg, cos_x, sin_x, r_f, r_b)
        pool = _pool(pxp, wpool, pool_scale[l])
        x = _out_proj(na, yf, yb, px, pool, x, g1, ret_gn_w[l], avg, wo)
        x = _peer_residual(x, norm2_w[l], sh2, sc2, g2, peer_wq[l], peer_keys[l], peer_u[l], peer_v[l],
                           final_norm_w, last)

        if not last:
            na_c = _ctx_attention(pc)
            pool_c = _pool(pcp, wpool, pool_scale[l])
            ctx = _out_proj(na_c, ycf, ycb, pc, pool_c, ctx, cg1, ret_gn_w[l], avg, wo)
            ctx = _peer_residual(ctx, norm2_w[l], csh2, csc2, cg2, peer_wq[l], peer_keys[l], peer_u[l],
                                 peer_v[l], final_norm_w, False)
    return x
```

```python
import functools

import jax
import jax.numpy as jnp
from jax import lax
from jax.experimental import pallas as pl
from jax.experimental.pallas import tpu as pltpu

D_MODEL = 1024
GRID_W = 64
HEAD_DIM = 64
NA_HEADS = 6
NA_WIN_R = 8
NA_WIN_C = 16
RET_HEADS = 6
RET_CHUNK = 128
POOL_WINDOWS = (2, 4, 8, 16)
POOL_GROUP = 64
NA_WIDTH = NA_HEADS * HEAD_DIM
RET_WIDTH = RET_HEADS * HEAD_DIM
POOL_WIDTH = POOL_GROUP * len(POOL_WINDOWS)
O_RET_G = 3 * NA_WIDTH + 3 * RET_WIDTH
O_POOL = O_RET_G + RET_WIDTH
D_PROJ = O_POOL + POOL_WIDTH
ROPE_BASE = 10000.0
ROPE_PAIRS = HEAD_DIM // 4
PEER_HEADS = 8
PEER_NKEYS = 128
PEER_KEY_DIM = 128
PEER_TOPK = 16
N_SEL = PEER_HEADS * PEER_TOPK
NORM_EPS = 1e-6
NEG_INF = -1e30

LANES = 128
VMEM_LIMIT_BYTES = 56 * 1024 * 1024

F32 = jnp.float32
BF16 = jnp.bfloat16
HIGHEST = lax.Precision.HIGHEST
HEAD_PAIRS = NA_HEADS // 2
TOK_BLOCK = 128
CHUNKS = D_MODEL // LANES
TABLE_ROWS_PER_EXPERT = CHUNKS // 2


def _cparams(*sem):
    return pltpu.CompilerParams(dimension_semantics=sem, vmem_limit_bytes=VMEM_LIMIT_BYTES)


def _dot_nt(a, b, precision=None):
    return lax.dot_general(a, b, (((1,), (1,)), ((), ())), precision=precision,
                           preferred_element_type=F32)


def _split_bf16(a):
    hi = a.astype(BF16)
    return hi, (a - hi.astype(F32)).astype(BF16)


def _rms_mod(x, nw, shift, scale):
    y = x * lax.rsqrt(jnp.mean(x * x, axis=-1, keepdims=True) + NORM_EPS)
    return (y * nw) * (1.0 + scale) + shift


def _mod_kernel(c_ref, w_ref, b_ref, o_ref):
    c = c_ref[...]
    a = c * jax.nn.sigmoid(c)
    o_ref[0] = jnp.dot(a, w_ref[0], precision=HIGHEST, preferred_element_type=F32) + b_ref[0]


def _modulation(cvec, w_ada, b_ada):
    depth, d, n = w_ada.shape
    tn = 1536
    return pl.pallas_call(
        _mod_kernel,
        grid=(depth, n // tn),
        in_specs=[pl.BlockSpec((8, d), lambda l, j: (0, 0)),
                  pl.BlockSpec((1, d, tn), lambda l, j: (l, 0, j)),
                  pl.BlockSpec((1, 1, tn), lambda l, j: (l, 0, j))],
        out_specs=pl.BlockSpec((1, 8, tn), lambda l, j: (l, 0, j)),
        out_shape=jax.ShapeDtypeStruct((depth, 8, n), F32),
        compiler_params=_cparams("parallel", "parallel"),
        name="adaln_mod",
    )(cvec, w_ada, b_ada.reshape(depth, 1, n))


def _inproj_kernel(x_ref, nw_ref, sh_ref, sc_ref, w_ref, o_ref, p_ref):
    h = _rms_mod(x_ref[0], nw_ref[...], sh_ref[0], sc_ref[0])
    r = jnp.dot(h.astype(BF16), w_ref[...], preferred_element_type=F32)
    o_ref[0] = r[:, :O_POOL]
    p_ref[0] = r[:, O_POOL:]


def _in_proj(x, nw, shift, scale, w_bf16):
    b, t, d = x.shape
    tm = min(512, t)
    return pl.pallas_call(
        _inproj_kernel,
        grid=(b, t // tm),
        in_specs=[pl.BlockSpec((1, tm, d), lambda i, j: (i, j, 0)),
                  pl.BlockSpec((1, d), lambda i, j: (0, 0)),
                  pl.BlockSpec((1, 1, d), lambda i, j: (i, 0, 0)),
                  pl.BlockSpec((1, 1, d), lambda i, j: (i, 0, 0)),
                  pl.BlockSpec((d, D_PROJ), lambda i, j: (0, 0))],
        out_specs=[pl.BlockSpec((1, tm, O_POOL), lambda i, j: (i, j, 0)),
                   pl.BlockSpec((1, tm, POOL_WIDTH), lambda i, j: (i, j, 0))],
        out_shape=[jax.ShapeDtypeStruct((b, t, O_POOL), F32),
                   jax.ShapeDtypeStruct((b, t, POOL_WIDTH), F32)],
        compiler_params=_cparams("parallel", "parallel"),
        name="in_proj",
    )(x, nw.reshape(1, d), shift, scale, w_bf16)


def _softmax_pv(s_list, v_list):
    m = s_list[0].max(axis=-1, keepdims=True)
    for s in s_list[1:]:
        m = jnp.maximum(m, s.max(axis=-1, keepdims=True))
    num = None
    den = None
    for s, v in zip(s_list, v_list):
        p = jnp.exp(s - m)
        pv = jnp.dot(p.astype(BF16), v.astype(BF16), preferred_element_type=F32)
        ps = p.sum(axis=-1, keepdims=True)
        num = pv if num is None else num + pv
        den = ps if den is None else den + ps
    return num / den


NA_ROWS_PER_TRIP = 4


def _na_kernel(q_ref, k_ref, v_ref, kc_ref, vc_ref, bias_ref, o_ref, *, rows, rb):
    i = pl.program_id(2)
    lane = lax.broadcasted_iota(jnp.int32, (1, LANES), 1)
    first = lane < HEAD_DIM
    kc = kc_ref[0].astype(BF16)
    vc = vc_ref[0].astype(BF16)
    scale = HEAD_DIM ** -0.5
    nk = NA_WIN_R * GRID_W

    def body(it, carry):
        for u in range(NA_ROWS_PER_TRIP):
            one_row(it * NA_ROWS_PER_TRIP + u)
        return carry

    def one_row(rr):
        r = i * rb + rr
        rs = jnp.clip(r - NA_WIN_R // 2, 0, rows - NA_WIN_R)
        delta = rs - r + (NA_WIN_R - 1)
        q = q_ref[0, pl.ds(pl.multiple_of(rr * GRID_W, GRID_W), GRID_W), :]
        k = k_ref[0, pl.ds(pl.multiple_of(rs * GRID_W, GRID_W), nk), :].astype(BF16)
        v = v_ref[0, pl.ds(pl.multiple_of(rs * GRID_W, GRID_W), nk), :].astype(BF16)
        q2 = jnp.concatenate([jnp.where(first, q, 0.0), jnp.where(first, 0.0, q)], axis=0).astype(BF16)
        s = _dot_nt(q2, k) * scale + bias_ref[0, delta]
        sc = _dot_nt(q2, kc) * scale
        o2 = _softmax_pv([s, sc], [v, vc])
        o_ref[0, pl.ds(pl.multiple_of(rr * GRID_W, GRID_W), GRID_W), :] = jnp.where(
            first, o2[0:GRID_W], o2[GRID_W:2 * GRID_W])

    lax.fori_loop(0, rb // NA_ROWS_PER_TRIP, body, 0)


def _na_attention(px, pc, bias8):
    b, s, _ = px.shape
    c = pc.shape[1]
    rows = s // GRID_W
    assert rows >= NA_WIN_R and s % GRID_W == 0
    rb = min(8, rows)
    nq, nkb, nvb = 0, HEAD_PAIRS, 2 * HEAD_PAIRS
    return pl.pallas_call(
        functools.partial(_na_kernel, rows=rows, rb=rb),
        grid=(b, HEAD_PAIRS, rows // rb),
        in_specs=[pl.BlockSpec((1, rb * GRID_W, LANES), lambda bi, hp, i: (bi, i, nq + hp)),
                  pl.BlockSpec((1, s, LANES), lambda bi, hp, i: (bi, 0, nkb + hp)),
                  pl.BlockSpec((1, s, LANES), lambda bi, hp, i: (bi, 0, nvb + hp)),
                  pl.BlockSpec((1, c, LANES), lambda bi, hp, i: (bi, 0, nkb + hp)),
                  pl.BlockSpec((1, c, LANES), lambda bi, hp, i: (bi, 0, nvb + hp)),
                  pl.BlockSpec((1, NA_WIN_R, 2 * GRID_W, NA_WIN_R * GRID_W),
                               lambda bi, hp, i: (hp, 0, 0, 0))],
        out_specs=pl.BlockSpec((1, rb * GRID_W, LANES), lambda bi, hp, i: (bi, i, hp)),
        out_shape=jax.ShapeDtypeStruct((b, s, NA_WIDTH), F32),
        compiler_params=_cparams("parallel", "parallel", "arbitrary"),
        name="na_attention",
    )(px, px, px, pc, pc, bias8)


def _ctx_attn_kernel(q_ref, k_ref, v_ref, o_ref):
    lane = lax.broadcasted_iota(jnp.int32, (1, LANES), 1)
    first = lane < HEAD_DIM
    q = q_ref[0]
    k = k_ref[0]
    v = v_ref[0]
    outs = []
    for h in range(2):
        qm = jnp.where(first if h == 0 else jnp.logical_not(first), q, 0.0)
        s = _dot_nt(qm, k) * HEAD_DIM ** -0.5
        outs.append(_softmax_pv([s], [v]))
    o_ref[0] = jnp.where(first, outs[0], outs[1])


def _ctx_attention(pc):
    b, c, _ = pc.shape
    return pl.pallas_call(
        _ctx_attn_kernel,
        grid=(b, HEAD_PAIRS),
        in_specs=[pl.BlockSpec((1, c, LANES), lambda bi, hp: (bi, 0, hp)),
                  pl.BlockSpec((1, c, LANES), lambda bi, hp: (bi, 0, HEAD_PAIRS + hp)),
                  pl.BlockSpec((1, c, LANES), lambda bi, hp: (bi, 0, 2 * HEAD_PAIRS + hp))],
        out_specs=pl.BlockSpec((1, c, LANES), lambda bi, hp: (bi, 0, hp)),
        out_shape=jax.ShapeDtypeStruct((b, c, NA_WIDTH), F32),
        compiler_params=_cparams("parallel", "parallel"),
        name="ctx_attention",
    )(pc, pc, pc)


RET_CHUNKS_PER_STEP = 4


def _ret_kernel(lg_ref, qf_ref, kf_ref, vf_ref, qb_ref, kb_ref, vb_ref,
                cf_ref, sf_ref, cb_ref, sb_ref, rf0_ref, rb0_ref,
                yf_ref, yb_ref, rfo_ref, rbo_ref, rf_scr, rb_scr, *, cpb):
    hp = pl.program_id(1)
    c = pl.program_id(2)
    nc = pl.num_programs(2)
    cs = RET_CHUNK

    @pl.when(c == 0)
    def _():
        rf_scr[...] = rf0_ref[0, 0]
        rb_scr[...] = rb0_ref[0, 0]

    lane = lax.broadcasted_iota(jnp.int32, (1, LANES), 1)
    first = lane < HEAD_DIM
    low = (lane % (2 * ROPE_PAIRS)) < ROPE_PAIRS
    pos = lax.broadcasted_iota(jnp.int32, (cs, 1), 0).astype(F32)
    ii = lax.broadcasted_iota(jnp.int32, (cs, cs), 0)
    jj = lax.broadcasted_iota(jnp.int32, (cs, cs), 1)
    diff = (ii - jj).astype(F32)
    same_head = (ii < HEAD_DIM) == (jj < HEAD_DIM)
    scale = HEAD_DIM ** -0.5

    def rope(x, cos, sin):
        swapped = jnp.where(low, pltpu.roll(x, LANES - ROPE_PAIRS, 1), pltpu.roll(x, ROPE_PAIRS, 1))
        return x * cos + swapped * sin

    def direction(d, q_ref, k_ref, v_ref, cos_ref, sin_ref, r_scr, y_ref):
        lg0 = lg_ref[d, 2 * hp]
        lg1 = lg_ref[d, 2 * hp + 1]
        lgv = jnp.where(first, lg0, lg1)
        decs = []
        for lg in (lg0, lg1):
            if d == 0:
                decs.append(jnp.where(diff >= 0, jnp.exp(jnp.maximum(diff, 0.0) * lg), 0.0))
            else:
                decs.append(jnp.where(diff <= 0, jnp.exp(jnp.maximum(-diff, 0.0) * lg), 0.0))
        dec2 = jnp.concatenate(decs, axis=0)
        if d == 0:
            xi = jnp.exp((pos + 1.0) * lgv)
            zeta = jnp.exp((cs - 1.0 - pos) * lgv)
        else:
            xi = jnp.exp((cs - pos) * lgv)
            zeta = jnp.exp(pos * lgv)
        chunk_decay = jnp.exp(cs * lgv)
        r = r_scr[...]
        for j in (range(cpb) if d == 0 else reversed(range(cpb))):
            rows = pl.ds(j * cs, cs)
            cos = cos_ref[rows, :]
            sin = sin_ref[rows, :]
            q = rope(q_ref[0, rows, :], cos, sin)
            k = rope(k_ref[0, rows, :], cos, sin) * scale
            v = v_ref[0, rows, :].astype(BF16)
            q2 = jnp.concatenate([jnp.where(first, q, 0.0), jnp.where(first, 0.0, q)], axis=0).astype(BF16)
            s2 = _dot_nt(q2, k.astype(BF16)) * dec2
            o2 = jnp.dot(s2.astype(BF16), v, preferred_element_type=F32)
            inner = jnp.where(first, o2[0:cs], o2[cs:2 * cs])
            y_ref[0, rows, :] = inner + jnp.dot((q * xi).astype(BF16), r.astype(BF16),
                                                preferred_element_type=F32)
            kv = lax.dot_general((k * zeta).astype(BF16), v, (((0,), (0,)), ((), ())),
                                 preferred_element_type=F32)
            r = chunk_decay * r + jnp.where(same_head, kv, 0.0)
        r_scr[...] = r

    direction(0, qf_ref, kf_ref, vf_ref, cf_ref, sf_ref, rf_scr, yf_ref)
    direction(1, qb_ref, kb_ref, vb_ref, cb_ref, sb_ref, rb_scr, yb_ref)

    @pl.when(c == nc - 1)
    def _():
        rfo_ref[0, 0] = rf_scr[...]
        rbo_ref[0, 0] = rb_scr[...]


def _retention(p, lg, cos_t, sin_t, rf0, rb0):
    b, t, _ = p.shape
    cpb = min(RET_CHUNKS_PER_STEP, t // RET_CHUNK)
    nc = t // (RET_CHUNK * cpb)
    qo, ko, vo = 3 * HEAD_PAIRS, 4 * HEAD_PAIRS, 5 * HEAD_PAIRS
    blk = (1, RET_CHUNK * cpb, LANES)
    fwd = lambda o: pl.BlockSpec(blk, lambda bi, hp, c: (bi, c, o + hp))
    bwd = lambda o: pl.BlockSpec(blk, lambda bi, hp, c: (bi, nc - 1 - c, o + hp))
    tab_f = pl.BlockSpec((RET_CHUNK * cpb, LANES), lambda bi, hp, c: (c, 0))
    tab_b = pl.BlockSpec((RET_CHUNK * cpb, LANES), lambda bi, hp, c: (nc - 1 - c, 0))
    st = pl.BlockSpec((1, 1, LANES, LANES), lambda bi, hp, c: (bi, hp, 0, 0))
    return pl.pallas_call(
        functools.partial(_ret_kernel, cpb=cpb),
        grid=(b, HEAD_PAIRS, nc),
        in_specs=[pl.BlockSpec(memory_space=pltpu.SMEM),
                  fwd(qo), fwd(ko), fwd(vo), bwd(qo), bwd(ko), bwd(vo),
                  tab_f, tab_f, tab_b, tab_b, st, st],
        out_specs=[pl.BlockSpec(blk, lambda bi, hp, c: (bi, c, hp)),
                   pl.BlockSpec(blk, lambda bi, hp, c: (bi, nc - 1 - c, hp)),
                   st, st],
        out_shape=[jax.ShapeDtypeStruct((b, t, RET_WIDTH), F32),
                   jax.ShapeDtypeStruct((b, t, RET_WIDTH), F32),
                   jax.ShapeDtypeStruct((b, HEAD_PAIRS, LANES, LANES), F32),
                   jax.ShapeDtypeStruct((b, HEAD_PAIRS, LANES, LANES), F32)],
        scratch_shapes=[pltpu.VMEM((LANES, LANES), F32), pltpu.VMEM((LANES, LANES), F32)],
        compiler_params=_cparams("parallel", "parallel", "arbitrary"),
        name="retention",
    )(lg, p, p, p, p, p, p, cos_t, sin_t, cos_t, sin_t, rf0, rb0)


def _pool_kernel(prev_ref, cur_ref, next_ref, w_ref, s_ref, o_ref, scr, *, t_total, tp):
    i = pl.program_id(1)
    n = pl.num_programs(1)
    halo = POOL_WINDOWS[-1] // 2
    x = cur_ref[0]
    scr[0:halo, :] = jnp.where(i > 0, prev_ref[0], 0.0)
    scr[halo:halo + tp, :] = x
    scr[halo + tp:2 * halo + tp, :] = jnp.where(i < n - 1, next_ref[0], 0.0)
    t = i * tp + lax.broadcasted_iota(jnp.int32, (tp, 1), 0)
    lane = lax.broadcasted_iota(jnp.int32, (1, POOL_WIDTH), 1)

    def shifted(s):
        return scr[halo + s:halo + s + tp, :]

    acc = None
    mean = None
    done = 0
    for g, w in enumerate(POOL_WINDOWS):
        half = w // 2
        for s in list(range(-half, -done)) + list(range(done, half)):
            sh = x if s == 0 else shifted(s)
            acc = sh if acc is None else acc + sh
        done = half
        cnt = (jnp.minimum(t + half, t_total) - jnp.maximum(t - half, 0)).astype(F32)
        mg = acc / cnt
        mean = mg if mean is None else jnp.where(lane >= g * POOL_GROUP, mg, mean)
    dlt = (mean - x).astype(BF16)
    o_ref[0] = jnp.dot(dlt, w_ref[...], preferred_element_type=F32) * s_ref[...]


def _pool(pin, w_bd_bf16, scale):
    b, t, _ = pin.shape
    tp = min(1024, t)
    halo = POOL_WINDOWS[-1] // 2
    nh = tp // halo
    last = t // halo - 1
    return pl.pallas_call(
        functools.partial(_pool_kernel, t_total=t, tp=tp),
        grid=(b, t // tp),
        in_specs=[pl.BlockSpec((1, halo, POOL_WIDTH), lambda bi, i: (bi, jnp.maximum(i * nh - 1, 0), 0)),
                  pl.BlockSpec((1, tp, POOL_WIDTH), lambda bi, i: (bi, i, 0)),
                  pl.BlockSpec((1, halo, POOL_WIDTH), lambda bi, i: (bi, jnp.minimum((i + 1) * nh, last), 0)),
                  pl.BlockSpec((POOL_WIDTH, POOL_WIDTH), lambda bi, i: (0, 0)),
                  pl.BlockSpec((1, POOL_WIDTH), lambda bi, i: (0, 0))],
        out_specs=pl.BlockSpec((1, tp, POOL_WIDTH), lambda bi, i: (bi, i, 0)),
        out_shape=jax.ShapeDtypeStruct((b, t, POOL_WIDTH), F32),
        scratch_shapes=[pltpu.VMEM((tp + 2 * halo, POOL_WIDTH), F32)],
        compiler_params=_cparams("parallel", "parallel"),
        name="multiscale_pool",
    )(pin, pin, pin, w_bd_bf16, scale.reshape(1, POOL_WIDTH))


def _out_kernel(na_ref, yf_ref, yb_ref, g_ref, pool_ref, x_ref, g1_ref, gnw_ref, avg_ref, wo_ref, o_ref):
    y = yf_ref[0] + yb_ref[0]
    avg = avg_ref[...]
    mu = jnp.dot(y, avg, precision=HIGHEST, preferred_element_type=F32)
    d = y - mu
    var = jnp.dot(d * d, avg, precision=HIGHEST, preferred_element_type=F32)
    yn = d * lax.rsqrt(var + NORM_EPS) * gnw_ref[...]
    g = g_ref[0]
    ret = yn * (g * jax.nn.sigmoid(g))
    mix = jnp.dot(na_ref[0].astype(BF16), wo_ref[0:NA_WIDTH, :], preferred_element_type=F32)
    mix += jnp.dot(ret.astype(BF16), wo_ref[NA_WIDTH:NA_WIDTH + RET_WIDTH, :], preferred_element_type=F32)
    mix += jnp.dot(pool_ref[0].astype(BF16), wo_ref[NA_WIDTH + RET_WIDTH:, :], preferred_element_type=F32)
    o_ref[0] = x_ref[0] + g1_ref[0] * mix


def _out_proj(na, yf, yb, p, pool, x, g1, gn_w, avg, wo_bf16):
    b, t, d = x.shape
    tm = min(512, t)
    row = lambda w: pl.BlockSpec((1, tm, w), lambda i, j: (i, j, 0))
    return pl.pallas_call(
        _out_kernel,
        grid=(b, t // tm),
        in_specs=[row(NA_WIDTH), row(RET_WIDTH), row(RET_WIDTH),
                  pl.BlockSpec((1, tm, RET_WIDTH), lambda i, j: (i, j, O_RET_G // RET_WIDTH)),
                  row(POOL_WIDTH), row(d),
                  pl.BlockSpec((1, 1, d), lambda i, j: (i, 0, 0)),
                  pl.BlockSpec((1, RET_WIDTH), lambda i, j: (0, 0)),
                  pl.BlockSpec((RET_WIDTH, RET_WIDTH), lambda i, j: (0, 0)),
                  pl.BlockSpec((d, d), lambda i, j: (0, 0))],
        out_specs=row(d),
        out_shape=jax.ShapeDtypeStruct((b, t, d), F32),
        compiler_params=_cparams("parallel", "parallel"),
        name="out_proj",
    )(na, yf, yb, p, pool, x, g1, gn_w.reshape(1, RET_WIDTH), avg, wo_bf16)


def _topk_rows(s, order=None, payload=None):
    n, m = s.shape
    if order is None:
        order = lax.broadcasted_iota(jnp.int32, (n, m), 0)
    krow = lax.broadcasted_iota(jnp.int32, (PEER_TOPK, m), 0)
    vals = jnp.zeros((PEER_TOPK, m), F32)
    idxs = jnp.zeros((PEER_TOPK, m), jnp.int32)
    for k in range(PEER_TOPK):
        mx = jnp.max(s, axis=0, keepdims=True)
        am = jnp.min(jnp.where(s == mx, order, jnp.iinfo(jnp.int32).max), axis=0, keepdims=True)
        sel = order == am
        pick = am if payload is None else jnp.sum(jnp.where(sel, payload, 0), axis=0, keepdims=True)
        vals = jnp.where(krow == k, mx, vals)
        idxs = jnp.where(krow == k, pick, idxs)
        s = jnp.where(sel, -jnp.inf, s)
    return vals, idxs


def _product_candidates(va, ia, vb, ib):
    k = PEER_TOPK
    sub = 8
    m = va.shape[1]
    row = lax.broadcasted_iota(jnp.int32, (sub, m), 0)
    cand, flat, eid = [], [], []

    def add(v, f, e, nvalid):
        cand.append(v if nvalid >= sub else jnp.where(row < nvalid, v, -jnp.inf))
        flat.append(f)
        eid.append(e)

    for i in range(sub):
        add(va[i:i + 1, :] + vb[0:sub, :], i * k + row, ia[i:i + 1, :] * PEER_NKEYS + ib[0:sub, :], k // (i + 1))
    add(va[0:1, :] + vb[sub:k, :], sub + row, ia[0:1, :] * PEER_NKEYS + ib[sub:k, :], sub)
    add(va[sub:k, :] + vb[0:1, :], (sub + row) * k, ia[sub:k, :] * PEER_NKEYS + ib[0:1, :], sub)
    return jnp.concatenate(cand, axis=0), jnp.concatenate(flat, axis=0), jnp.concatenate(eid, axis=0)


def _route_kernel(x_ref, nw_ref, sh_ref, sc_ref, wq_ref, keys_ref, idx_ref, gate_ref, q_scr, g_scr, i_scr, *, tm):
    h_hi, h_lo = _split_bf16(_rms_mod(x_ref[0], nw_ref[...], sh_ref[0], sc_ref[0]))
    q = (jnp.dot(h_hi, wq_ref[0], preferred_element_type=F32)
         + jnp.dot(h_hi, wq_ref[1], preferred_element_type=F32)
         + jnp.dot(h_lo, wq_ref[0], preferred_element_type=F32))
    for j in range(2 * PEER_HEADS):
        q_scr[j] = q[:, j * PEER_KEY_DIM:(j + 1) * PEER_KEY_DIM]

    def scores(half, hh):
        q_hi, q_lo = _split_bf16(q_scr[2 * hh + half])
        k_hi = keys_ref[0, half, hh]
        return _dot_nt(k_hi, q_hi) + _dot_nt(k_hi, q_lo) + _dot_nt(keys_ref[1, half, hh], q_hi)

    def head(hh, carry):
        sa = scores(0, hh)
        sb = scores(1, hh)
        va, ia = _topk_rows(sa)
        vb, ib = _topk_rows(sb)
        cand, flat, eid = _product_candidates(va, ia, vb, ib)
        sc, ei = _topk_rows(cand, order=flat, payload=eid)
        e = jnp.exp(sc - sc[0:1, :])
        off = pl.multiple_of(hh * PEER_TOPK, PEER_TOPK)
        g_scr[pl.ds(off, PEER_TOPK), :] = e / jnp.sum(e, axis=0, keepdims=True)
        i_scr[pl.ds(off, PEER_TOPK), :] = ei
        return carry

    lax.fori_loop(0, PEER_HEADS, head, 0)
    gate_ref[0] = g_scr[...].T
    idx_ref[0] = i_scr[...].T * TABLE_ROWS_PER_EXPERT


def _route(x, nw, shift, scale, wq, keys):
    b, t, d = x.shape
    tm = min(256, t)
    nt = t // tm
    nq = wq.shape[1]
    return pl.pallas_call(
        functools.partial(_route_kernel, tm=tm),
        grid=(b, nt),
        in_specs=[pl.BlockSpec((1, tm, d), lambda i, j: (i, j, 0)),
                  pl.BlockSpec((1, d), lambda i, j: (0, 0)),
                  pl.BlockSpec((1, 1, d), lambda i, j: (i, 0, 0)),
                  pl.BlockSpec((1, 1, d), lambda i, j: (i, 0, 0)),
                  pl.BlockSpec((2, d, nq), lambda i, j: (0, 0, 0)),
                  pl.BlockSpec((2, 2, PEER_HEADS, PEER_NKEYS, PEER_KEY_DIM), lambda i, j: (0, 0, 0, 0, 0))],
        out_specs=[pl.BlockSpec((1, tm, N_SEL), lambda i, j: (i, j, 0)),
                   pl.BlockSpec((1, tm, N_SEL), lambda i, j: (i, j, 0))],
        out_shape=[jax.ShapeDtypeStruct((b, t, N_SEL), jnp.int32),
                   jax.ShapeDtypeStruct((b, t, N_SEL), F32)],
        scratch_shapes=[pltpu.VMEM((2 * PEER_HEADS, tm, PEER_KEY_DIM), F32),
                        pltpu.VMEM((N_SEL, tm), F32),
                        pltpu.VMEM((N_SEL, tm), jnp.int32)],
        compiler_params=_cparams("parallel", "parallel"),
        name="peer_route",
    )(x, nw.reshape(1, d), shift, scale, jnp.stack(_split_bf16(wq)), jnp.stack(_split_bf16(keys)))


def _pack_table(tab):
    n, d = tab.shape
    bits = lax.bitcast_convert_type(tab.astype(BF16), jnp.uint16).astype(jnp.uint32)
    words = bits[:, :d // 2] | (bits[:, d // 2:] << 16)
    return words.reshape(n * TABLE_ROWS_PER_EXPERT, LANES)


def _token_loop(tb, token, per_trip):
    def trip(i, carry):
        for j in range(per_trip):
            token(i * per_trip + j, j)
        return carry

    lax.fori_loop(0, tb // per_trip, trip, 0)


def _unpack(slab):
    lo = lax.bitcast_convert_type(slab << 16, F32)
    hi = lax.bitcast_convert_type(slab & jnp.uint32(0xFFFF0000), F32)
    return lo, hi


def _peer_act_kernel(idx_ref, tab_ref, x_ref, nw_ref, sh_ref, sc_ref, gate_ref, w_ref, h_scr, act_scr, *, tb):
    h = _rms_mod(x_ref[...], nw_ref[...], sh_ref[0], sc_ref[0])
    half = CHUNKS // 2
    for c in range(CHUNKS):
        piece = h[:, c * LANES:(c + 1) * LANES]
        base = (c // half) * CHUNKS + c % half
        h_scr[pl.ds(base, tb, stride=2 * CHUNKS), :] = piece
        h_scr[pl.ds(base + half, tb, stride=2 * CHUNKS), :] = piece
    sub = lax.broadcasted_iota(jnp.int32, (CHUNKS, N_SEL), 0)
    lane = lax.broadcasted_iota(jnp.int32, (CHUNKS, N_SEL), 1)
    slot = lane - PAIR_STRIDE * (sub >= half).astype(jnp.int32)

    def token(t, _):
        off = pl.multiple_of(t * 2 * CHUNKS, 2 * CHUNKS)
        hlo = h_scr[pl.ds(off, CHUNKS), :]
        hhi = h_scr[pl.ds(off + CHUNKS, CHUNKS), :]
        acc = jnp.zeros((CHUNKS, N_SEL), F32)
        for a, slab in _slab_pairs(tab_ref, idx_ref, t):
            lo, hi = _unpack(slab)
            dot = jnp.sum(lo * hlo + hi * hhi, axis=-1, keepdims=True)
            acc = jnp.where(slot == a, dot, acc)
        act_scr[pl.ds(t, 1), :] = jnp.sum(acc, axis=0, keepdims=True)

    _token_loop(tb, token, 4)
    a = act_scr[...]
    w_ref[...] = gate_ref[...] * (0.5 * a * (1.0 + lax.erf(a * (2.0 ** -0.5))))


PAIR_STRIDE = N_SEL // 4


def _slab_pairs(tab_ref, idx_ref, t):
    views = [idx_ref.at[pl.ds(t * N_SEL + q * PAIR_STRIDE, PAIR_STRIDE)] for q in range(4)]
    for k in range(PAIR_STRIDE):
        rows = [pl.multiple_of(v[k], TABLE_ROWS_PER_EXPERT) for v in views]
        for q in (0, 2):
            yield q * PAIR_STRIDE + k, jnp.concatenate(
                [tab_ref[pl.ds(rows[q], TABLE_ROWS_PER_EXPERT), :],
                 tab_ref[pl.ds(rows[q + 1], TABLE_ROWS_PER_EXPERT), :]], axis=0)


def _peer_out_kernel(idx_ref, w_ref, tab_ref, x_ref, g2_ref, fw_ref, o_ref, p_scr, *, tb, final):
    half = CHUNKS // 2

    def token(t, _):
        accs = [jnp.zeros((half, LANES), F32) for _ in range(4)]
        for s in range(N_SEL):
            row = pl.multiple_of(idx_ref[t * N_SEL + s], TABLE_ROWS_PER_EXPERT)
            w = w_ref[t * N_SEL + s]
            lo, hi = _unpack(tab_ref[pl.ds(row, TABLE_ROWS_PER_EXPERT), :])
            j = 2 * (s % 2)
            accs[j] = accs[j] + w * lo
            accs[j + 1] = accs[j + 1] + w * hi
        off = pl.multiple_of(t * CHUNKS, CHUNKS)
        p_scr[pl.ds(off, half), :] = accs[0] + accs[2]
        p_scr[pl.ds(pl.multiple_of(off + half, half), half), :] = accs[1] + accs[3]

    _token_loop(tb, token, 1)
    peer = jnp.concatenate([p_scr[pl.ds(c, tb, stride=CHUNKS), :] for c in range(CHUNKS)], axis=1)
    y = x_ref[...] + g2_ref[0] * peer
    if final:
        y = y * lax.rsqrt(jnp.mean(y * y, axis=-1, keepdims=True) + NORM_EPS) * fw_ref[...]
    o_ref[...] = y


def _peer_residual(x, nw, shift, scale, g2, wq, keys, u, v, final_w, final):
    b, t, d = x.shape
    assert d == CHUNKS * LANES
    idx, gate = _route(x, nw, shift, scale, wq, keys)
    n = b * t
    tb = TOK_BLOCK
    per_batch = t // tb
    x2 = x.reshape(n, d)
    idx1 = idx.reshape(n * N_SEL)
    smem_blk = pl.BlockSpec((tb * N_SEL,), lambda i: (i,), memory_space=pltpu.SMEM)
    table = pl.BlockSpec((u.shape[0] * TABLE_ROWS_PER_EXPERT, LANES), lambda i: (0, 0),
                         pipeline_mode=pl.Buffered(1))
    rows = pl.BlockSpec((tb, d), lambda i: (i, 0))
    sel = pl.BlockSpec((tb, N_SEL), lambda i: (i, 0))
    vec = pl.BlockSpec((1, 1, d), lambda i: (i // per_batch, 0, 0))
    const = pl.BlockSpec((1, d), lambda i: (0, 0))
    w = pl.pallas_call(
        functools.partial(_peer_act_kernel, tb=tb),
        grid=(n // tb,),
        in_specs=[smem_blk, table, rows, const, vec, vec, sel],
        out_specs=sel,
        out_shape=jax.ShapeDtypeStruct((n, N_SEL), F32),
        scratch_shapes=[pltpu.VMEM((tb * 2 * CHUNKS, LANES), F32), pltpu.VMEM((tb, N_SEL), F32)],
        compiler_params=_cparams("arbitrary"),
        name="peer_act",
    )(idx1, _pack_table(u), x2, nw.reshape(1, d), shift, scale, gate.reshape(n, N_SEL))
    y = pl.pallas_call(
        functools.partial(_peer_out_kernel, tb=tb, final=final),
        grid=(n // tb,),
        in_specs=[smem_blk, smem_blk, table, rows, vec, const],
        out_specs=rows,
        out_shape=jax.ShapeDtypeStruct((n, d), F32),
        scratch_shapes=[pltpu.VMEM((tb * CHUNKS, LANES), F32)],
        compiler_params=_cparams("arbitrary"),
        name="peer_out",
    )(idx1, w.reshape(n * N_SEL), _pack_table(v), x2, g2, final_w.reshape(1, d))
    return y.reshape(b, t, d)


def _na_bias_table(rpb):
    cq = jnp.arange(GRID_W)
    coff = jnp.clip(cq[None, :] - cq[:, None] + (NA_WIN_C - 1), 0, 2 * NA_WIN_C - 2)
    c_start = jnp.clip(cq - NA_WIN_C // 2, 0, GRID_W - NA_WIN_C)
    ok = (cq[None, :] >= c_start[:, None]) & (cq[None, :] < c_start[:, None] + NA_WIN_C)
    roff = jnp.arange(NA_WIN_R)[:, None] + jnp.arange(NA_WIN_R)[None, :]
    bias = rpb.astype(F32)[:, roff][..., coff]
    bias = jnp.where(ok[None, None, None], bias, NEG_INF)
    bias = bias.transpose(0, 1, 3, 2, 4).reshape(HEAD_PAIRS, 2, NA_WIN_R, GRID_W, NA_WIN_R * GRID_W)
    return bias.transpose(0, 2, 1, 3, 4).reshape(HEAD_PAIRS, NA_WIN_R, 2 * GRID_W, NA_WIN_R * GRID_W)


def _rope_tables(s):
    t = jnp.arange(s)
    row = (t // GRID_W).astype(F32)
    col = (t % GRID_W).astype(F32)
    inv = ROPE_BASE ** (-jnp.arange(ROPE_PAIRS, dtype=F32) / ROPE_PAIRS)
    cr, sr = jnp.cos(row[:, None] * inv), jnp.sin(row[:, None] * inv)
    cc, sc = jnp.cos(col[:, None] * inv), jnp.sin(col[:, None] * inv)
    cos = jnp.concatenate([cr, cr, cc, cc], axis=-1)
    sin = jnp.concatenate([-sr, sr, -sc, sc], axis=-1)
    return jnp.tile(cos, (1, 2)), jnp.tile(sin, (1, 2))


def _block_diag(blocks):
    n, a, bb = blocks.shape
    eye = jnp.eye(n, dtype=blocks.dtype)
    return (eye[:, None, :, None] * blocks[:, :, None, :]).reshape(n * a, n * bb)


def kernel(x, c, ctx, c_ctx, norm1_w, norm2_w, w_ada, b_ada, w_in, w_out, na_rpb, ret_decay_fwd, ret_decay_bwd, ret_gn_w, pool_w, pool_scale, peer_wq, peer_keys, peer_u, peer_v, final_norm_w):
    b, s, d = x.shape
    depth = w_in.shape[0]
    clen = ctx.shape[1]
    cvec = jnp.concatenate([c, c_ctx[None, :], jnp.zeros((8 - b - 1, d), F32)], axis=0)
    mod = _modulation(cvec, w_ada, b_ada)
    cos_x, sin_x = _rope_tables(s)
    cos_c, sin_c = jnp.ones((clen, LANES), F32), jnp.zeros((clen, LANES), F32)
    avg = _block_diag(jnp.full((RET_HEADS, HEAD_DIM, HEAD_DIM), 1.0 / HEAD_DIM, F32))
    zero_state = jnp.zeros((b, HEAD_PAIRS, LANES, LANES), F32)

    for l in range(depth):
        last = l == depth - 1
        mx = mod[l, :b].reshape(b, 1, 6, d)
        mc = jnp.broadcast_to(mod[l, b].reshape(1, 1, 6, d), (b, 1, 6, d))
        sh1, sc1, g1, sh2, sc2, g2 = [mx[:, :, i] for i in range(6)]
        csh1, csc1, cg1, csh2, csc2, cg2 = [mc[:, :, i] for i in range(6)]
        lg = jnp.stack([jax.nn.log_sigmoid(ret_decay_fwd[l].astype(F32)),
                        jax.nn.log_sigmoid(ret_decay_bwd[l].astype(F32))], axis=0)
        wi = w_in[l].astype(BF16)
        wo = w_out[l].astype(BF16)
        wpool = _block_diag(pool_w[l]).astype(BF16)
        bias8 = _na_bias_table(na_rpb[l])

        pc, pcp = _in_proj(ctx, norm1_w[l], csh1, csc1, wi)
        ycf, ycb, r_f, r_b = _retention(pc, lg, cos_c, sin_c, zero_state, zero_state)

        px, pxp = _in_proj(x, norm1_w[l], sh1, sc1, wi)
        na = _na_attention(px, pc, bias8)
        yf, yb, _, _ = _retention(px, lg, cos_x, sin_x, r_f, r_b)
        pool = _pool(pxp, wpool, pool_scale[l])
        x = _out_proj(na, yf, yb, px, pool, x, g1, ret_gn_w[l], avg, wo)
        x = _peer_residual(x, norm2_w[l], sh2, sc2, g2, peer_wq[l], peer_keys[l], peer_u[l], peer_v[l],
                           final_norm_w, last)

        if not last:
            na_c = _ctx_attention(pc)
            pool_c = _pool(pcp, wpool, pool_scale[l])
            ctx = _out_proj(na_c, ycf, ycb, pc, pool_c, ctx, cg1, ret_gn_w[l], avg, wo)
            ctx = _peer_residual(ctx, norm2_w[l], csh2, csc2, cg2, peer_wq[l], peer_keys[l], peer_u[l],
                                 peer_v[l], final_norm_w, False)
    return x
```

```python
import functools

import jax
import jax.numpy as jnp
from jax import lax
from jax.experimental import pallas as pl
from jax.experimental.pallas import tpu as pltpu

D_MODEL = 1024
GRID_W = 64
HEAD_DIM = 64
NA_HEADS = 6
NA_WIN_R = 8
NA_WIN_C = 16
RET_HEADS = 6
RET_CHUNK = 128
POOL_WINDOWS = (2, 4, 8, 16)
POOL_GROUP = 64
NA_WIDTH = NA_HEADS * HEAD_DIM
RET_WIDTH = RET_HEADS * HEAD_DIM
POOL_WIDTH = POOL_GROUP * len(POOL_WINDOWS)
O_RET_G = 3 * NA_WIDTH + 3 * RET_WIDTH
O_POOL = O_RET_G + RET_WIDTH
D_PROJ = O_POOL + POOL_WIDTH
ROPE_BASE = 10000.0
ROPE_PAIRS = HEAD_DIM // 4
PEER_HEADS = 8
PEER_NKEYS = 128
PEER_KEY_DIM = 128
PEER_TOPK = 16
N_SEL = PEER_HEADS * PEER_TOPK
NORM_EPS = 1e-6
NEG_INF = -1e30

LANES = 128
VMEM_LIMIT_BYTES = 56 * 1024 * 1024

F32 = jnp.float32
BF16 = jnp.bfloat16
HIGHEST = lax.Precision.HIGHEST
HEAD_PAIRS = NA_HEADS // 2
TOK_BLOCK = 128
CHUNKS = D_MODEL // LANES
TABLE_ROWS_PER_EXPERT = CHUNKS // 2


def _cparams(*sem):
    return pltpu.CompilerParams(dimension_semantics=sem, vmem_limit_bytes=VMEM_LIMIT_BYTES)


def _dot_nt(a, b, precision=None):
    return lax.dot_general(a, b, (((1,), (1,)), ((), ())), precision=precision,
                           preferred_element_type=F32)


def _split_bf16(a):
    hi = a.astype(BF16)
    return hi, (a - hi.astype(F32)).astype(BF16)


def _rms_mod(x, nw, shift, scale):
    y = x * lax.rsqrt(jnp.mean(x * x, axis=-1, keepdims=True) + NORM_EPS)
    return (y * nw) * (1.0 + scale) + shift


def _mod_kernel(c_ref, w_ref, b_ref, o_ref):
    c = c_ref[...]
    a = c * jax.nn.sigmoid(c)
    o_ref[0] = jnp.dot(a, w_ref[0], precision=HIGHEST, preferred_element_type=F32) + b_ref[0]


def _modulation(cvec, w_ada, b_ada):
    depth, d, n = w_ada.shape
    tn = 1536
    return pl.pallas_call(
        _mod_kernel,
        grid=(depth, n // tn),
        in_specs=[pl.BlockSpec((8, d), lambda l, j: (0, 0)),
                  pl.BlockSpec((1, d, tn), lambda l, j: (l, 0, j)),
                  pl.BlockSpec((1, 1, tn), lambda l, j: (l, 0, j))],
        out_specs=pl.BlockSpec((1, 8, tn), lambda l, j: (l, 0, j)),
        out_shape=jax.ShapeDtypeStruct((depth, 8, n), F32),
        compiler_params=_cparams("parallel", "parallel"),
        name="adaln_mod",
    )(cvec, w_ada, b_ada.reshape(depth, 1, n))


def _inproj_kernel(x_ref, nw_ref, sh_ref, sc_ref, w_ref, o_ref, p_ref):
    h = _rms_mod(x_ref[0], nw_ref[...], sh_ref[0], sc_ref[0])
    r = jnp.dot(h.astype(BF16), w_ref[...], preferred_element_type=F32)
    o_ref[0] = r[:, :O_POOL]
    p_ref[0] = r[:, O_POOL:]


def _in_proj(x, nw, shift, scale, w_bf16):
    b, t, d = x.shape
    tm = min(512, t)
    return pl.pallas_call(
        _inproj_kernel,
        grid=(b, t // tm),
        in_specs=[pl.BlockSpec((1, tm, d), lambda i, j: (i, j, 0)),
                  pl.BlockSpec((1, d), lambda i, j: (0, 0)),
                  pl.BlockSpec((1, 1, d), lambda i, j: (i, 0, 0)),
                  pl.BlockSpec((1, 1, d), lambda i, j: (i, 0, 0)),
                  pl.BlockSpec((d, D_PROJ), lambda i, j: (0, 0))],
        out_specs=[pl.BlockSpec((1, tm, O_POOL), lambda i, j: (i, j, 0)),
                   pl.BlockSpec((1, tm, POOL_WIDTH), lambda i, j: (i, j, 0))],
        out_shape=[jax.ShapeDtypeStruct((b, t, O_POOL), F32),
                   jax.ShapeDtypeStruct((b, t, POOL_WIDTH), F32)],
        compiler_params=_cparams("parallel", "parallel"),
        name="in_proj",
    )(x, nw.reshape(1, d), shift, scale, w_bf16)


def _softmax_pv(s_list, v_list):
    m = s_list[0].max(axis=-1, keepdims=True)
    for s in s_list[1:]:
        m = jnp.maximum(m, s.max(axis=-1, keepdims=True))
    num = None
    den = None
    for s, v in zip(s_list, v_list):
        p = jnp.exp(s - m)
        pv = jnp.dot(p.astype(BF16), v.astype(BF16), preferred_element_type=F32)
        ps = p.sum(axis=-1, keepdims=True)
        num = pv if num is None else num + pv
        den = ps if den is None else den + ps
    return num / den


NA_ROWS_PER_TRIP = 4


def _na_kernel(q_ref, k_ref, v_ref, kc_ref, vc_ref, bias_ref, o_ref, *, rows, rb):
    i = pl.program_id(2)
    lane = lax.broadcasted_iota(jnp.int32, (1, LANES), 1)
    first = lane < HEAD_DIM
    kc = kc_ref[0].astype(BF16)
    vc = vc_ref[0].astype(BF16)
    scale = HEAD_DIM ** -0.5
    nk = NA_WIN_R * GRID_W

    def body(it, carry):
        for u in range(NA_ROWS_PER_TRIP):
            one_row(it * NA_ROWS_PER_TRIP + u)
        return carry

    def one_row(rr):
        r = i * rb + rr
        rs = jnp.clip(r - NA_WIN_R // 2, 0, rows - NA_WIN_R)
        delta = rs - r + (NA_WIN_R - 1)
        q = q_ref[0, pl.ds(pl.multiple_of(rr * GRID_W, GRID_W), GRID_W), :]
        k = k_ref[0, pl.ds(pl.multiple_of(rs * GRID_W, GRID_W), nk), :].astype(BF16)
        v = v_ref[0, pl.ds(pl.multiple_of(rs * GRID_W, GRID_W), nk), :].astype(BF16)
        q2 = jnp.concatenate([jnp.where(first, q, 0.0), jnp.where(first, 0.0, q)], axis=0).astype(BF16)
        s = _dot_nt(q2, k) * scale + bias_ref[0, delta]
        sc = _dot_nt(q2, kc) * scale
        o2 = _softmax_pv([s, sc], [v, vc])
        o_ref[0, pl.ds(pl.multiple_of(rr * GRID_W, GRID_W), GRID_W), :] = jnp.where(
            first, o2[0:GRID_W], o2[GRID_W:2 * GRID_W])

    lax.fori_loop(0, rb // NA_ROWS_PER_TRIP, body, 0)


def _na_attention(px, pc, bias8):
    b, s, _ = px.shape
    c = pc.shape[1]
    rows = s // GRID_W
    assert rows >= NA_WIN_R and s % GRID_W == 0
    rb = min(8, rows)
    nq, nkb, nvb = 0, HEAD_PAIRS, 2 * HEAD_PAIRS
    return pl.pallas_call(
        functools.partial(_na_kernel, rows=rows, rb=rb),
        grid=(b, HEAD_PAIRS, rows // rb),
        in_specs=[pl.BlockSpec((1, rb * GRID_W, LANES), lambda bi, hp, i: (bi, i, nq + hp)),
                  pl.BlockSpec((1, s, LANES), lambda bi, hp, i: (bi, 0, nkb + hp)),
                  pl.BlockSpec((1, s, LANES), lambda bi, hp, i: (bi, 0, nvb + hp)),
                  pl.BlockSpec((1, c, LANES), lambda bi, hp, i: (bi, 0, nkb + hp)),
                  pl.BlockSpec((1, c, LANES), lambda bi, hp, i: (bi, 0, nvb + hp)),
                  pl.BlockSpec((1, NA_WIN_R, 2 * GRID_W, NA_WIN_R * GRID_W),
                               lambda bi, hp, i: (hp, 0, 0, 0))],
        out_specs=pl.BlockSpec((1, rb * GRID_W, LANES), lambda bi, hp, i: (bi, i, hp)),
        out_shape=jax.ShapeDtypeStruct((b, s, NA_WIDTH), F32),
        compiler_params=_cparams("parallel", "parallel", "arbitrary"),
        name="na_attention",
    )(px, px, px, pc, pc, bias8)


def _ctx_attn_kernel(q_ref, k_ref, v_ref, o_ref):
    lane = lax.broadcasted_iota(jnp.int32, (1, LANES), 1)
    first = lane < HEAD_DIM
    q = q_ref[0]
    k = k_ref[0]
    v = v_ref[0]
    outs = []
    for h in range(2):
        qm = jnp.where(first if h == 0 else jnp.logical_not(first), q, 0.0)
        s = _dot_nt(qm, k) * HEAD_DIM ** -0.5
        outs.append(_softmax_pv([s], [v]))
    o_ref[0] = jnp.where(first, outs[0], outs[1])


def _ctx_attention(pc):
    b, c, _ = pc.shape
    return pl.pallas_call(
        _ctx_attn_kernel,
        grid=(b, HEAD_PAIRS),
        in_specs=[pl.BlockSpec((1, c, LANES), lambda bi, hp: (bi, 0, hp)),
                  pl.BlockSpec((1, c, LANES), lambda bi, hp: (bi, 0, HEAD_PAIRS + hp)),
                  pl.BlockSpec((1, c, LANES), lambda bi, hp: (bi, 0, 2 * HEAD_PAIRS + hp))],
        out_specs=pl.BlockSpec((1, c, LANES), lambda bi, hp: (bi, 0, hp)),
        out_shape=jax.ShapeDtypeStruct((b, c, NA_WIDTH), F32),
        compiler_params=_cparams("parallel", "parallel"),
        name="ctx_attention",
    )(pc, pc, pc)


RET_CHUNKS_PER_STEP = 4


def _ret_kernel(lg_ref, qf_ref, kf_ref, vf_ref, qb_ref, kb_ref, vb_ref,
                cf_ref, sf_ref, cb_ref, sb_ref, rf0_ref, rb0_ref,
                yf_ref, yb_ref, rfo_ref, rbo_ref, rf_scr, rb_scr, *, cpb):
    hp = pl.program_id(1)
    c = pl.program_id(2)
    nc = pl.num_programs(2)
    cs = RET_CHUNK

    @pl.when(c == 0)
    def _():
        rf_scr[...] = rf0_ref[0, 0]
        rb_scr[...] = rb0_ref[0, 0]

    lane = lax.broadcasted_iota(jnp.int32, (1, LANES), 1)
    first = lane < HEAD_DIM
    low = (lane % (2 * ROPE_PAIRS)) < ROPE_PAIRS
    pos = lax.broadcasted_iota(jnp.int32, (cs, 1), 0).astype(F32)
    ii = lax.broadcasted_iota(jnp.int32, (cs, cs), 0)
    jj = lax.broadcasted_iota(jnp.int32, (cs, cs), 1)
    diff = (ii - jj).astype(F32)
    same_head = (ii < HEAD_DIM) == (jj < HEAD_DIM)
    scale = HEAD_DIM ** -0.5

    def rope(x, cos, sin):
        swapped = jnp.where(low, pltpu.roll(x, LANES - ROPE_PAIRS, 1), pltpu.roll(x, ROPE_PAIRS, 1))
        return x * cos + swapped * sin

    def direction(d, q_ref, k_ref, v_ref, cos_ref, sin_ref, r_scr, y_ref):
        lg0 = lg_ref[d, 2 * hp]
        lg1 = lg_ref[d, 2 * hp + 1]
        lgv = jnp.where(first, lg0, lg1)
        decs = []
        for lg in (lg0, lg1):
            if d == 0:
                decs.append(jnp.where(diff >= 0, jnp.exp(jnp.maximum(diff, 0.0) * lg), 0.0))
            else:
                decs.append(jnp.where(diff <= 0, jnp.exp(jnp.maximum(-diff, 0.0) * lg), 0.0))
        dec2 = jnp.concatenate(decs, axis=0)
        if d == 0:
            xi = jnp.exp((pos + 1.0) * lgv)
            zeta = jnp.exp((cs - 1.0 - pos) * lgv)
        else:
            xi = jnp.exp((cs - pos) * lgv)
            zeta = jnp.exp(pos * lgv)
        chunk_decay = jnp.exp(cs * lgv)
        r = r_scr[...]
        for j in (range(cpb) if d == 0 else reversed(range(cpb))):
            rows = pl.ds(j * cs, cs)
            cos = cos_ref[rows, :]
            sin = sin_ref[rows, :]
            q = rope(q_ref[0, rows, :], cos, sin)
            k = rope(k_ref[0, rows, :], cos, sin) * scale
            v = v_ref[0, rows, :].astype(BF16)
            q2 = jnp.concatenate([jnp.where(first, q, 0.0), jnp.where(first, 0.0, q)], axis=0).astype(BF16)
            s2 = _dot_nt(q2, k.astype(BF16)) * dec2
            o2 = jnp.dot(s2.astype(BF16), v, preferred_element_type=F32)
            inner = jnp.where(first, o2[0:cs], o2[cs:2 * cs])
            y_ref[0, rows, :] = inner + jnp.dot((q * xi).astype(BF16), r.astype(BF16),
                                                preferred_element_type=F32)
            kv = lax.dot_general((k * zeta).astype(BF16), v, (((0,), (0,)), ((), ())),
                                 preferred_element_type=F32)
            r = chunk_decay * r + jnp.where(same_head, kv, 0.0)
        r_scr[...] = r

    direction(0, qf_ref, kf_ref, vf_ref, cf_ref, sf_ref, rf_scr, yf_ref)
    direction(1, qb_ref, kb_ref, vb_ref, cb_ref, sb_ref, rb_scr, yb_ref)

    @pl.when(c == nc - 1)
    def _():
        rfo_ref[0, 0] = rf_scr[...]
        rbo_ref[0, 0] = rb_scr[...]


def _retention(p, lg, cos_t, sin_t, rf0, rb0):
    b, t, _ = p.shape
    cpb = min(RET_CHUNKS_PER_STEP, t // RET_CHUNK)
    nc = t // (RET_CHUNK * cpb)
    qo, ko, vo = 3 * HEAD_PAIRS, 4 * HEAD_PAIRS, 5 * HEAD_PAIRS
    blk = (1, RET_CHUNK * cpb, LANES)
    fwd = lambda o: pl.BlockSpec(blk, lambda bi, hp, c: (bi, c, o + hp))
    bwd = lambda o: pl.BlockSpec(blk, lambda bi, hp, c: (bi, nc - 1 - c, o + hp))
    tab_f = pl.BlockSpec((RET_CHUNK * cpb, LANES), lambda bi, hp, c: (c, 0))
    tab_b = pl.BlockSpec((RET_CHUNK * cpb, LANES), lambda bi, hp, c: (nc - 1 - c, 0))
    st = pl.BlockSpec((1, 1, LANES, LANES), lambda bi, hp, c: (bi, hp, 0, 0))
    return pl.pallas_call(
        functools.partial(_ret_kernel, cpb=cpb),
        grid=(b, HEAD_PAIRS, nc),
        in_specs=[pl.BlockSpec(memory_space=pltpu.SMEM),
                  fwd(qo), fwd(ko), fwd(vo), bwd(qo), bwd(ko), bwd(vo),
                  tab_f, tab_f, tab_b, tab_b, st, st],
        out_specs=[pl.BlockSpec(blk, lambda bi, hp, c: (bi, c, hp)),
                   pl.BlockSpec(blk, lambda bi, hp, c: (bi, nc - 1 - c, hp)),
                   st, st],
        out_shape=[jax.ShapeDtypeStruct((b, t, RET_WIDTH), F32),
                   jax.ShapeDtypeStruct((b, t, RET_WIDTH), F32),
                   jax.ShapeDtypeStruct((b, HEAD_PAIRS, LANES, LANES), F32),
                   jax.ShapeDtypeStruct((b, HEAD_PAIRS, LANES, LANES), F32)],
        scratch_shapes=[pltpu.VMEM((LANES, LANES), F32), pltpu.VMEM((LANES, LANES), F32)],
        compiler_params=_cparams("parallel", "parallel", "arbitrary"),
        name="retention",
    )(lg, p, p, p, p, p, p, cos_t, sin_t, cos_t, sin_t, rf0, rb0)


def _pool_kernel(prev_ref, cur_ref, next_ref, w_ref, s_ref, o_ref, scr, *, t_total, tp):
    i = pl.program_id(1)
    n = pl.num_programs(1)
    halo = POOL_WINDOWS[-1] // 2
    x = cur_ref[0]
    scr[0:halo, :] = jnp.where(i > 0, prev_ref[0], 0.0)
    scr[halo:halo + tp, :] = x
    scr[halo + tp:2 * halo + tp, :] = jnp.where(i < n - 1, next_ref[0], 0.0)
    t = i * tp + lax.broadcasted_iota(jnp.int32, (tp, 1), 0)
    lane = lax.broadcasted_iota(jnp.int32, (1, POOL_WIDTH), 1)

    def shifted(s):
        return scr[halo + s:halo + s + tp, :]

    acc = None
    mean = None
    done = 0
    for g, w in enumerate(POOL_WINDOWS):
        half = w // 2
        for s in list(range(-half, -done)) + list(range(done, half)):
            sh = x if s == 0 else shifted(s)
            acc = sh if acc is None else acc + sh
        done = half
        cnt = (jnp.minimum(t + half, t_total) - jnp.maximum(t - half, 0)).astype(F32)
        mg = acc / cnt
        mean = mg if mean is None else jnp.where(lane >= g * POOL_GROUP, mg, mean)
    dlt = (mean - x).astype(BF16)
    o_ref[0] = jnp.dot(dlt, w_ref[...], preferred_element_type=F32) * s_ref[...]


def _pool(pin, w_bd_bf16, scale):
    b, t, _ = pin.shape
    tp = min(1024, t)
    halo = POOL_WINDOWS[-1] // 2
    nh = tp // halo
    last = t // halo - 1
    return pl.pallas_call(
        functools.partial(_pool_kernel, t_total=t, tp=tp),
        grid=(b, t // tp),
        in_specs=[pl.BlockSpec((1, halo, POOL_WIDTH), lambda bi, i: (bi, jnp.maximum(i * nh - 1, 0), 0)),
                  pl.BlockSpec((1, tp, POOL_WIDTH), lambda bi, i: (bi, i, 0)),
                  pl.BlockSpec((1, halo, POOL_WIDTH), lambda bi, i: (bi, jnp.minimum((i + 1) * nh, last), 0)),
                  pl.BlockSpec((POOL_WIDTH, POOL_WIDTH), lambda bi, i: (0, 0)),
                  pl.BlockSpec((1, POOL_WIDTH), lambda bi, i: (0, 0))],
        out_specs=pl.BlockSpec((1, tp, POOL_WIDTH), lambda bi, i: (bi, i, 0)),
        out_shape=jax.ShapeDtypeStruct((b, t, POOL_WIDTH), F32),
        scratch_shapes=[pltpu.VMEM((tp + 2 * halo, POOL_WIDTH), F32)],
        compiler_params=_cparams("parallel", "parallel"),
        name="multiscale_pool",
    )(pin, pin, pin, w_bd_bf16, scale.reshape(1, POOL_WIDTH))


def _out_kernel(na_ref, yf_ref, yb_ref, g_ref, pool_ref, x_ref, g1_ref, gnw_ref, avg_ref, wo_ref, o_ref):
    y = yf_ref[0] + yb_ref[0]
    avg = avg_ref[...]
    mu = jnp.dot(y, avg, precision=HIGHEST, preferred_element_type=F32)
    d = y - mu
    var = jnp.dot(d * d, avg, precision=HIGHEST, preferred_element_type=F32)
    yn = d * lax.rsqrt(var + NORM_EPS) * gnw_ref[...]
    g = g_ref[0]
    ret = yn * (g * jax.nn.sigmoid(g))
    mix = jnp.dot(na_ref[0].astype(BF16), wo_ref[0:NA_WIDTH, :], preferred_element_type=F32)
    mix += jnp.dot(ret.astype(BF16), wo_ref[NA_WIDTH:NA_WIDTH + RET_WIDTH, :], preferred_element_type=F32)
    mix += jnp.dot(pool_ref[0].astype(BF16), wo_ref[NA_WIDTH + RET_WIDTH:, :], preferred_element_type=F32)
    o_ref[0] = x_ref[0] + g1_ref[0] * mix


def _out_proj(na, yf, yb, p, pool, x, g1, gn_w, avg, wo_bf16):
    b, t, d = x.shape
    tm = min(512, t)
    row = lambda w: pl.BlockSpec((1, tm, w), lambda i, j: (i, j, 0))
    return pl.pallas_call(
        _out_kernel,
        grid=(b, t // tm),
        in_specs=[row(NA_WIDTH), row(RET_WIDTH), row(RET_WIDTH),
                  pl.BlockSpec((1, tm, RET_WIDTH), lambda i, j: (i, j, O_RET_G // RET_WIDTH)),
                  row(POOL_WIDTH), row(d),
                  pl.BlockSpec((1, 1, d), lambda i, j: (i, 0, 0)),
                  pl.BlockSpec((1, RET_WIDTH), lambda i, j: (0, 0)),
                  pl.BlockSpec((RET_WIDTH, RET_WIDTH), lambda i, j: (0, 0)),
                  pl.BlockSpec((d, d), lambda i, j: (0, 0))],
        out_specs=row(d),
        out_shape=jax.ShapeDtypeStruct((b, t, d), F32),
        compiler_params=_cparams("parallel", "parallel"),
        name="out_proj",
    )(na, yf, yb, p, pool, x, g1, gn_w.reshape(1, RET_WIDTH), avg, wo_bf16)


def _topk_rows(s, order=None, payload=None):
    n, m = s.shape
    if order is None:
        order = lax.broadcasted_iota(jnp.int32, (n, m), 0)
    krow = lax.broadcasted_iota(jnp.int32, (PEER_TOPK, m), 0)
    vals = jnp.zeros((PEER_TOPK, m), F32)
    idxs = jnp.zeros((PEER_TOPK, m), jnp.int32)
    for k in range(PEER_TOPK):
        mx = jnp.max(s, axis=0, keepdims=True)
        am = jnp.min(jnp.where(s == mx, order, jnp.iinfo(jnp.int32).max), axis=0, keepdims=True)
        sel = order == am
        pick = am if payload is None else jnp.sum(jnp.where(sel, payload, 0), axis=0, keepdims=True)
        vals = jnp.where(krow == k, mx, vals)
        idxs = jnp.where(krow == k, pick, idxs)
        s = jnp.where(sel, -jnp.inf, s)
    return vals, idxs


def _product_candidates(va, ia, vb, ib):
    k = PEER_TOPK
    sub = 8
    m = va.shape[1]
    row = lax.broadcasted_iota(jnp.int32, (sub, m), 0)
    cand, flat, eid = [], [], []

    def add(v, f, e, nvalid):
        cand.append(v if nvalid >= sub else jnp.where(row < nvalid, v, -jnp.inf))
        flat.append(f)
        eid.append(e)

    for i in range(sub):
        add(va[i:i + 1, :] + vb[0:sub, :], i * k + row, ia[i:i + 1, :] * PEER_NKEYS + ib[0:sub, :], k // (i + 1))
    add(va[0:1, :] + vb[sub:k, :], sub + row, ia[0:1, :] * PEER_NKEYS + ib[sub:k, :], sub)
    add(va[sub:k, :] + vb[0:1, :], (sub + row) * k, ia[sub:k, :] * PEER_NKEYS + ib[0:1, :], sub)
    return jnp.concatenate(cand, axis=0), jnp.concatenate(flat, axis=0), jnp.concatenate(eid, axis=0)


def _route_kernel(x_ref, nw_ref, sh_ref, sc_ref, wq_ref, keys_ref, idx_ref, gate_ref, q_scr, g_scr, i_scr, *, tm):
    h_hi, h_lo = _split_bf16(_rms_mod(x_ref[0], nw_ref[...], sh_ref[0], sc_ref[0]))
    q = (jnp.dot(h_hi, wq_ref[0], preferred_element_type=F32)
         + jnp.dot(h_hi, wq_ref[1], preferred_element_type=F32)
         + jnp.dot(h_lo, wq_ref[0], preferred_element_type=F32))
    for j in range(2 * PEER_HEADS):
        q_scr[j] = q[:, j * PEER_KEY_DIM:(j + 1) * PEER_KEY_DIM]

    def scores(half, hh):
        q_hi, q_lo = _split_bf16(q_scr[2 * hh + half])
        k_hi = keys_ref[0, half, hh]
        return _dot_nt(k_hi, q_hi) + _dot_nt(k_hi, q_lo) + _dot_nt(keys_ref[1, half, hh], q_hi)

    def head(hh, carry):
        sa = scores(0, hh)
        sb = scores(1, hh)
        va, ia = _topk_rows(sa)
        vb, ib = _topk_rows(sb)
        cand, flat, eid = _product_candidates(va, ia, vb, ib)
        sc, ei = _topk_rows(cand, order=flat, payload=eid)
        e = jnp.exp(sc - sc[0:1, :])
        off = pl.multiple_of(hh * PEER_TOPK, PEER_TOPK)
        g_scr[pl.ds(off, PEER_TOPK), :] = e / jnp.sum(e, axis=0, keepdims=True)
        i_scr[pl.ds(off, PEER_TOPK), :] = ei
        return carry

    lax.fori_loop(0, PEER_HEADS, head, 0)
    gate_ref[0] = g_scr[...].T
    idx_ref[0] = i_scr[...].T * TABLE_ROWS_PER_EXPERT


def _route(x, nw, shift, scale, wq, keys):
    b, t, d = x.shape
    tm = min(256, t)
    nt = t // tm
    nq = wq.shape[2]
    return pl.pallas_call(
        functools.partial(_route_kernel, tm=tm),
        grid=(b, nt),
        in_specs=[pl.BlockSpec((1, tm, d), lambda i, j: (i, j, 0)),
                  pl.BlockSpec((1, d), lambda i, j: (0, 0)),
                  pl.BlockSpec((1, 1, d), lambda i, j: (i, 0, 0)),
                  pl.BlockSpec((1, 1, d), lambda i, j: (i, 0, 0)),
                  pl.BlockSpec((2, d, nq), lambda i, j: (0, 0, 0)),
                  pl.BlockSpec((2, 2, PEER_HEADS, PEER_NKEYS, PEER_KEY_DIM), lambda i, j: (0, 0, 0, 0, 0))],
        out_specs=[pl.BlockSpec((1, tm, N_SEL), lambda i, j: (i, j, 0)),
                   pl.BlockSpec((1, tm, N_SEL), lambda i, j: (i, j, 0))],
        out_shape=[jax.ShapeDtypeStruct((b, t, N_SEL), jnp.int32),
                   jax.ShapeDtypeStruct((b, t, N_SEL), F32)],
        scratch_shapes=[pltpu.VMEM((2 * PEER_HEADS, tm, PEER_KEY_DIM), F32),
                        pltpu.VMEM((N_SEL, tm), F32),
                        pltpu.VMEM((N_SEL, tm), jnp.int32)],
        compiler_params=_cparams("parallel", "parallel"),
        name="peer_route",
    )(x, nw.reshape(1, d), shift, scale, wq, keys)


def _pack_kernel(t_ref, o_ref, *, te):
    x = t_ref[...]
    d = x.shape[1]

    def bf16_bits(a):
        return lax.bitcast_convert_type(a.astype(BF16).astype(F32), jnp.uint32)

    words = (bf16_bits(x[:, :d // 2]) >> 16) | bf16_bits(x[:, d // 2:])
    for q in range(TABLE_ROWS_PER_EXPERT):
        o_ref[pl.ds(q, te, stride=TABLE_ROWS_PER_EXPERT), :] = words[:, q * LANES:(q + 1) * LANES]


def _pack_table(tab):
    n, d = tab.shape
    te = 512
    return pl.pallas_call(
        functools.partial(_pack_kernel, te=te),
        grid=(n // te,),
        in_specs=[pl.BlockSpec((te, d), lambda i: (i, 0))],
        out_specs=pl.BlockSpec((te * TABLE_ROWS_PER_EXPERT, LANES), lambda i: (i, 0)),
        out_shape=jax.ShapeDtypeStruct((n * TABLE_ROWS_PER_EXPERT, LANES), jnp.uint32),
        compiler_params=_cparams("parallel"),
        name="pack_table",
    )(tab)


def _token_loop(tb, token, per_trip):
    def trip(i, carry):
        for j in range(per_trip):
            token(i * per_trip + j, j)
        return carry

    lax.fori_loop(0, tb // per_trip, trip, 0)


def _unpack(slab):
    lo = lax.bitcast_convert_type(slab << 16, F32)
    hi = lax.bitcast_convert_type(slab & jnp.uint32(0xFFFF0000), F32)
    return lo, hi


def _peer_act_kernel(idx_ref, tab_ref, x_ref, nw_ref, sh_ref, sc_ref, gate_ref, w_ref, h_scr, act_scr, *, tb):
    h = _rms_mod(x_ref[...], nw_ref[...], sh_ref[0], sc_ref[0])
    half = CHUNKS // 2
    for c in range(CHUNKS):
        piece = h[:, c * LANES:(c + 1) * LANES]
        base = (c // half) * CHUNKS + c % half
        h_scr[pl.ds(base, tb, stride=2 * CHUNKS), :] = piece
        h_scr[pl.ds(base + half, tb, stride=2 * CHUNKS), :] = piece
    sub = lax.broadcasted_iota(jnp.int32, (CHUNKS, N_SEL), 0)
    lane = lax.broadcasted_iota(jnp.int32, (CHUNKS, N_SEL), 1)
    slot = lane - PAIR_STRIDE * (sub >= half).astype(jnp.int32)

    def token(t, _):
        off = pl.multiple_of(t * 2 * CHUNKS, 2 * CHUNKS)
        hlo = h_scr[pl.ds(off, CHUNKS), :]
        hhi = h_scr[pl.ds(off + CHUNKS, CHUNKS), :]
        acc = jnp.zeros((CHUNKS, N_SEL), F32)
        for a, slab in _slab_pairs(tab_ref, idx_ref, t):
            lo, hi = _unpack(slab)
            dot = jnp.sum(lo * hlo + hi * hhi, axis=-1, keepdims=True)
            acc = jnp.where(slot == a, dot, acc)
        act_scr[pl.ds(t, 1), :] = jnp.sum(acc, axis=0, keepdims=True)

    _token_loop(tb, token, 8)
    a = act_scr[...]
    w_ref[...] = gate_ref[...] * (0.5 * a * (1.0 + lax.erf(a * (2.0 ** -0.5))))


PAIR_STRIDE = N_SEL // 4


def _slab_pairs(tab_ref, idx_ref, t):
    views = [idx_ref.at[pl.ds(t * N_SEL + q * PAIR_STRIDE, PAIR_STRIDE)] for q in range(4)]
    for k in range(PAIR_STRIDE):
        rows = [pl.multiple_of(v[k], TABLE_ROWS_PER_EXPERT) for v in views]
        for q in (0, 2):
            yield q * PAIR_STRIDE + k, jnp.concatenate(
                [tab_ref[pl.ds(rows[q], TABLE_ROWS_PER_EXPERT), :],
                 tab_ref[pl.ds(rows[q + 1], TABLE_ROWS_PER_EXPERT), :]], axis=0)


def _peer_out_kernel(idx_ref, w_ref, tab_ref, x_ref, g2_ref, fw_ref, o_ref, p_scr, *, tb, final):
    half = CHUNKS // 2

    def token(t, _):
        accs = [jnp.zeros((half, LANES), F32) for _ in range(4)]
        for s in range(N_SEL):
            row = pl.multiple_of(idx_ref[t * N_SEL + s], TABLE_ROWS_PER_EXPERT)
            w = w_ref[t * N_SEL + s]
            lo, hi = _unpack(tab_ref[pl.ds(row, TABLE_ROWS_PER_EXPERT), :])
            j = 2 * (s % 2)
            accs[j] = accs[j] + w * lo
            accs[j + 1] = accs[j + 1] + w * hi
        off = pl.multiple_of(t * CHUNKS, CHUNKS)
        p_scr[pl.ds(off, half), :] = accs[0] + accs[2]
        p_scr[pl.ds(pl.multiple_of(off + half, half), half), :] = accs[1] + accs[3]

    _token_loop(tb, token, 1)
    peer = jnp.concatenate([p_scr[pl.ds(c, tb, stride=CHUNKS), :] for c in range(CHUNKS)], axis=1)
    y = x_ref[...] + g2_ref[0] * peer
    if final:
        y = y * lax.rsqrt(jnp.mean(y * y, axis=-1, keepdims=True) + NORM_EPS) * fw_ref[...]
    o_ref[...] = y


def _peer_residual(x, nw, shift, scale, g2, wq, keys, u, v, final_w, final):
    b, t, d = x.shape
    assert d == CHUNKS * LANES
    idx, gate = _route(x, nw, shift, scale, wq, keys)
    n = b * t
    tb = TOK_BLOCK
    per_batch = t // tb
    x2 = x.reshape(n, d)
    idx1 = idx.reshape(n * N_SEL)
    smem_blk = pl.BlockSpec((tb * N_SEL,), lambda i: (i,), memory_space=pltpu.SMEM)
    table = pl.BlockSpec(u.shape, lambda i: (0, 0), pipeline_mode=pl.Buffered(1))
    rows = pl.BlockSpec((tb, d), lambda i: (i, 0))
    sel = pl.BlockSpec((tb, N_SEL), lambda i: (i, 0))
    vec = pl.BlockSpec((1, 1, d), lambda i: (i // per_batch, 0, 0))
    const = pl.BlockSpec((1, d), lambda i: (0, 0))
    w = pl.pallas_call(
        functools.partial(_peer_act_kernel, tb=tb),
        grid=(n // tb,),
        in_specs=[smem_blk, table, rows, const, vec, vec, sel],
        out_specs=sel,
        out_shape=jax.ShapeDtypeStruct((n, N_SEL), F32),
        scratch_shapes=[pltpu.VMEM((tb * 2 * CHUNKS, LANES), F32), pltpu.VMEM((tb, N_SEL), F32)],
        compiler_params=_cparams("arbitrary"),
        name="peer_act",
    )(idx1, u, x2, nw.reshape(1, d), shift, scale, gate.reshape(n, N_SEL))
    y = pl.pallas_call(
        functools.partial(_peer_out_kernel, tb=tb, final=final),
        grid=(n // tb,),
        in_specs=[smem_blk, smem_blk, table, rows, vec, const],
        out_specs=rows,
        out_shape=jax.ShapeDtypeStruct((n, d), F32),
        scratch_shapes=[pltpu.VMEM((tb * CHUNKS, LANES), F32)],
        compiler_params=_cparams("arbitrary"),
        name="peer_out",
    )(idx1, w.reshape(n * N_SEL), v, x2, g2, final_w.reshape(1, d))
    return y.reshape(b, t, d)


def _na_bias_table(rpb):
    cq = jnp.arange(GRID_W)
    coff = jnp.clip(cq[None, :] - cq[:, None] + (NA_WIN_C - 1), 0, 2 * NA_WIN_C - 2)
    c_start = jnp.clip(cq - NA_WIN_C // 2, 0, GRID_W - NA_WIN_C)
    ok = (cq[None, :] >= c_start[:, None]) & (cq[None, :] < c_start[:, None] + NA_WIN_C)
    roff = jnp.arange(NA_WIN_R)[:, None] + jnp.arange(NA_WIN_R)[None, :]
    bias = rpb.astype(F32)[:, roff][..., coff]
    bias = jnp.where(ok[None, None, None], bias, NEG_INF)
    bias = bias.transpose(0, 1, 3, 2, 4).reshape(HEAD_PAIRS, 2, NA_WIN_R, GRID_W, NA_WIN_R * GRID_W)
    return bias.transpose(0, 2, 1, 3, 4).reshape(HEAD_PAIRS, NA_WIN_R, 2 * GRID_W, NA_WIN_R * GRID_W)


def _rope_tables(s):
    t = jnp.arange(s)
    row = (t // GRID_W).astype(F32)
    col = (t % GRID_W).astype(F32)
    inv = ROPE_BASE ** (-jnp.arange(ROPE_PAIRS, dtype=F32) / ROPE_PAIRS)
    cr, sr = jnp.cos(row[:, None] * inv), jnp.sin(row[:, None] * inv)
    cc, sc = jnp.cos(col[:, None] * inv), jnp.sin(col[:, None] * inv)
    cos = jnp.concatenate([cr, cr, cc, cc], axis=-1)
    sin = jnp.concatenate([-sr, sr, -sc, sc], axis=-1)
    return jnp.tile(cos, (1, 2)), jnp.tile(sin, (1, 2))


def _block_diag(blocks):
    n, a, bb = blocks.shape
    eye = jnp.eye(n, dtype=blocks.dtype)
    return (eye[:, None, :, None] * blocks[:, :, None, :]).reshape(n * a, n * bb)


def kernel(x, c, ctx, c_ctx, norm1_w, norm2_w, w_ada, b_ada, w_in, w_out, na_rpb, ret_decay_fwd, ret_decay_bwd, ret_gn_w, pool_w, pool_scale, peer_wq, peer_keys, peer_u, peer_v, final_norm_w):
    b, s, d = x.shape
    depth = w_in.shape[0]
    clen = ctx.shape[1]
    cvec = jnp.concatenate([c, c_ctx[None, :], jnp.zeros((8 - b - 1, d), F32)], axis=0)
    mod = _modulation(cvec, w_ada, b_ada)
    cos_x, sin_x = _rope_tables(s)
    cos_c, sin_c = jnp.ones((clen, LANES), F32), jnp.zeros((clen, LANES), F32)
    avg = _block_diag(jnp.full((RET_HEADS, HEAD_DIM, HEAD_DIM), 1.0 / HEAD_DIM, F32))
    zero_state = jnp.zeros((b, HEAD_PAIRS, LANES, LANES), F32)

    for l in range(depth):
        last = l == depth - 1
        mx = mod[l, :b].reshape(b, 1, 6, d)
        mc = jnp.broadcast_to(mod[l, b].reshape(1, 1, 6, d), (b, 1, 6, d))
        sh1, sc1, g1, sh2, sc2, g2 = [mx[:, :, i] for i in range(6)]
        csh1, csc1, cg1, csh2, csc2, cg2 = [mc[:, :, i] for i in range(6)]
        lg = jnp.stack([jax.nn.log_sigmoid(ret_decay_fwd[l].astype(F32)),
                        jax.nn.log_sigmoid(ret_decay_bwd[l].astype(F32))], axis=0)
        wi = w_in[l].astype(BF16)
        wo = w_out[l].astype(BF16)
        wpool = _block_diag(pool_w[l]).astype(BF16)
        bias8 = _na_bias_table(na_rpb[l])

        pc, pcp = _in_proj(ctx, norm1_w[l], csh1, csc1, wi)
        ycf, ycb, r_f, r_b = _retention(pc, lg, cos_c, sin_c, zero_state, zero_state)

        px, pxp = _in_proj(x, norm1_w[l], sh1, sc1, wi)
        na = _na_attention(px, pc, bias8)
        yf, yb, _, _ = _retention(px, lg, cos_x, sin_x, r_f, r_b)
        pool = _pool(pxp, wpool, pool_scale[l])
        x = _out_proj(na, yf, yb, px, pool, x, g1, ret_gn_w[l], avg, wo)
        wq = jnp.stack(_split_bf16(peer_wq[l]))
        keys = jnp.stack(_split_bf16(peer_keys[l]))
        u_tab = _pack_table(peer_u[l])
        v_tab = _pack_table(peer_v[l])
        x = _peer_residual(x, norm2_w[l], sh2, sc2, g2, wq, keys, u_tab, v_tab, final_norm_w, last)

        if not last:
            na_c = _ctx_attention(pc)
            pool_c = _pool(pcp, wpool, pool_scale[l])
            ctx = _out_proj(na_c, ycf, ycb, pc, pool_c, ctx, cg1, ret_gn_w[l], avg, wo)
            ctx = _peer_residual(ctx, norm2_w[l], csh2, csc2, cg2, wq, keys, u_tab, v_tab, final_norm_w, False)
    return x
```

```python
import functools

import jax
import jax.numpy as jnp
from jax import lax
from jax.experimental import pallas as pl
from jax.experimental.pallas import tpu as pltpu

D_MODEL = 1024
GRID_W = 64
HEAD_DIM = 64
NA_HEADS = 6
NA_WIN_R = 8
NA_WIN_C = 16
RET_HEADS = 6
RET_CHUNK = 128
POOL_WINDOWS = (2, 4, 8, 16)
POOL_GROUP = 64
NA_WIDTH = NA_HEADS * HEAD_DIM
RET_WIDTH = RET_HEADS * HEAD_DIM
POOL_WIDTH = POOL_GROUP * len(POOL_WINDOWS)
O_RET_G = 3 * NA_WIDTH + 3 * RET_WIDTH
O_POOL = O_RET_G + RET_WIDTH
D_PROJ = O_POOL + POOL_WIDTH
ROPE_BASE = 10000.0
ROPE_PAIRS = HEAD_DIM // 4
PEER_HEADS = 8
PEER_NKEYS = 128
PEER_KEY_DIM = 128
PEER_TOPK = 16
N_SEL = PEER_HEADS * PEER_TOPK
NORM_EPS = 1e-6
NEG_INF = -1e30

LANES = 128
VMEM_LIMIT_BYTES = 56 * 1024 * 1024

F32 = jnp.float32
BF16 = jnp.bfloat16
HIGHEST = lax.Precision.HIGHEST
HEAD_PAIRS = NA_HEADS // 2
TOK_BLOCK = 128
CHUNKS = D_MODEL // LANES
TABLE_ROWS_PER_EXPERT = CHUNKS // 2


def _cparams(*sem):
    return pltpu.CompilerParams(dimension_semantics=sem, vmem_limit_bytes=VMEM_LIMIT_BYTES)


def _dot_nt(a, b, precision=None):
    return lax.dot_general(a, b, (((1,), (1,)), ((), ())), precision=precision,
                           preferred_element_type=F32)


def _split_bf16(a):
    hi = a.astype(BF16)
    return hi, (a - hi.astype(F32)).astype(BF16)


def _rms_mod(x, nw, shift, scale):
    y = x * lax.rsqrt(jnp.mean(x * x, axis=-1, keepdims=True) + NORM_EPS)
    return (y * nw) * (1.0 + scale) + shift


def _mod_kernel(c_ref, w_ref, b_ref, o_ref):
    c = c_ref[...]
    a = c * jax.nn.sigmoid(c)
    o_ref[0] = jnp.dot(a, w_ref[0], precision=HIGHEST, preferred_element_type=F32) + b_ref[0]


def _modulation(cvec, w_ada, b_ada):
    depth, d, n = w_ada.shape
    tn = 1536
    return pl.pallas_call(
        _mod_kernel,
        grid=(depth, n // tn),
        in_specs=[pl.BlockSpec((8, d), lambda l, j: (0, 0)),
                  pl.BlockSpec((1, d, tn), lambda l, j: (l, 0, j)),
                  pl.BlockSpec((1, 1, tn), lambda l, j: (l, 0, j))],
        out_specs=pl.BlockSpec((1, 8, tn), lambda l, j: (l, 0, j)),
        out_shape=jax.ShapeDtypeStruct((depth, 8, n), F32),
        compiler_params=_cparams("parallel", "parallel"),
        name="adaln_mod",
    )(cvec, w_ada, b_ada.reshape(depth, 1, n))


def _inproj_kernel(x_ref, nw_ref, sh_ref, sc_ref, w_ref, o_ref, p_ref):
    h = _rms_mod(x_ref[0], nw_ref[...], sh_ref[0], sc_ref[0])
    r = jnp.dot(h.astype(BF16), w_ref[...], preferred_element_type=F32)
    o_ref[0] = r[:, :O_POOL]
    p_ref[0] = r[:, O_POOL:]


def _in_proj(x, nw, shift, scale, w_bf16):
    b, t, d = x.shape
    tm = min(512, t)
    return pl.pallas_call(
        _inproj_kernel,
        grid=(b, t // tm),
        in_specs=[pl.BlockSpec((1, tm, d), lambda i, j: (i, j, 0)),
                  pl.BlockSpec((1, d), lambda i, j: (0, 0)),
                  pl.BlockSpec((1, 1, d), lambda i, j: (i, 0, 0)),
                  pl.BlockSpec((1, 1, d), lambda i, j: (i, 0, 0)),
                  pl.BlockSpec((d, D_PROJ), lambda i, j: (0, 0))],
        out_specs=[pl.BlockSpec((1, tm, O_POOL), lambda i, j: (i, j, 0)),
                   pl.BlockSpec((1, tm, POOL_WIDTH), lambda i, j: (i, j, 0))],
        out_shape=[jax.ShapeDtypeStruct((b, t, O_POOL), F32),
                   jax.ShapeDtypeStruct((b, t, POOL_WIDTH), F32)],
        compiler_params=_cparams("parallel", "parallel"),
        name="in_proj",
    )(x, nw.reshape(1, d), shift, scale, w_bf16)


def _softmax_pv(s_list, v_list):
    m = s_list[0].max(axis=-1, keepdims=True)
    for s in s_list[1:]:
        m = jnp.maximum(m, s.max(axis=-1, keepdims=True))
    num = None
    den = None
    for s, v in zip(s_list, v_list):
        p = jnp.exp(s - m)
        pv = jnp.dot(p.astype(BF16), v.astype(BF16), preferred_element_type=F32)
        ps = p.sum(axis=-1, keepdims=True)
        num = pv if num is None else num + pv
        den = ps if den is None else den + ps
    return num / den


NA_ROWS_PER_TRIP = 4


def _na_kernel(q_ref, k_ref, v_ref, kc_ref, vc_ref, bias_ref, o_ref, *, rows, rb):
    i = pl.program_id(2)
    lane = lax.broadcasted_iota(jnp.int32, (1, LANES), 1)
    first = lane < HEAD_DIM
    kc = kc_ref[0].astype(BF16)
    vc = vc_ref[0].astype(BF16)
    scale = HEAD_DIM ** -0.5
    nk = NA_WIN_R * GRID_W

    def body(it, carry):
        for u in range(NA_ROWS_PER_TRIP):
            one_row(it * NA_ROWS_PER_TRIP + u)
        return carry

    def one_row(rr):
        r = i * rb + rr
        rs = jnp.clip(r - NA_WIN_R // 2, 0, rows - NA_WIN_R)
        delta = rs - r + (NA_WIN_R - 1)
        q = q_ref[0, pl.ds(pl.multiple_of(rr * GRID_W, GRID_W), GRID_W), :]
        k = k_ref[0, pl.ds(pl.multiple_of(rs * GRID_W, GRID_W), nk), :].astype(BF16)
        v = v_ref[0, pl.ds(pl.multiple_of(rs * GRID_W, GRID_W), nk), :].astype(BF16)
        q2 = jnp.concatenate([jnp.where(first, q, 0.0), jnp.where(first, 0.0, q)], axis=0).astype(BF16)
        s = _dot_nt(q2, k) * scale + bias_ref[0, delta]
        sc = _dot_nt(q2, kc) * scale
        o2 = _softmax_pv([s, sc], [v, vc])
        o_ref[0, pl.ds(pl.multiple_of(rr * GRID_W, GRID_W), GRID_W), :] = jnp.where(
            first, o2[0:GRID_W], o2[GRID_W:2 * GRID_W])

    lax.fori_loop(0, rb // NA_ROWS_PER_TRIP, body, 0)


def _na_attention(px, pc, bias8):
    b, s, _ = px.shape
    c = pc.shape[1]
    rows = s // GRID_W
    assert rows >= NA_WIN_R and s % GRID_W == 0
    rb = min(8, rows)
    nq, nkb, nvb = 0, HEAD_PAIRS, 2 * HEAD_PAIRS
    return pl.pallas_call(
        functools.partial(_na_kernel, rows=rows, rb=rb),
        grid=(b, HEAD_PAIRS, rows // rb),
        in_specs=[pl.BlockSpec((1, rb * GRID_W, LANES), lambda bi, hp, i: (bi, i, nq + hp)),
                  pl.BlockSpec((1, s, LANES), lambda bi, hp, i: (bi, 0, nkb + hp)),
                  pl.BlockSpec((1, s, LANES), lambda bi, hp, i: (bi, 0, nvb + hp)),
                  pl.BlockSpec((1, c, LANES), lambda bi, hp, i: (bi, 0, nkb + hp)),
                  pl.BlockSpec((1, c, LANES), lambda bi, hp, i: (bi, 0, nvb + hp)),
                  pl.BlockSpec((1, NA_WIN_R, 2 * GRID_W, NA_WIN_R * GRID_W),
                               lambda bi, hp, i: (hp, 0, 0, 0))],
        out_specs=pl.BlockSpec((1, rb * GRID_W, LANES), lambda bi, hp, i: (bi, i, hp)),
        out_shape=jax.ShapeDtypeStruct((b, s, NA_WIDTH), F32),
        compiler_params=_cparams("parallel", "parallel", "arbitrary"),
        name="na_attention",
    )(px, px, px, pc, pc, bias8)


def _ctx_attn_kernel(q_ref, k_ref, v_ref, o_ref):
    lane = lax.broadcasted_iota(jnp.int32, (1, LANES), 1)
    first = lane < HEAD_DIM
    q = q_ref[0]
    k = k_ref[0]
    v = v_ref[0]
    outs = []
    for h in range(2):
        qm = jnp.where(first if h == 0 else jnp.logical_not(first), q, 0.0)
        s = _dot_nt(qm, k) * HEAD_DIM ** -0.5
        outs.append(_softmax_pv([s], [v]))
    o_ref[0] = jnp.where(first, outs[0], outs[1])


def _ctx_attention(pc):
    b, c, _ = pc.shape
    return pl.pallas_call(
        _ctx_attn_kernel,
        grid=(b, HEAD_PAIRS),
        in_specs=[pl.BlockSpec((1, c, LANES), lambda bi, hp: (bi, 0, hp)),
                  pl.BlockSpec((1, c, LANES), lambda bi, hp: (bi, 0, HEAD_PAIRS + hp)),
                  pl.BlockSpec((1, c, LANES), lambda bi, hp: (bi, 0, 2 * HEAD_PAIRS + hp))],
        out_specs=pl.BlockSpec((1, c, LANES), lambda bi, hp: (bi, 0, hp)),
        out_shape=jax.ShapeDtypeStruct((b, c, NA_WIDTH), F32),
        compiler_params=_cparams("parallel", "parallel"),
        name="ctx_attention",
    )(pc, pc, pc)


RET_CHUNKS_PER_STEP = 4


def _ret_kernel(lg_ref, qf_ref, kf_ref, vf_ref, qb_ref, kb_ref, vb_ref,
                cf_ref, sf_ref, cb_ref, sb_ref, rf0_ref, rb0_ref,
                yf_ref, yb_ref, rfo_ref, rbo_ref, rf_scr, rb_scr, *, cpb):
    hp = pl.program_id(1)
    c = pl.program_id(2)
    nc = pl.num_programs(2)
    cs = RET_CHUNK

    @pl.when(c == 0)
    def _():
        rf_scr[...] = rf0_ref[0, 0]
        rb_scr[...] = rb0_ref[0, 0]

    lane = lax.broadcasted_iota(jnp.int32, (1, LANES), 1)
    first = lane < HEAD_DIM
    low = (lane % (2 * ROPE_PAIRS)) < ROPE_PAIRS
    pos = lax.broadcasted_iota(jnp.int32, (cs, 1), 0).astype(F32)
    ii = lax.broadcasted_iota(jnp.int32, (cs, cs), 0)
    jj = lax.broadcasted_iota(jnp.int32, (cs, cs), 1)
    diff = (ii - jj).astype(F32)
    same_head = (ii < HEAD_DIM) == (jj < HEAD_DIM)
    scale = HEAD_DIM ** -0.5

    def rope(x, cos, sin):
        swapped = jnp.where(low, pltpu.roll(x, LANES - ROPE_PAIRS, 1), pltpu.roll(x, ROPE_PAIRS, 1))
        return x * cos + swapped * sin

    def direction(d, q_ref, k_ref, v_ref, cos_ref, sin_ref, r_scr, y_ref):
        lg0 = lg_ref[d, 2 * hp]
        lg1 = lg_ref[d, 2 * hp + 1]
        lgv = jnp.where(first, lg0, lg1)
        decs = []
        for lg in (lg0, lg1):
            if d == 0:
                decs.append(jnp.where(diff >= 0, jnp.exp(jnp.maximum(diff, 0.0) * lg), 0.0))
            else:
                decs.append(jnp.where(diff <= 0, jnp.exp(jnp.maximum(-diff, 0.0) * lg), 0.0))
        dec2 = jnp.concatenate(decs, axis=0)
        if d == 0:
            xi = jnp.exp((pos + 1.0) * lgv)
            zeta = jnp.exp((cs - 1.0 - pos) * lgv)
        else:
            xi = jnp.exp((cs - pos) * lgv)
            zeta = jnp.exp(pos * lgv)
        chunk_decay = jnp.exp(cs * lgv)
        r = r_scr[...]
        for j in (range(cpb) if d == 0 else reversed(range(cpb))):
            rows = pl.ds(j * cs, cs)
            cos = cos_ref[rows, :]
            sin = sin_ref[rows, :]
            q = rope(q_ref[0, rows, :], cos, sin)
            k = rope(k_ref[0, rows, :], cos, sin) * scale
            v = v_ref[0, rows, :].astype(BF16)
            q2 = jnp.concatenate([jnp.where(first, q, 0.0), jnp.where(first, 0.0, q)], axis=0).astype(BF16)
            s2 = _dot_nt(q2, k.astype(BF16)) * dec2
            o2 = jnp.dot(s2.astype(BF16), v, preferred_element_type=F32)
            inner = jnp.where(first, o2[0:cs], o2[cs:2 * cs])
            y_ref[0, rows, :] = inner + jnp.dot((q * xi).astype(BF16), r.astype(BF16),
                                                preferred_element_type=F32)
            kv = lax.dot_general((k * zeta).astype(BF16), v, (((0,), (0,)), ((), ())),
                                 preferred_element_type=F32)
            r = chunk_decay * r + jnp.where(same_head, kv, 0.0)
        r_scr[...] = r

    direction(0, qf_ref, kf_ref, vf_ref, cf_ref, sf_ref, rf_scr, yf_ref)
    direction(1, qb_ref, kb_ref, vb_ref, cb_ref, sb_ref, rb_scr, yb_ref)

    @pl.when(c == nc - 1)
    def _():
        rfo_ref[0, 0] = rf_scr[...]
        rbo_ref[0, 0] = rb_scr[...]


def _retention(p, lg, cos_t, sin_t, rf0, rb0):
    b, t, _ = p.shape
    cpb = min(RET_CHUNKS_PER_STEP, t // RET_CHUNK)
    nc = t // (RET_CHUNK * cpb)
    qo, ko, vo = 3 * HEAD_PAIRS, 4 * HEAD_PAIRS, 5 * HEAD_PAIRS
    blk = (1, RET_CHUNK * cpb, LANES)
    fwd = lambda o: pl.BlockSpec(blk, lambda bi, hp, c: (bi, c, o + hp))
    bwd = lambda o: pl.BlockSpec(blk, lambda bi, hp, c: (bi, nc - 1 - c, o + hp))
    tab_f = pl.BlockSpec((RET_CHUNK * cpb, LANES), lambda bi, hp, c: (c, 0))
    tab_b = pl.BlockSpec((RET_CHUNK * cpb, LANES), lambda bi, hp, c: (nc - 1 - c, 0))
    st = pl.BlockSpec((1, 1, LANES, LANES), lambda bi, hp, c: (bi, hp, 0, 0))
    return pl.pallas_call(
        functools.partial(_ret_kernel, cpb=cpb),
        grid=(b, HEAD_PAIRS, nc),
        in_specs=[pl.BlockSpec(memory_space=pltpu.SMEM),
                  fwd(qo), fwd(ko), fwd(vo), bwd(qo), bwd(ko), bwd(vo),
                  tab_f, tab_f, tab_b, tab_b, st, st],
        out_specs=[pl.BlockSpec(blk, lambda bi, hp, c: (bi, c, hp)),
                   pl.BlockSpec(blk, lambda bi, hp, c: (bi, nc - 1 - c, hp)),
                   st, st],
        out_shape=[jax.ShapeDtypeStruct((b, t, RET_WIDTH), F32),
                   jax.ShapeDtypeStruct((b, t, RET_WIDTH), F32),
                   jax.ShapeDtypeStruct((b, HEAD_PAIRS, LANES, LANES), F32),
                   jax.ShapeDtypeStruct((b, HEAD_PAIRS, LANES, LANES), F32)],
        scratch_shapes=[pltpu.VMEM((LANES, LANES), F32), pltpu.VMEM((LANES, LANES), F32)],
        compiler_params=_cparams("parallel", "parallel", "arbitrary"),
        name="retention",
    )(lg, p, p, p, p, p, p, cos_t, sin_t, cos_t, sin_t, rf0, rb0)


def _pool_kernel(prev_ref, cur_ref, next_ref, w_ref, s_ref, o_ref, scr, *, t_total, tp):
    i = pl.program_id(1)
    n = pl.num_programs(1)
    halo = POOL_WINDOWS[-1] // 2
    x = cur_ref[0]
    scr[0:halo, :] = jnp.where(i > 0, prev_ref[0], 0.0)
    scr[halo:halo + tp, :] = x
    scr[halo + tp:2 * halo + tp, :] = jnp.where(i < n - 1, next_ref[0], 0.0)
    t = i * tp + lax.broadcasted_iota(jnp.int32, (tp, 1), 0)
    lane = lax.broadcasted_iota(jnp.int32, (1, POOL_WIDTH), 1)

    def shifted(s):
        return scr[halo + s:halo + s + tp, :]

    acc = None
    mean = None
    done = 0
    for g, w in enumerate(POOL_WINDOWS):
        half = w // 2
        for s in list(range(-half, -done)) + list(range(done, half)):
            sh = x if s == 0 else shifted(s)
            acc = sh if acc is None else acc + sh
        done = half
        cnt = (jnp.minimum(t + half, t_total) - jnp.maximum(t - half, 0)).astype(F32)
        mg = acc / cnt
        mean = mg if mean is None else jnp.where(lane >= g * POOL_GROUP, mg, mean)
    dlt = (mean - x).astype(BF16)
    o_ref[0] = jnp.dot(dlt, w_ref[...], preferred_element_type=F32) * s_ref[...]


def _pool(pin, w_bd_bf16, scale):
    b, t, _ = pin.shape
    tp = min(1024, t)
    halo = POOL_WINDOWS[-1] // 2
    nh = tp // halo
    last = t // halo - 1
    return pl.pallas_call(
        functools.partial(_pool_kernel, t_total=t, tp=tp),
        grid=(b, t // tp),
        in_specs=[pl.BlockSpec((1, halo, POOL_WIDTH), lambda bi, i: (bi, jnp.maximum(i * nh - 1, 0), 0)),
                  pl.BlockSpec((1, tp, POOL_WIDTH), lambda bi, i: (bi, i, 0)),
                  pl.BlockSpec((1, halo, POOL_WIDTH), lambda bi, i: (bi, jnp.minimum((i + 1) * nh, last), 0)),
                  pl.BlockSpec((POOL_WIDTH, POOL_WIDTH), lambda bi, i: (0, 0)),
                  pl.BlockSpec((1, POOL_WIDTH), lambda bi, i: (0, 0))],
        out_specs=pl.BlockSpec((1, tp, POOL_WIDTH), lambda bi, i: (bi, i, 0)),
        out_shape=jax.ShapeDtypeStruct((b, t, POOL_WIDTH), F32),
        scratch_shapes=[pltpu.VMEM((tp + 2 * halo, POOL_WIDTH), F32)],
        compiler_params=_cparams("parallel", "parallel"),
        name="multiscale_pool",
    )(pin, pin, pin, w_bd_bf16, scale.reshape(1, POOL_WIDTH))


def _out_kernel(na_ref, yf_ref, yb_ref, g_ref, pool_ref, x_ref, g1_ref, gnw_ref, avg_ref, wo_ref, o_ref):
    y = yf_ref[0] + yb_ref[0]
    avg = avg_ref[...]
    mu = jnp.dot(y, avg, precision=HIGHEST, preferred_element_type=F32)
    d = y - mu
    var = jnp.dot(d * d, avg, precision=HIGHEST, preferred_element_type=F32)
    yn = d * lax.rsqrt(var + NORM_EPS) * gnw_ref[...]
    g = g_ref[0]
    ret = yn * (g * jax.nn.sigmoid(g))
    mix = jnp.dot(na_ref[0].astype(BF16), wo_ref[0:NA_WIDTH, :], preferred_element_type=F32)
    mix += jnp.dot(ret.astype(BF16), wo_ref[NA_WIDTH:NA_WIDTH + RET_WIDTH, :], preferred_element_type=F32)
    mix += jnp.dot(pool_ref[0].astype(BF16), wo_ref[NA_WIDTH + RET_WIDTH:, :], preferred_element_type=F32)
    o_ref[0] = x_ref[0] + g1_ref[0] * mix


def _out_proj(na, yf, yb, p, pool, x, g1, gn_w, avg, wo_bf16):
    b, t, d = x.shape
    tm = min(512, t)
    row = lambda w: pl.BlockSpec((1, tm, w), lambda i, j: (i, j, 0))
    return pl.pallas_call(
        _out_kernel,
        grid=(b, t // tm),
        in_specs=[row(NA_WIDTH), row(RET_WIDTH), row(RET_WIDTH),
                  pl.BlockSpec((1, tm, RET_WIDTH), lambda i, j: (i, j, O_RET_G // RET_WIDTH)),
                  row(POOL_WIDTH), row(d),
                  pl.BlockSpec((1, 1, d), lambda i, j: (i, 0, 0)),
                  pl.BlockSpec((1, RET_WIDTH), lambda i, j: (0, 0)),
                  pl.BlockSpec((RET_WIDTH, RET_WIDTH), lambda i, j: (0, 0)),
                  pl.BlockSpec((d, d), lambda i, j: (0, 0))],
        out_specs=row(d),
        out_shape=jax.ShapeDtypeStruct((b, t, d), F32),
        compiler_params=_cparams("parallel", "parallel"),
        name="out_proj",
    )(na, yf, yb, p, pool, x, g1, gn_w.reshape(1, RET_WIDTH), avg, wo_bf16)


def _topk_rows(s, order=None, payload=None):
    n, m = s.shape
    if order is None:
        order = lax.broadcasted_iota(jnp.int32, (n, m), 0)
    krow = lax.broadcasted_iota(jnp.int32, (PEER_TOPK, m), 0)
    vals = jnp.zeros((PEER_TOPK, m), F32)
    idxs = jnp.zeros((PEER_TOPK, m), jnp.int32)
    for k in range(PEER_TOPK):
        mx = jnp.max(s, axis=0, keepdims=True)
        am = jnp.min(jnp.where(s == mx, order, jnp.iinfo(jnp.int32).max), axis=0, keepdims=True)
        sel = order == am
        pick = am if payload is None else jnp.sum(jnp.where(sel, payload, 0), axis=0, keepdims=True)
        vals = jnp.where(krow == k, mx, vals)
        idxs = jnp.where(krow == k, pick, idxs)
        s = jnp.where(sel, -jnp.inf, s)
    return vals, idxs


def _product_candidates(va, ia, vb, ib):
    k = PEER_TOPK
    sub = 8
    m = va.shape[1]
    row = lax.broadcasted_iota(jnp.int32, (sub, m), 0)
    cand, flat, eid = [], [], []

    def add(v, f, e, nvalid):
        cand.append(v if nvalid >= sub else jnp.where(row < nvalid, v, -jnp.inf))
        flat.append(f)
        eid.append(e)

    for i in range(sub):
        add(va[i:i + 1, :] + vb[0:sub, :], i * k + row, ia[i:i + 1, :] * PEER_NKEYS + ib[0:sub, :], k // (i + 1))
    add(va[0:1, :] + vb[sub:k, :], sub + row, ia[0:1, :] * PEER_NKEYS + ib[sub:k, :], sub)
    add(va[sub:k, :] + vb[0:1, :], (sub + row) * k, ia[sub:k, :] * PEER_NKEYS + ib[0:1, :], sub)
    return jnp.concatenate(cand, axis=0), jnp.concatenate(flat, axis=0), jnp.concatenate(eid, axis=0)


def _route_kernel(x_ref, nw_ref, sh_ref, sc_ref, wq_ref, keys_ref, idx_ref, gate_ref, q_scr, g_scr, i_scr, *, tm):
    h_hi, h_lo = _split_bf16(_rms_mod(x_ref[0], nw_ref[...], sh_ref[0], sc_ref[0]))
    q = (jnp.dot(h_hi, wq_ref[0], preferred_element_type=F32)
         + jnp.dot(h_hi, wq_ref[1], preferred_element_type=F32)
         + jnp.dot(h_lo, wq_ref[0], preferred_element_type=F32))
    for j in range(2 * PEER_HEADS):
        q_scr[j] = q[:, j * PEER_KEY_DIM:(j + 1) * PEER_KEY_DIM]

    def scores(half, hh):
        q_hi, q_lo = _split_bf16(q_scr[2 * hh + half])
        k_hi = keys_ref[0, half, hh]
        return _dot_nt(k_hi, q_hi) + _dot_nt(k_hi, q_lo) + _dot_nt(keys_ref[1, half, hh], q_hi)

    def head(hh, carry):
        sa = scores(0, hh)
        sb = scores(1, hh)
        va, ia = _topk_rows(sa)
        vb, ib = _topk_rows(sb)
        cand, flat, eid = _product_candidates(va, ia, vb, ib)
        sc, ei = _topk_rows(cand, order=flat, payload=eid)
        e = jnp.exp(sc - sc[0:1, :])
        off = pl.multiple_of(hh * PEER_TOPK, PEER_TOPK)
        g_scr[pl.ds(off, PEER_TOPK), :] = e / jnp.sum(e, axis=0, keepdims=True)
        i_scr[pl.ds(off, PEER_TOPK), :] = ei
        return carry

    lax.fori_loop(0, PEER_HEADS, head, 0)
    gate_ref[0] = g_scr[...].T
    idx_ref[0] = i_scr[...].T * TABLE_ROWS_PER_EXPERT


def _route(x, nw, shift, scale, wq, keys):
    b, t, d = x.shape
    tm = min(512, t)
    nt = t // tm
    nq = wq.shape[2]
    return pl.pallas_call(
        functools.partial(_route_kernel, tm=tm),
        grid=(b, nt),
        in_specs=[pl.BlockSpec((1, tm, d), lambda i, j: (i, j, 0)),
                  pl.BlockSpec((1, d), lambda i, j: (0, 0)),
                  pl.BlockSpec((1, 1, d), lambda i, j: (i, 0, 0)),
                  pl.BlockSpec((1, 1, d), lambda i, j: (i, 0, 0)),
                  pl.BlockSpec((2, d, nq), lambda i, j: (0, 0, 0)),
                  pl.BlockSpec((2, 2, PEER_HEADS, PEER_NKEYS, PEER_KEY_DIM), lambda i, j: (0, 0, 0, 0, 0))],
        out_specs=[pl.BlockSpec((1, tm, N_SEL), lambda i, j: (i, j, 0)),
                   pl.BlockSpec((1, tm, N_SEL), lambda i, j: (i, j, 0))],
        out_shape=[jax.ShapeDtypeStruct((b, t, N_SEL), jnp.int32),
                   jax.ShapeDtypeStruct((b, t, N_SEL), F32)],
        scratch_shapes=[pltpu.VMEM((2 * PEER_HEADS, tm, PEER_KEY_DIM), F32),
                        pltpu.VMEM((N_SEL, tm), F32),
                        pltpu.VMEM((N_SEL, tm), jnp.int32)],
        compiler_params=_cparams("parallel", "parallel"),
        name="peer_route",
    )(x, nw.reshape(1, d), shift, scale, wq, keys)


def _pack_kernel(t_ref, o_ref, *, te):
    x = t_ref[...]
    d = x.shape[1]

    def bf16_bits(a):
        return lax.bitcast_convert_type(a.astype(BF16).astype(F32), jnp.uint32)

    words = (bf16_bits(x[:, :d // 2]) >> 16) | bf16_bits(x[:, d // 2:])
    for q in range(TABLE_ROWS_PER_EXPERT):
        o_ref[pl.ds(q, te, stride=TABLE_ROWS_PER_EXPERT), :] = words[:, q * LANES:(q + 1) * LANES]


def _pack_table(tab):
    n, d = tab.shape
    te = 512
    return pl.pallas_call(
        functools.partial(_pack_kernel, te=te),
        grid=(n // te,),
        in_specs=[pl.BlockSpec((te, d), lambda i: (i, 0))],
        out_specs=pl.BlockSpec((te * TABLE_ROWS_PER_EXPERT, LANES), lambda i: (i, 0)),
        out_shape=jax.ShapeDtypeStruct((n * TABLE_ROWS_PER_EXPERT, LANES), jnp.uint32),
        compiler_params=_cparams("parallel"),
        name="pack_table",
    )(tab)


def _token_loop(tb, token, per_trip):
    def trip(i, carry):
        for j in range(per_trip):
            token(i * per_trip + j, j)
        return carry

    lax.fori_loop(0, tb // per_trip, trip, 0)


def _unpack(slab):
    lo = lax.bitcast_convert_type(slab << 16, F32)
    hi = lax.bitcast_convert_type(slab & jnp.uint32(0xFFFF0000), F32)
    return lo, hi


def _peer_act_kernel(idx_ref, tab_ref, x_ref, nw_ref, sh_ref, sc_ref, gate_ref, w_ref, h_scr, act_scr, *, tb):
    h = _rms_mod(x_ref[...], nw_ref[...], sh_ref[0], sc_ref[0])
    half = CHUNKS // 2
    for c in range(CHUNKS):
        piece = h[:, c * LANES:(c + 1) * LANES]
        base = (c // half) * CHUNKS + c % half
        h_scr[pl.ds(base, tb, stride=2 * CHUNKS), :] = piece
        h_scr[pl.ds(base + half, tb, stride=2 * CHUNKS), :] = piece
    sub = lax.broadcasted_iota(jnp.int32, (CHUNKS, N_SEL), 0)
    lane = lax.broadcasted_iota(jnp.int32, (CHUNKS, N_SEL), 1)
    slot = lane - PAIR_STRIDE * (sub >= half).astype(jnp.int32)

    def token(t, _):
        off = pl.multiple_of(t * 2 * CHUNKS, 2 * CHUNKS)
        hlo = h_scr[pl.ds(off, CHUNKS), :]
        hhi = h_scr[pl.ds(off + CHUNKS, CHUNKS), :]
        acc = jnp.zeros((CHUNKS, N_SEL), F32)
        for a, slab in _slab_pairs(tab_ref, idx_ref, t):
            lo, hi = _unpack(slab)
            dot = jnp.sum(lo * hlo + hi * hhi, axis=-1, keepdims=True)
            acc = jnp.where(slot == a, dot, acc)
        act_scr[pl.ds(t, 1), :] = jnp.sum(acc, axis=0, keepdims=True)

    _token_loop(tb, token, 8)
    a = act_scr[...]
    w_ref[...] = gate_ref[...] * (0.5 * a * (1.0 + lax.erf(a * (2.0 ** -0.5))))


PAIR_STRIDE = N_SEL // 4


def _slab_pairs(tab_ref, idx_ref, t):
    views = [idx_ref.at[pl.ds(t * N_SEL + q * PAIR_STRIDE, PAIR_STRIDE)] for q in range(4)]
    for k in range(PAIR_STRIDE):
        rows = [pl.multiple_of(v[k], TABLE_ROWS_PER_EXPERT) for v in views]
        for q in (0, 2):
            yield q * PAIR_STRIDE + k, jnp.concatenate(
                [tab_ref[pl.ds(rows[q], TABLE_ROWS_PER_EXPERT), :],
                 tab_ref[pl.ds(rows[q + 1], TABLE_ROWS_PER_EXPERT), :]], axis=0)


def _peer_out_kernel(idx_ref, w_ref, tab_ref, x_ref, g2_ref, fw_ref, o_ref, p_scr, *, tb, final):
    half = CHUNKS // 2

    def token(t, _):
        accs = [jnp.zeros((half, LANES), F32) for _ in range(4)]
        for s in range(N_SEL):
            row = pl.multiple_of(idx_ref[t * N_SEL + s], TABLE_ROWS_PER_EXPERT)
            w = w_ref[t * N_SEL + s]
            lo, hi = _unpack(tab_ref[pl.ds(row, TABLE_ROWS_PER_EXPERT), :])
            j = 2 * (s % 2)
            accs[j] = accs[j] + w * lo
            accs[j + 1] = accs[j + 1] + w * hi
        off = pl.multiple_of(t * CHUNKS, CHUNKS)
        p_scr[pl.ds(off, half), :] = accs[0] + accs[2]
        p_scr[pl.ds(pl.multiple_of(off + half, half), half), :] = accs[1] + accs[3]

    _token_loop(tb, token, 1)
    peer = jnp.concatenate([p_scr[pl.ds(c, tb, stride=CHUNKS), :] for c in range(CHUNKS)], axis=1)
    y = x_ref[...] + g2_ref[0] * peer
    if final:
        y = y * lax.rsqrt(jnp.mean(y * y, axis=-1, keepdims=True) + NORM_EPS) * fw_ref[...]
    o_ref[...] = y


def _peer_residual(x, nw, shift, scale, g2, wq, keys, u, v, final_w, final):
    b, t, d = x.shape
    assert d == CHUNKS * LANES
    idx, gate = _route(x, nw, shift, scale, wq, keys)
    n = b * t
    tb = TOK_BLOCK
    per_batch = t // tb
    x2 = x.reshape(n, d)
    idx1 = idx.reshape(n * N_SEL)
    smem_blk = pl.BlockSpec((tb * N_SEL,), lambda i: (i,), memory_space=pltpu.SMEM)
    table = pl.BlockSpec(u.shape, lambda i: (0, 0), pipeline_mode=pl.Buffered(1))
    rows = pl.BlockSpec((tb, d), lambda i: (i, 0))
    sel = pl.BlockSpec((tb, N_SEL), lambda i: (i, 0))
    vec = pl.BlockSpec((1, 1, d), lambda i: (i // per_batch, 0, 0))
    const = pl.BlockSpec((1, d), lambda i: (0, 0))
    w = pl.pallas_call(
        functools.partial(_peer_act_kernel, tb=tb),
        grid=(n // tb,),
        in_specs=[smem_blk, table, rows, const, vec, vec, sel],
        out_specs=sel,
        out_shape=jax.ShapeDtypeStruct((n, N_SEL), F32),
        scratch_shapes=[pltpu.VMEM((tb * 2 * CHUNKS, LANES), F32), pltpu.VMEM((tb, N_SEL), F32)],
        compiler_params=_cparams("arbitrary"),
        name="peer_act",
    )(idx1, u, x2, nw.reshape(1, d), shift, scale, gate.reshape(n, N_SEL))
    y = pl.pallas_call(
        functools.partial(_peer_out_kernel, tb=tb, final=final),
        grid=(n // tb,),
        in_specs=[smem_blk, smem_blk, table, rows, vec, const],
        out_specs=rows,
        out_shape=jax.ShapeDtypeStruct((n, d), F32),
        scratch_shapes=[pltpu.VMEM((tb * CHUNKS, LANES), F32)],
        compiler_params=_cparams("arbitrary"),
        name="peer_out",
    )(idx1, w.reshape(n * N_SEL), v, x2, g2, final_w.reshape(1, d))
    return y.reshape(b, t, d)


def _na_bias_table(rpb):
    cq = jnp.arange(GRID_W)
    coff = jnp.clip(cq[None, :] - cq[:, None] + (NA_WIN_C - 1), 0, 2 * NA_WIN_C - 2)
    c_start = jnp.clip(cq - NA_WIN_C // 2, 0, GRID_W - NA_WIN_C)
    ok = (cq[None, :] >= c_start[:, None]) & (cq[None, :] < c_start[:, None] + NA_WIN_C)
    roff = jnp.arange(NA_WIN_R)[:, None] + jnp.arange(NA_WIN_R)[None, :]
    bias = rpb.astype(F32)[:, roff][..., coff]
    bias = jnp.where(ok[None, None, None], bias, NEG_INF)
    bias = bias.transpose(0, 1, 3, 2, 4).reshape(HEAD_PAIRS, 2, NA_WIN_R, GRID_W, NA_WIN_R * GRID_W)
    return bias.transpose(0, 2, 1, 3, 4).reshape(HEAD_PAIRS, NA_WIN_R, 2 * GRID_W, NA_WIN_R * GRID_W)


def _rope_tables(s):
    t = jnp.arange(s)
    row = (t // GRID_W).astype(F32)
    col = (t % GRID_W).astype(F32)
    inv = ROPE_BASE ** (-jnp.arange(ROPE_PAIRS, dtype=F32) / ROPE_PAIRS)
    cr, sr = jnp.cos(row[:, None] * inv), jnp.sin(row[:, None] * inv)
    cc, sc = jnp.cos(col[:, None] * inv), jnp.sin(col[:, None] * inv)
    cos = jnp.concatenate([cr, cr, cc, cc], axis=-1)
    sin = jnp.concatenate([-sr, sr, -sc, sc], axis=-1)
    return jnp.tile(cos, (1, 2)), jnp.tile(sin, (1, 2))


def _block_diag(blocks):
    n, a, bb = blocks.shape
    eye = jnp.eye(n, dtype=blocks.dtype)
    return (eye[:, None, :, None] * blocks[:, :, None, :]).reshape(n * a, n * bb)


def kernel(x, c, ctx, c_ctx, norm1_w, norm2_w, w_ada, b_ada, w_in, w_out, na_rpb, ret_decay_fwd, ret_decay_bwd, ret_gn_w, pool_w, pool_scale, peer_wq, peer_keys, peer_u, peer_v, final_norm_w):
    b, s, d = x.shape
    depth = w_in.shape[0]
    clen = ctx.shape[1]
    cvec = jnp.concatenate([c, c_ctx[None, :], jnp.zeros((8 - b - 1, d), F32)], axis=0)
    mod = _modulation(cvec, w_ada, b_ada)
    cos_x, sin_x = _rope_tables(s)
    cos_c, sin_c = jnp.ones((clen, LANES), F32), jnp.zeros((clen, LANES), F32)
    avg = _block_diag(jnp.full((RET_HEADS, HEAD_DIM, HEAD_DIM), 1.0 / HEAD_DIM, F32))
    zero_state = jnp.zeros((b, HEAD_PAIRS, LANES, LANES), F32)

    for l in range(depth):
        last = l == depth - 1
        mx = mod[l, :b].reshape(b, 1, 6, d)
        mc = jnp.broadcast_to(mod[l, b].reshape(1, 1, 6, d), (b, 1, 6, d))
        sh1, sc1, g1, sh2, sc2, g2 = [mx[:, :, i] for i in range(6)]
        csh1, csc1, cg1, csh2, csc2, cg2 = [mc[:, :, i] for i in range(6)]
        lg = jnp.stack([jax.nn.log_sigmoid(ret_decay_fwd[l].astype(F32)),
                        jax.nn.log_sigmoid(ret_decay_bwd[l].astype(F32))], axis=0)
        wi = w_in[l].astype(BF16)
        wo = w_out[l].astype(BF16)
        wpool = _block_diag(pool_w[l]).astype(BF16)
        bias8 = _na_bias_table(na_rpb[l])

        pc, pcp = _in_proj(ctx, norm1_w[l], csh1, csc1, wi)
        ycf, ycb, r_f, r_b = _retention(pc, lg, cos_c, sin_c, zero_state, zero_state)

        px, pxp = _in_proj(x, norm1_w[l], sh1, sc1, wi)
        na = _na_attention(px, pc, bias8)
        yf, yb, _, _ = _retention(px, lg, cos_x, sin_x, r_f, r_b)
        pool = _pool(pxp, wpool, pool_scale[l])
        x = _out_proj(na, yf, yb, px, pool, x, g1, ret_gn_w[l], avg, wo)
        wq = jnp.stack(_split_bf16(peer_wq[l]))
        keys = jnp.stack(_split_bf16(peer_keys[l]))
        u_tab = _pack_table(peer_u[l])
        v_tab = _pack_table(peer_v[l])
        x = _peer_residual(x, norm2_w[l], sh2, sc2, g2, wq, keys, u_tab, v_tab, final_norm_w, last)

        if not last:
            na_c = _ctx_attention(pc)
            pool_c = _pool(pcp, wpool, pool_scale[l])
            ctx = _out_proj(na_c, ycf, ycb, pc, pool_c, ctx, cg1, ret_gn_w[l], avg, wo)
            ctx = _peer_residual(ctx, norm2_w[l], csh2, csc2, cg2, wq, keys, u_tab, v_tab, final_norm_w, False)
    return x
```

```python
import functools

import jax
import jax.numpy as jnp
from jax import lax
from jax.experimental import pallas as pl
from jax.experimental.pallas import tpu as pltpu

D_MODEL = 1024
GRID_W = 64
HEAD_DIM = 64
NA_HEADS = 6
NA_WIN_R = 8
NA_WIN_C = 16
RET_HEADS = 6
RET_CHUNK = 128
POOL_WINDOWS = (2, 4, 8, 16)
POOL_GROUP = 64
NA_WIDTH = NA_HEADS * HEAD_DIM
RET_WIDTH = RET_HEADS * HEAD_DIM
POOL_WIDTH = POOL_GROUP * len(POOL_WINDOWS)
O_RET_G = 3 * NA_WIDTH + 3 * RET_WIDTH
O_POOL = O_RET_G + RET_WIDTH
D_PROJ = O_POOL + POOL_WIDTH
ROPE_BASE = 10000.0
ROPE_PAIRS = HEAD_DIM // 4
PEER_HEADS = 8
PEER_NKEYS = 128
PEER_KEY_DIM = 128
PEER_TOPK = 16
N_SEL = PEER_HEADS * PEER_TOPK
NORM_EPS = 1e-6
NEG_INF = -1e30

LANES = 128
VMEM_LIMIT_BYTES = 56 * 1024 * 1024

F32 = jnp.float32
BF16 = jnp.bfloat16
HIGHEST = lax.Precision.HIGHEST
HEAD_PAIRS = NA_HEADS // 2
TOK_BLOCK = 128
CHUNKS = D_MODEL // LANES
TABLE_ROWS_PER_EXPERT = CHUNKS // 2


def _cparams(*sem):
    return pltpu.CompilerParams(dimension_semantics=sem, vmem_limit_bytes=VMEM_LIMIT_BYTES)


def _dot_nt(a, b, precision=None):
    return lax.dot_general(a, b, (((1,), (1,)), ((), ())), precision=precision,
                           preferred_element_type=F32)


def _split_bf16(a):
    hi = a.astype(BF16)
    return hi, (a - hi.astype(F32)).astype(BF16)


def _rms_mod(x, nw, shift, scale):
    y = x * lax.rsqrt(jnp.mean(x * x, axis=-1, keepdims=True) + NORM_EPS)
    return (y * nw) * (1.0 + scale) + shift


def _mod_kernel(c_ref, w_ref, b_ref, o_ref):
    c = c_ref[...]
    a = c * jax.nn.sigmoid(c)
    o_ref[0] = jnp.dot(a, w_ref[0], precision=HIGHEST, preferred_element_type=F32) + b_ref[0]


def _modulation(cvec, w_ada, b_ada):
    depth, d, n = w_ada.shape
    tn = 1536
    return pl.pallas_call(
        _mod_kernel,
        grid=(depth, n // tn),
        in_specs=[pl.BlockSpec((8, d), lambda l, j: (0, 0)),
                  pl.BlockSpec((1, d, tn), lambda l, j: (l, 0, j)),
                  pl.BlockSpec((1, 1, tn), lambda l, j: (l, 0, j))],
        out_specs=pl.BlockSpec((1, 8, tn), lambda l, j: (l, 0, j)),
        out_shape=jax.ShapeDtypeStruct((depth, 8, n), F32),
        compiler_params=_cparams("parallel", "parallel"),
        name="adaln_mod",
    )(cvec, w_ada, b_ada.reshape(depth, 1, n))


def _inproj_kernel(x_ref, nw_ref, sh_ref, sc_ref, w_ref, o_ref, p_ref):
    h = _rms_mod(x_ref[0], nw_ref[...], sh_ref[0], sc_ref[0])
    r = jnp.dot(h.astype(BF16), w_ref[...], preferred_element_type=F32)
    o_ref[0] = r[:, :O_POOL]
    p_ref[0] = r[:, O_POOL:]


def _in_proj(x, nw, shift, scale, w_bf16):
    b, t, d = x.shape
    tm = min(512, t)
    return pl.pallas_call(
        _inproj_kernel,
        grid=(b, t // tm),
        in_specs=[pl.BlockSpec((1, tm, d), lambda i, j: (i, j, 0)),
                  pl.BlockSpec((1, d), lambda i, j: (0, 0)),
                  pl.BlockSpec((1, 1, d), lambda i, j: (i, 0, 0)),
                  pl.BlockSpec((1, 1, d), lambda i, j: (i, 0, 0)),
                  pl.BlockSpec((d, D_PROJ), lambda i, j: (0, 0))],
        out_specs=[pl.BlockSpec((1, tm, O_POOL), lambda i, j: (i, j, 0)),
                   pl.BlockSpec((1, tm, POOL_WIDTH), lambda i, j: (i, j, 0))],
        out_shape=[jax.ShapeDtypeStruct((b, t, O_POOL), F32),
                   jax.ShapeDtypeStruct((b, t, POOL_WIDTH), F32)],
        compiler_params=_cparams("parallel", "parallel"),
        name="in_proj",
    )(x, nw.reshape(1, d), shift, scale, w_bf16)


def _softmax_pv(s_list, v_list):
    m = s_list[0].max(axis=-1, keepdims=True)
    for s in s_list[1:]:
        m = jnp.maximum(m, s.max(axis=-1, keepdims=True))
    num = None
    den = None
    for s, v in zip(s_list, v_list):
        p = jnp.exp(s - m)
        pv = jnp.dot(p.astype(BF16), v.astype(BF16), preferred_element_type=F32)
        ps = p.sum(axis=-1, keepdims=True)
        num = pv if num is None else num + pv
        den = ps if den is None else den + ps
    return num / den


NA_ROWS_PER_TRIP = 4


def _na_kernel(q_ref, k_ref, v_ref, kc_ref, vc_ref, bias_ref, o_ref, *, rows, rb):
    i = pl.program_id(2)
    lane = lax.broadcasted_iota(jnp.int32, (1, LANES), 1)
    first = lane < HEAD_DIM
    kc = kc_ref[0].astype(BF16)
    vc = vc_ref[0].astype(BF16)
    scale = HEAD_DIM ** -0.5
    nk = NA_WIN_R * GRID_W

    def body(it, carry):
        for u in range(NA_ROWS_PER_TRIP):
            one_row(it * NA_ROWS_PER_TRIP + u)
        return carry

    def one_row(rr):
        r = i * rb + rr
        rs = jnp.clip(r - NA_WIN_R // 2, 0, rows - NA_WIN_R)
        delta = rs - r + (NA_WIN_R - 1)
        q = q_ref[0, pl.ds(pl.multiple_of(rr * GRID_W, GRID_W), GRID_W), :]
        k = k_ref[0, pl.ds(pl.multiple_of(rs * GRID_W, GRID_W), nk), :].astype(BF16)
        v = v_ref[0, pl.ds(pl.multiple_of(rs * GRID_W, GRID_W), nk), :].astype(BF16)
        q2 = jnp.concatenate([jnp.where(first, q, 0.0), jnp.where(first, 0.0, q)], axis=0).astype(BF16)
        s = _dot_nt(q2, k) * scale + bias_ref[0, delta]
        sc = _dot_nt(q2, kc) * scale
        o2 = _softmax_pv([s, sc], [v, vc])
        o_ref[0, pl.ds(pl.multiple_of(rr * GRID_W, GRID_W), GRID_W), :] = jnp.where(
            first, o2[0:GRID_W], o2[GRID_W:2 * GRID_W])

    lax.fori_loop(0, rb // NA_ROWS_PER_TRIP, body, 0)


def _na_attention(px, pc, bias8):
    b, s, _ = px.shape
    c = pc.shape[1]
    rows = s // GRID_W
    assert rows >= NA_WIN_R and s % GRID_W == 0
    rb = min(8, rows)
    nq, nkb, nvb = 0, HEAD_PAIRS, 2 * HEAD_PAIRS
    return pl.pallas_call(
        functools.partial(_na_kernel, rows=rows, rb=rb),
        grid=(b, HEAD_PAIRS, rows // rb),
        in_specs=[pl.BlockSpec((1, rb * GRID_W, LANES), lambda bi, hp, i: (bi, i, nq + hp)),
                  pl.BlockSpec((1, s, LANES), lambda bi, hp, i: (bi, 0, nkb + hp)),
                  pl.BlockSpec((1, s, LANES), lambda bi, hp, i: (bi, 0, nvb + hp)),
                  pl.BlockSpec((1, c, LANES), lambda bi, hp, i: (bi, 0, nkb + hp)),
                  pl.BlockSpec((1, c, LANES), lambda bi, hp, i: (bi, 0, nvb + hp)),
                  pl.BlockSpec((1, NA_WIN_R, 2 * GRID_W, NA_WIN_R * GRID_W),
                               lambda bi, hp, i: (hp, 0, 0, 0))],
        out_specs=pl.BlockSpec((1, rb * GRID_W, LANES), lambda bi, hp, i: (bi, i, hp)),
        out_shape=jax.ShapeDtypeStruct((b, s, NA_WIDTH), F32),
        compiler_params=_cparams("parallel", "parallel", "arbitrary"),
        name="na_attention",
    )(px, px, px, pc, pc, bias8)


def _ctx_attn_kernel(q_ref, k_ref, v_ref, o_ref):
    lane = lax.broadcasted_iota(jnp.int32, (1, LANES), 1)
    first = lane < HEAD_DIM
    q = q_ref[0]
    k = k_ref[0]
    v = v_ref[0]
    outs = []
    for h in range(2):
        qm = jnp.where(first if h == 0 else jnp.logical_not(first), q, 0.0)
        s = _dot_nt(qm, k) * HEAD_DIM ** -0.5
        outs.append(_softmax_pv([s], [v]))
    o_ref[0] = jnp.where(first, outs[0], outs[1])


def _ctx_attention(pc):
    b, c, _ = pc.shape
    return pl.pallas_call(
        _ctx_attn_kernel,
        grid=(b, HEAD_PAIRS),
        in_specs=[pl.BlockSpec((1, c, LANES), lambda bi, hp: (bi, 0, hp)),
                  pl.BlockSpec((1, c, LANES), lambda bi, hp: (bi, 0, HEAD_PAIRS + hp)),
                  pl.BlockSpec((1, c, LANES), lambda bi, hp: (bi, 0, 2 * HEAD_PAIRS + hp))],
        out_specs=pl.BlockSpec((1, c, LANES), lambda bi, hp: (bi, 0, hp)),
        out_shape=jax.ShapeDtypeStruct((b, c, NA_WIDTH), F32),
        compiler_params=_cparams("parallel", "parallel"),
        name="ctx_attention",
    )(pc, pc, pc)


RET_CHUNKS_PER_STEP = 4


def _ret_kernel(lg_ref, qf_ref, kf_ref, vf_ref, qb_ref, kb_ref, vb_ref,
                cf_ref, sf_ref, cb_ref, sb_ref, rf0_ref, rb0_ref,
                yf_ref, yb_ref, rfo_ref, rbo_ref, rf_scr, rb_scr, *, cpb):
    hp = pl.program_id(1)
    c = pl.program_id(2)
    nc = pl.num_programs(2)
    cs = RET_CHUNK

    @pl.when(c == 0)
    def _():
        rf_scr[...] = rf0_ref[0, 0]
        rb_scr[...] = rb0_ref[0, 0]

    lane = lax.broadcasted_iota(jnp.int32, (1, LANES), 1)
    first = lane < HEAD_DIM
    low = (lane % (2 * ROPE_PAIRS)) < ROPE_PAIRS
    pos = lax.broadcasted_iota(jnp.int32, (cs, 1), 0).astype(F32)
    ii = lax.broadcasted_iota(jnp.int32, (cs, cs), 0)
    jj = lax.broadcasted_iota(jnp.int32, (cs, cs), 1)
    diff = (ii - jj).astype(F32)
    same_head = (ii < HEAD_DIM) == (jj < HEAD_DIM)
    scale = HEAD_DIM ** -0.5

    def rope(x, cos, sin):
        swapped = jnp.where(low, pltpu.roll(x, LANES - ROPE_PAIRS, 1), pltpu.roll(x, ROPE_PAIRS, 1))
        return x * cos + swapped * sin

    def direction(d, q_ref, k_ref, v_ref, cos_ref, sin_ref, r_scr, y_ref):
        lg0 = lg_ref[d, 2 * hp]
        lg1 = lg_ref[d, 2 * hp + 1]
        lgv = jnp.where(first, lg0, lg1)
        decs = []
        for lg in (lg0, lg1):
            if d == 0:
                decs.append(jnp.where(diff >= 0, jnp.exp(jnp.maximum(diff, 0.0) * lg), 0.0))
            else:
                decs.append(jnp.where(diff <= 0, jnp.exp(jnp.maximum(-diff, 0.0) * lg), 0.0))
        dec2 = jnp.concatenate(decs, axis=0)
        if d == 0:
            xi = jnp.exp((pos + 1.0) * lgv)
            zeta = jnp.exp((cs - 1.0 - pos) * lgv)
        else:
            xi = jnp.exp((cs - pos) * lgv)
            zeta = jnp.exp(pos * lgv)
        chunk_decay = jnp.exp(cs * lgv)
        r = r_scr[...]
        for j in (range(cpb) if d == 0 else reversed(range(cpb))):
            rows = pl.ds(j * cs, cs)
            cos = cos_ref[rows, :]
            sin = sin_ref[rows, :]
            q = rope(q_ref[0, rows, :], cos, sin)
            k = rope(k_ref[0, rows, :], cos, sin) * scale
            v = v_ref[0, rows, :].astype(BF16)
            q2 = jnp.concatenate([jnp.where(first, q, 0.0), jnp.where(first, 0.0, q)], axis=0).astype(BF16)
            s2 = _dot_nt(q2, k.astype(BF16)) * dec2
            o2 = jnp.dot(s2.astype(BF16), v, preferred_element_type=F32)
            inner = jnp.where(first, o2[0:cs], o2[cs:2 * cs])
            y_ref[0, rows, :] = inner + jnp.dot((q * xi).astype(BF16), r.astype(BF16),
                                                preferred_element_type=F32)
            kv = lax.dot_general((k * zeta).astype(BF16), v, (((0,), (0,)), ((), ())),
                                 preferred_element_type=F32)
            r = chunk_decay * r + jnp.where(same_head, kv, 0.0)
        r_scr[...] = r

    direction(0, qf_ref, kf_ref, vf_ref, cf_ref, sf_ref, rf_scr, yf_ref)
    direction(1, qb_ref, kb_ref, vb_ref, cb_ref, sb_ref, rb_scr, yb_ref)

    @pl.when(c == nc - 1)
    def _():
        rfo_ref[0, 0] = rf_scr[...]
        rbo_ref[0, 0] = rb_scr[...]


def _retention(p, lg, cos_t, sin_t, rf0, rb0):
    b, t, _ = p.shape
    cpb = min(RET_CHUNKS_PER_STEP, t // RET_CHUNK)
    nc = t // (RET_CHUNK * cpb)
    qo, ko, vo = 3 * HEAD_PAIRS, 4 * HEAD_PAIRS, 5 * HEAD_PAIRS
    blk = (1, RET_CHUNK * cpb, LANES)
    fwd = lambda o: pl.BlockSpec(blk, lambda bi, hp, c: (bi, c, o + hp))
    bwd = lambda o: pl.BlockSpec(blk, lambda bi, hp, c: (bi, nc - 1 - c, o + hp))
    tab_f = pl.BlockSpec((RET_CHUNK * cpb, LANES), lambda bi, hp, c: (c, 0))
    tab_b = pl.BlockSpec((RET_CHUNK * cpb, LANES), lambda bi, hp, c: (nc - 1 - c, 0))
    st = pl.BlockSpec((1, 1, LANES, LANES), lambda bi, hp, c: (bi, hp, 0, 0))
    return pl.pallas_call(
        functools.partial(_ret_kernel, cpb=cpb),
        grid=(b, HEAD_PAIRS, nc),
        in_specs=[pl.BlockSpec(memory_space=pltpu.SMEM),
                  fwd(qo), fwd(ko), fwd(vo), bwd(qo), bwd(ko), bwd(vo),
                  tab_f, tab_f, tab_b, tab_b, st, st],
        out_specs=[pl.BlockSpec(blk, lambda bi, hp, c: (bi, c, hp)),
                   pl.BlockSpec(blk, lambda bi, hp, c: (bi, nc - 1 - c, hp)),
                   st, st],
        out_shape=[jax.ShapeDtypeStruct((b, t, RET_WIDTH), F32),
                   jax.ShapeDtypeStruct((b, t, RET_WIDTH), F32),
                   jax.ShapeDtypeStruct((b, HEAD_PAIRS, LANES, LANES), F32),
                   jax.ShapeDtypeStruct((b, HEAD_PAIRS, LANES, LANES), F32)],
        scratch_shapes=[pltpu.VMEM((LANES, LANES), F32), pltpu.VMEM((LANES, LANES), F32)],
        compiler_params=_cparams("parallel", "parallel", "arbitrary"),
        name="retention",
    )(lg, p, p, p, p, p, p, cos_t, sin_t, cos_t, sin_t, rf0, rb0)


def _pool_kernel(prev_ref, cur_ref, next_ref, w_ref, s_ref, o_ref, scr, *, t_total, tp):
    i = pl.program_id(1)
    n = pl.num_programs(1)
    halo = POOL_WINDOWS[-1] // 2
    x = cur_ref[0]
    scr[0:halo, :] = jnp.where(i > 0, prev_ref[0], 0.0)
    scr[halo:halo + tp, :] = x
    scr[halo + tp:2 * halo + tp, :] = jnp.where(i < n - 1, next_ref[0], 0.0)
    t = i * tp + lax.broadcasted_iota(jnp.int32, (tp, 1), 0)
    lane = lax.broadcasted_iota(jnp.int32, (1, POOL_WIDTH), 1)

    def shifted(s):
        return scr[halo + s:halo + s + tp, :]

    acc = None
    mean = None
    done = 0
    for g, w in enumerate(POOL_WINDOWS):
        half = w // 2
        for s in list(range(-half, -done)) + list(range(done, half)):
            sh = x if s == 0 else shifted(s)
            acc = sh if acc is None else acc + sh
        done = half
        cnt = (jnp.minimum(t + half, t_total) - jnp.maximum(t - half, 0)).astype(F32)
        mg = acc / cnt
        mean = mg if mean is None else jnp.where(lane >= g * POOL_GROUP, mg, mean)
    dlt = (mean - x).astype(BF16)
    o_ref[0] = jnp.dot(dlt, w_ref[...], preferred_element_type=F32) * s_ref[...]


def _pool(pin, w_bd_bf16, scale):
    b, t, _ = pin.shape
    tp = min(1024, t)
    halo = POOL_WINDOWS[-1] // 2
    nh = tp // halo
    last = t // halo - 1
    return pl.pallas_call(
        functools.partial(_pool_kernel, t_total=t, tp=tp),
        grid=(b, t // tp),
        in_specs=[pl.BlockSpec((1, halo, POOL_WIDTH), lambda bi, i: (bi, jnp.maximum(i * nh - 1, 0), 0)),
                  pl.BlockSpec((1, tp, POOL_WIDTH), lambda bi, i: (bi, i, 0)),
                  pl.BlockSpec((1, halo, POOL_WIDTH), lambda bi, i: (bi, jnp.minimum((i + 1) * nh, last), 0)),
                  pl.BlockSpec((POOL_WIDTH, POOL_WIDTH), lambda bi, i: (0, 0)),
                  pl.BlockSpec((1, POOL_WIDTH), lambda bi, i: (0, 0))],
        out_specs=pl.BlockSpec((1, tp, POOL_WIDTH), lambda bi, i: (bi, i, 0)),
        out_shape=jax.ShapeDtypeStruct((b, t, POOL_WIDTH), F32),
        scratch_shapes=[pltpu.VMEM((tp + 2 * halo, POOL_WIDTH), F32)],
        compiler_params=_cparams("parallel", "parallel"),
        name="multiscale_pool",
    )(pin, pin, pin, w_bd_bf16, scale.reshape(1, POOL_WIDTH))


def _out_kernel(na_ref, yf_ref, yb_ref, g_ref, pool_ref, x_ref, g1_ref, gnw_ref, avg_ref, wo_ref, o_ref):
    y = yf_ref[0] + yb_ref[0]
    avg = avg_ref[...]
    mu = jnp.dot(y, avg, precision=HIGHEST, preferred_element_type=F32)
    d = y - mu
    var = jnp.dot(d * d, avg, precision=HIGHEST, preferred_element_type=F32)
    yn = d * lax.rsqrt(var + NORM_EPS) * gnw_ref[...]
    g = g_ref[0]
    ret = yn * (g * jax.nn.sigmoid(g))
    mix = jnp.dot(na_ref[0].astype(BF16), wo_ref[0:NA_WIDTH, :], preferred_element_type=F32)
    mix += jnp.dot(ret.astype(BF16), wo_ref[NA_WIDTH:NA_WIDTH + RET_WIDTH, :], preferred_element_type=F32)
    mix += jnp.dot(pool_ref[0].astype(BF16), wo_ref[NA_WIDTH + RET_WIDTH:, :], preferred_element_type=F32)
    o_ref[0] = x_ref[0] + g1_ref[0] * mix


def _out_proj(na, yf, yb, p, pool, x, g1, gn_w, avg, wo_bf16):
    b, t, d = x.shape
    tm = min(512, t)
    row = lambda w: pl.BlockSpec((1, tm, w), lambda i, j: (i, j, 0))
    return pl.pallas_call(
        _out_kernel,
        grid=(b, t // tm),
        in_specs=[row(NA_WIDTH), row(RET_WIDTH), row(RET_WIDTH),
                  pl.BlockSpec((1, tm, RET_WIDTH), lambda i, j: (i, j, O_RET_G // RET_WIDTH)),
                  row(POOL_WIDTH), row(d),
                  pl.BlockSpec((1, 1, d), lambda i, j: (i, 0, 0)),
                  pl.BlockSpec((1, RET_WIDTH), lambda i, j: (0, 0)),
                  pl.BlockSpec((RET_WIDTH, RET_WIDTH), lambda i, j: (0, 0)),
                  pl.BlockSpec((d, d), lambda i, j: (0, 0))],
        out_specs=row(d),
        out_shape=jax.ShapeDtypeStruct((b, t, d), F32),
        compiler_params=_cparams("parallel", "parallel"),
        name="out_proj",
    )(na, yf, yb, p, pool, x, g1, gn_w.reshape(1, RET_WIDTH), avg, wo_bf16)


def _topk_rows(s, order=None, payload=None):
    n, m = s.shape
    if order is None:
        order = lax.broadcasted_iota(jnp.int32, (n, m), 0)
    krow = lax.broadcasted_iota(jnp.int32, (PEER_TOPK, m), 0)
    vals = jnp.zeros((PEER_TOPK, m), F32)
    idxs = jnp.zeros((PEER_TOPK, m), jnp.int32)
    for k in range(PEER_TOPK):
        mx = jnp.max(s, axis=0, keepdims=True)
        am = jnp.min(jnp.where(s == mx, order, jnp.iinfo(jnp.int32).max), axis=0, keepdims=True)
        sel = order == am
        pick = am if payload is None else jnp.sum(jnp.where(sel, payload, 0), axis=0, keepdims=True)
        vals = jnp.where(krow == k, mx, vals)
        idxs = jnp.where(krow == k, pick, idxs)
        s = jnp.where(sel, -jnp.inf, s)
    return vals, idxs


def _product_candidates(va, ia, vb, ib):
    k = PEER_TOPK
    sub = 8
    m = va.shape[1]
    row = lax.broadcasted_iota(jnp.int32, (sub, m), 0)
    cand, flat, eid = [], [], []

    def add(v, f, e, nvalid):
        cand.append(v if nvalid >= sub else jnp.where(row < nvalid, v, -jnp.inf))
        flat.append(f)
        eid.append(e)

    for i in range(sub):
        add(va[i:i + 1, :] + vb[0:sub, :], i * k + row, ia[i:i + 1, :] * PEER_NKEYS + ib[0:sub, :], k // (i + 1))
    add(va[0:1, :] + vb[sub:k, :], sub + row, ia[0:1, :] * PEER_NKEYS + ib[sub:k, :], sub)
    add(va[sub:k, :] + vb[0:1, :], (sub + row) * k, ia[sub:k, :] * PEER_NKEYS + ib[0:1, :], sub)
    return jnp.concatenate(cand, axis=0), jnp.concatenate(flat, axis=0), jnp.concatenate(eid, axis=0)


def _route_kernel(x_ref, nw_ref, sh_ref, sc_ref, wq_ref, keys_ref, idx_ref, gate_ref, q_scr, g_scr, i_scr, *, tm):
    h_hi, h_lo = _split_bf16(_rms_mod(x_ref[0], nw_ref[...], sh_ref[0], sc_ref[0]))
    q = (jnp.dot(h_hi, wq_ref[0], preferred_element_type=F32)
         + jnp.dot(h_hi, wq_ref[1], preferred_element_type=F32)
         + jnp.dot(h_lo, wq_ref[0], preferred_element_type=F32))
    for j in range(2 * PEER_HEADS):
        q_scr[j] = q[:, j * PEER_KEY_DIM:(j + 1) * PEER_KEY_DIM]

    def scores(half, hh):
        q_hi, q_lo = _split_bf16(q_scr[2 * hh + half])
        k_hi = keys_ref[0, half, hh]
        return _dot_nt(k_hi, q_hi) + _dot_nt(k_hi, q_lo) + _dot_nt(keys_ref[1, half, hh], q_hi)

    def head(hh, carry):
        sa = scores(0, hh)
        sb = scores(1, hh)
        va, ia = _topk_rows(sa)
        vb, ib = _topk_rows(sb)
        cand, flat, eid = _product_candidates(va, ia, vb, ib)
        sc, ei = _topk_rows(cand, order=flat, payload=eid)
        e = jnp.exp(sc - sc[0:1, :])
        off = pl.multiple_of(hh * PEER_TOPK, PEER_TOPK)
        g_scr[pl.ds(off, PEER_TOPK), :] = e / jnp.sum(e, axis=0, keepdims=True)
        i_scr[pl.ds(off, PEER_TOPK), :] = ei
        return carry

    lax.fori_loop(0, PEER_HEADS, head, 0)
    gate_ref[0] = g_scr[...].T
    idx_ref[0] = i_scr[...].T * TABLE_ROWS_PER_EXPERT


def _route(x, nw, shift, scale, wq, keys):
    b, t, d = x.shape
    tm = min(512, t)
    nt = t // tm
    nq = wq.shape[2]
    return pl.pallas_call(
        functools.partial(_route_kernel, tm=tm),
        grid=(b, nt),
        in_specs=[pl.BlockSpec((1, tm, d), lambda i, j: (i, j, 0)),
                  pl.BlockSpec((1, d), lambda i, j: (0, 0)),
                  pl.BlockSpec((1, 1, d), lambda i, j: (i, 0, 0)),
                  pl.BlockSpec((1, 1, d), lambda i, j: (i, 0, 0)),
                  pl.BlockSpec((2, d, nq), lambda i, j: (0, 0, 0)),
                  pl.BlockSpec((2, 2, PEER_HEADS, PEER_NKEYS, PEER_KEY_DIM), lambda i, j: (0, 0, 0, 0, 0))],
        out_specs=[pl.BlockSpec((1, tm, N_SEL), lambda i, j: (i, j, 0)),
                   pl.BlockSpec((1, tm, N_SEL), lambda i, j: (i, j, 0))],
        out_shape=[jax.ShapeDtypeStruct((b, t, N_SEL), jnp.int32),
                   jax.ShapeDtypeStruct((b, t, N_SEL), F32)],
        scratch_shapes=[pltpu.VMEM((2 * PEER_HEADS, tm, PEER_KEY_DIM), F32),
                        pltpu.VMEM((N_SEL, tm), F32),
                        pltpu.VMEM((N_SEL, tm), jnp.int32)],
        compiler_params=_cparams("parallel", "parallel"),
        name="peer_route",
    )(x, nw.reshape(1, d), shift, scale, wq, keys)


def _pack_kernel(t_ref, o_ref, *, te):
    x = t_ref[...]
    d = x.shape[1]

    def bf16_bits(a):
        return lax.bitcast_convert_type(a.astype(BF16).astype(F32), jnp.uint32)

    words = (bf16_bits(x[:, :d // 2]) >> 16) | bf16_bits(x[:, d // 2:])
    for q in range(TABLE_ROWS_PER_EXPERT):
        o_ref[pl.ds(q, te, stride=TABLE_ROWS_PER_EXPERT), :] = words[:, q * LANES:(q + 1) * LANES]


def _pack_table(tab):
    n, d = tab.shape
    te = 512
    return pl.pallas_call(
        functools.partial(_pack_kernel, te=te),
        grid=(n // te,),
        in_specs=[pl.BlockSpec((te, d), lambda i: (i, 0))],
        out_specs=pl.BlockSpec((te * TABLE_ROWS_PER_EXPERT, LANES), lambda i: (i, 0)),
        out_shape=jax.ShapeDtypeStruct((n * TABLE_ROWS_PER_EXPERT, LANES), jnp.uint32),
        compiler_params=_cparams("parallel"),
        name="pack_table",
    )(tab)


def _token_loop(tb, token, per_trip):
    def trip(i, carry):
        for j in range(per_trip):
            token(i * per_trip + j, j)
        return carry

    lax.fori_loop(0, tb // per_trip, trip, 0)


def _unpack(slab):
    lo = lax.bitcast_convert_type(slab << 16, F32)
    hi = lax.bitcast_convert_type(slab & jnp.uint32(0xFFFF0000), F32)
    return lo, hi


def _peer_act_kernel(idx_ref, tab_ref, x_ref, nw_ref, sh_ref, sc_ref, gate_ref, w_ref, h_scr, act_scr, *, tb):
    h = _rms_mod(x_ref[...], nw_ref[...], sh_ref[0], sc_ref[0])
    half = CHUNKS // 2
    for c in range(CHUNKS):
        piece = h[:, c * LANES:(c + 1) * LANES]
        base = (c // half) * CHUNKS + c % half
        h_scr[pl.ds(base, tb, stride=2 * CHUNKS), :] = piece
        h_scr[pl.ds(base + half, tb, stride=2 * CHUNKS), :] = piece
    sub = lax.broadcasted_iota(jnp.int32, (CHUNKS, N_SEL), 0)
    lane = lax.broadcasted_iota(jnp.int32, (CHUNKS, N_SEL), 1)
    slot = lane - PAIR_STRIDE * (sub >= half).astype(jnp.int32)

    def token(t, _):
        off = pl.multiple_of(t * 2 * CHUNKS, 2 * CHUNKS)
        hlo = h_scr[pl.ds(off, CHUNKS), :]
        hhi = h_scr[pl.ds(off + CHUNKS, CHUNKS), :]
        acc = jnp.zeros((CHUNKS, N_SEL), F32)
        for a, slab in _slab_pairs(tab_ref, idx_ref, t):
            lo, hi = _unpack(slab)
            dot = jnp.sum(lo * hlo + hi * hhi, axis=-1, keepdims=True)
            acc = jnp.where(slot == a, dot, acc)
        act_scr[pl.ds(t, 1), :] = jnp.sum(acc, axis=0, keepdims=True)

    _token_loop(tb, token, 8)
    a = act_scr[...]
    w_ref[...] = gate_ref[...] * (0.5 * a * (1.0 + lax.erf(a * (2.0 ** -0.5))))


PAIR_STRIDE = N_SEL // 4


def _slab_pairs(tab_ref, idx_ref, t):
    views = [idx_ref.at[pl.ds(t * N_SEL + q * PAIR_STRIDE, PAIR_STRIDE)] for q in range(4)]
    for k in range(PAIR_STRIDE):
        rows = [pl.multiple_of(v[k], TABLE_ROWS_PER_EXPERT) for v in views]
        for q in (0, 2):
            yield q * PAIR_STRIDE + k, jnp.concatenate(
                [tab_ref[pl.ds(rows[q], TABLE_ROWS_PER_EXPERT), :],
                 tab_ref[pl.ds(rows[q + 1], TABLE_ROWS_PER_EXPERT), :]], axis=0)


def _peer_out_kernel(pk_ref, tab_ref, x_ref, g2_ref, fw_ref, o_ref, p_scr, *, tb, final):
    half = CHUNKS // 2
    upper = lax.broadcasted_iota(jnp.int32, (CHUNKS, LANES), 0) >= half

    def token(t, _):
        views = [pk_ref.at[pl.ds(t * N_SEL + q * PAIR_STRIDE, PAIR_STRIDE)] for q in range(4)]
        accs = [jnp.zeros((CHUNKS, LANES), F32) for _ in range(4)]
        for k in range(PAIR_STRIDE):
            words = [v[k] for v in views]
            for q in (0, 2):
                rows = [pl.multiple_of(words[q + i] & 0xFFFF, TABLE_ROWS_PER_EXPERT) for i in range(2)]
                slab = jnp.concatenate([tab_ref[pl.ds(r, TABLE_ROWS_PER_EXPERT), :] for r in rows], axis=0)
                lo, hi = _unpack(slab)
                both = jnp.where(upper, words[q + 1], words[q])
                wv = lax.bitcast_convert_type(both & jnp.int32(-65536), F32)
                accs[q] = accs[q] + wv * lo
                accs[q + 1] = accs[q + 1] + wv * hi
        alo = accs[0] + accs[2]
        ahi = accs[1] + accs[3]
        off = pl.multiple_of(t * CHUNKS, CHUNKS)
        p_scr[pl.ds(off, half), :] = alo[0:half] + alo[half:CHUNKS]
        p_scr[pl.ds(pl.multiple_of(off + half, half), half), :] = ahi[0:half] + ahi[half:CHUNKS]

    _token_loop(tb, token, 2)
    peer = jnp.concatenate([p_scr[pl.ds(c, tb, stride=CHUNKS), :] for c in range(CHUNKS)], axis=1)
    y = x_ref[...] + g2_ref[0] * peer
    if final:
        y = y * lax.rsqrt(jnp.mean(y * y, axis=-1, keepdims=True) + NORM_EPS) * fw_ref[...]
    o_ref[...] = y


def _peer_residual(x, nw, shift, scale, g2, wq, keys, u, v, final_w, final):
    b, t, d = x.shape
    assert d == CHUNKS * LANES
    idx, gate = _route(x, nw, shift, scale, wq, keys)
    n = b * t
    tb = TOK_BLOCK
    per_batch = t // tb
    x2 = x.reshape(n, d)
    idx1 = idx.reshape(n * N_SEL)
    smem_blk = pl.BlockSpec((tb * N_SEL,), lambda i: (i,), memory_space=pltpu.SMEM)
    table = pl.BlockSpec(u.shape, lambda i: (0, 0), pipeline_mode=pl.Buffered(1))
    rows = pl.BlockSpec((tb, d), lambda i: (i, 0))
    sel = pl.BlockSpec((tb, N_SEL), lambda i: (i, 0))
    vec = pl.BlockSpec((1, 1, d), lambda i: (i // per_batch, 0, 0))
    const = pl.BlockSpec((1, d), lambda i: (0, 0))
    w = pl.pallas_call(
        functools.partial(_peer_act_kernel, tb=tb),
        grid=(n // tb,),
        in_specs=[smem_blk, table, rows, const, vec, vec, sel],
        out_specs=sel,
        out_shape=jax.ShapeDtypeStruct((n, N_SEL), F32),
        scratch_shapes=[pltpu.VMEM((tb * 2 * CHUNKS, LANES), F32), pltpu.VMEM((tb, N_SEL), F32)],
        compiler_params=_cparams("arbitrary"),
        name="peer_act",
    )(idx1, u, x2, nw.reshape(1, d), shift, scale, gate.reshape(n, N_SEL))
    w_bits = lax.bitcast_convert_type(w.astype(BF16).astype(F32), jnp.int32) & jnp.int32(-65536)
    packed = (w_bits | idx.reshape(n, N_SEL)).reshape(n * N_SEL)
    y = pl.pallas_call(
        functools.partial(_peer_out_kernel, tb=tb, final=final),
        grid=(n // tb,),
        in_specs=[smem_blk, table, rows, vec, const],
        out_specs=rows,
        out_shape=jax.ShapeDtypeStruct((n, d), F32),
        scratch_shapes=[pltpu.VMEM((tb * CHUNKS, LANES), F32)],
        compiler_params=_cparams("arbitrary"),
        name="peer_out",
    )(packed, v, x2, g2, final_w.reshape(1, d))
    return y.reshape(b, t, d)


def _na_bias_table(rpb):
    cq = jnp.arange(GRID_W)
    coff = jnp.clip(cq[None, :] - cq[:, None] + (NA_WIN_C - 1), 0, 2 * NA_WIN_C - 2)
    c_start = jnp.clip(cq - NA_WIN_C // 2, 0, GRID_W - NA_WIN_C)
    ok = (cq[None, :] >= c_start[:, None]) & (cq[None, :] < c_start[:, None] + NA_WIN_C)
    roff = jnp.arange(NA_WIN_R)[:, None] + jnp.arange(NA_WIN_R)[None, :]
    bias = rpb.astype(F32)[:, roff][..., coff]
    bias = jnp.where(ok[None, None, None], bias, NEG_INF)
    bias = bias.transpose(0, 1, 3, 2, 4).reshape(HEAD_PAIRS, 2, NA_WIN_R, GRID_W, NA_WIN_R * GRID_W)
    return bias.transpose(0, 2, 1, 3, 4).reshape(HEAD_PAIRS, NA_WIN_R, 2 * GRID_W, NA_WIN_R * GRID_W)


def _rope_tables(s):
    t = jnp.arange(s)
    row = (t // GRID_W).astype(F32)
    col = (t % GRID_W).astype(F32)
    inv = ROPE_BASE ** (-jnp.arange(ROPE_PAIRS, dtype=F32) / ROPE_PAIRS)
    cr, sr = jnp.cos(row[:, None] * inv), jnp.sin(row[:, None] * inv)
    cc, sc = jnp.cos(col[:, None] * inv), jnp.sin(col[:, None] * inv)
    cos = jnp.concatenate([cr, cr, cc, cc], axis=-1)
    sin = jnp.concatenate([-sr, sr, -sc, sc], axis=-1)
    return jnp.tile(cos, (1, 2)), jnp.tile(sin, (1, 2))


def _block_diag(blocks):
    n, a, bb = blocks.shape
    eye = jnp.eye(n, dtype=blocks.dtype)
    return (eye[:, None, :, None] * blocks[:, :, None, :]).reshape(n * a, n * bb)


def kernel(x, c, ctx, c_ctx, norm1_w, norm2_w, w_ada, b_ada, w_in, w_out, na_rpb, ret_decay_fwd, ret_decay_bwd, ret_gn_w, pool_w, pool_scale, peer_wq, peer_keys, peer_u, peer_v, final_norm_w):
    b, s, d = x.shape
    depth = w_in.shape[0]
    clen = ctx.shape[1]
    cvec = jnp.concatenate([c, c_ctx[None, :], jnp.zeros((8 - b - 1, d), F32)], axis=0)
    mod = _modulation(cvec, w_ada, b_ada)
    cos_x, sin_x = _rope_tables(s)
    cos_c, sin_c = jnp.ones((clen, LANES), F32), jnp.zeros((clen, LANES), F32)
    avg = _block_diag(jnp.full((RET_HEADS, HEAD_DIM, HEAD_DIM), 1.0 / HEAD_DIM, F32))
    zero_state = jnp.zeros((b, HEAD_PAIRS, LANES, LANES), F32)

    for l in range(depth):
        last = l == depth - 1
        mx = mod[l, :b].reshape(b, 1, 6, d)
        mc = jnp.broadcast_to(mod[l, b].reshape(1, 1, 6, d), (b, 1, 6, d))
        sh1, sc1, g1, sh2, sc2, g2 = [mx[:, :, i] for i in range(6)]
        csh1, csc1, cg1, csh2, csc2, cg2 = [mc[:, :, i] for i in range(6)]
        lg = jnp.stack([jax.nn.log_sigmoid(ret_decay_fwd[l].astype(F32)),
                        jax.nn.log_sigmoid(ret_decay_bwd[l].astype(F32))], axis=0)
        wi = w_in[l].astype(BF16)
        wo = w_out[l].astype(BF16)
        wpool = _block_diag(pool_w[l]).astype(BF16)
        bias8 = _na_bias_table(na_rpb[l])

        pc, pcp = _in_proj(ctx, norm1_w[l], csh1, csc1, wi)
        ycf, ycb, r_f, r_b = _retention(pc, lg, cos_c, sin_c, zero_state, zero_state)

        px, pxp = _in_proj(x, norm1_w[l], sh1, sc1, wi)
        na = _na_attention(px, pc, bias8)
        yf, yb, _, _ = _retention(px, lg, cos_x, sin_x, r_f, r_b)
        pool = _pool(pxp, wpool, pool_scale[l])
        x = _out_proj(na, yf, yb, px, pool, x, g1, ret_gn_w[l], avg, wo)
        wq = jnp.stack(_split_bf16(peer_wq[l]))
        keys = jnp.stack(_split_bf16(peer_keys[l]))
        u_tab = _pack_table(peer_u[l])
        v_tab = _pack_table(peer_v[l])
        x = _peer_residual(x, norm2_w[l], sh2, sc2, g2, wq, keys, u_tab, v_tab, final_norm_w, last)

        if not last:
            na_c = _ctx_attention(pc)
            pool_c = _pool(pcp, wpool, pool_scale[l])
            ctx = _out_proj(na_c, ycf, ycb, pc, pool_c, ctx, cg1, ret_gn_w[l], avg, wo)
            ctx = _peer_residual(ctx, norm2_w[l], csh2, csc2, cg2, wq, keys, u_tab, v_tab, final_norm_w, False)
    return x
```

```python
import functools

import jax
import jax.numpy as jnp
from jax import lax
from jax.experimental import pallas as pl
from jax.experimental.pallas import tpu as pltpu

D_MODEL = 1024
GRID_W = 64
HEAD_DIM = 64
NA_HEADS = 6
NA_WIN_R = 8
NA_WIN_C = 16
RET_HEADS = 6
RET_CHUNK = 128
POOL_WINDOWS = (2, 4, 8, 16)
POOL_GROUP = 64
NA_WIDTH = NA_HEADS * HEAD_DIM
RET_WIDTH = RET_HEADS * HEAD_DIM
POOL_WIDTH = POOL_GROUP * len(POOL_WINDOWS)
O_RET_G = 3 * NA_WIDTH + 3 * RET_WIDTH
O_POOL = O_RET_G + RET_WIDTH
D_PROJ = O_POOL + POOL_WIDTH
ROPE_BASE = 10000.0
ROPE_PAIRS = HEAD_DIM // 4
PEER_HEADS = 8
PEER_NKEYS = 128
PEER_KEY_DIM = 128
PEER_TOPK = 16
N_SEL = PEER_HEADS * PEER_TOPK
NORM_EPS = 1e-6
NEG_INF = -1e30

LANES = 128
VMEM_LIMIT_BYTES = 56 * 1024 * 1024

F32 = jnp.float32
BF16 = jnp.bfloat16
HIGHEST = lax.Precision.HIGHEST
HEAD_PAIRS = NA_HEADS // 2
TOK_BLOCK = 128
CHUNKS = D_MODEL // LANES
TABLE_ROWS_PER_EXPERT = CHUNKS // 2


def _cparams(*sem):
    return pltpu.CompilerParams(dimension_semantics=sem, vmem_limit_bytes=VMEM_LIMIT_BYTES)


def _dot_nt(a, b, precision=None):
    return lax.dot_general(a, b, (((1,), (1,)), ((), ())), precision=precision,
                           preferred_element_type=F32)


def _split_bf16(a):
    hi = a.astype(BF16)
    return hi, (a - hi.astype(F32)).astype(BF16)


def _rms_mod(x, nw, shift, scale):
    y = x * lax.rsqrt(jnp.mean(x * x, axis=-1, keepdims=True) + NORM_EPS)
    return (y * nw) * (1.0 + scale) + shift


def _mod_kernel(c_ref, w_ref, b_ref, o_ref):
    c = c_ref[...]
    a = c * jax.nn.sigmoid(c)
    o_ref[0] = jnp.dot(a, w_ref[0], precision=HIGHEST, preferred_element_type=F32) + b_ref[0]


def _modulation(cvec, w_ada, b_ada):
    depth, d, n = w_ada.shape
    tn = 1536
    return pl.pallas_call(
        _mod_kernel,
        grid=(depth, n // tn),
        in_specs=[pl.BlockSpec((8, d), lambda l, j: (0, 0)),
                  pl.BlockSpec((1, d, tn), lambda l, j: (l, 0, j)),
                  pl.BlockSpec((1, 1, tn), lambda l, j: (l, 0, j))],
        out_specs=pl.BlockSpec((1, 8, tn), lambda l, j: (l, 0, j)),
        out_shape=jax.ShapeDtypeStruct((depth, 8, n), F32),
        compiler_params=_cparams("parallel", "parallel"),
        name="adaln_mod",
    )(cvec, w_ada, b_ada.reshape(depth, 1, n))


def _inproj_kernel(x_ref, nw_ref, sh_ref, sc_ref, w_ref, o_ref, p_ref):
    h = _rms_mod(x_ref[0], nw_ref[...], sh_ref[0], sc_ref[0])
    r = jnp.dot(h.astype(BF16), w_ref[...], preferred_element_type=F32)
    o_ref[0] = r[:, :O_POOL]
    p_ref[0] = r[:, O_POOL:]


def _in_proj(x, nw, shift, scale, w_bf16):
    b, t, d = x.shape
    tm = min(512, t)
    return pl.pallas_call(
        _inproj_kernel,
        grid=(b, t // tm),
        in_specs=[pl.BlockSpec((1, tm, d), lambda i, j: (i, j, 0)),
                  pl.BlockSpec((1, d), lambda i, j: (0, 0)),
                  pl.BlockSpec((1, 1, d), lambda i, j: (i, 0, 0)),
                  pl.BlockSpec((1, 1, d), lambda i, j: (i, 0, 0)),
                  pl.BlockSpec((d, D_PROJ), lambda i, j: (0, 0))],
        out_specs=[pl.BlockSpec((1, tm, O_POOL), lambda i, j: (i, j, 0)),
                   pl.BlockSpec((1, tm, POOL_WIDTH), lambda i, j: (i, j, 0))],
        out_shape=[jax.ShapeDtypeStruct((b, t, O_POOL), F32),
                   jax.ShapeDtypeStruct((b, t, POOL_WIDTH), F32)],
        compiler_params=_cparams("parallel", "parallel"),
        name="in_proj",
    )(x, nw.reshape(1, d), shift, scale, w_bf16)


def _softmax_pv(s_list, v_list):
    m = s_list[0].max(axis=-1, keepdims=True)
    for s in s_list[1:]:
        m = jnp.maximum(m, s.max(axis=-1, keepdims=True))
    num = None
    den = None
    for s, v in zip(s_list, v_list):
        p = jnp.exp(s - m)
        pv = jnp.dot(p.astype(BF16), v.astype(BF16), preferred_element_type=F32)
        ps = p.sum(axis=-1, keepdims=True)
        num = pv if num is None else num + pv
        den = ps if den is None else den + ps
    return num / den


NA_ROWS_PER_TRIP = 4


def _na_kernel(q_ref, k_ref, v_ref, kc_ref, vc_ref, bias_ref, o_ref, *, rows, rb):
    i = pl.program_id(2)
    lane = lax.broadcasted_iota(jnp.int32, (1, LANES), 1)
    first = lane < HEAD_DIM
    kc = kc_ref[0].astype(BF16)
    vc = vc_ref[0].astype(BF16)
    scale = HEAD_DIM ** -0.5
    nk = NA_WIN_R * GRID_W

    def body(it, carry):
        for u in range(NA_ROWS_PER_TRIP):
            one_row(it * NA_ROWS_PER_TRIP + u)
        return carry

    def one_row(rr):
        r = i * rb + rr
        rs = jnp.clip(r - NA_WIN_R // 2, 0, rows - NA_WIN_R)
        delta = rs - r + (NA_WIN_R - 1)
        q = q_ref[0, pl.ds(pl.multiple_of(rr * GRID_W, GRID_W), GRID_W), :]
        k = k_ref[0, pl.ds(pl.multiple_of(rs * GRID_W, GRID_W), nk), :].astype(BF16)
        v = v_ref[0, pl.ds(pl.multiple_of(rs * GRID_W, GRID_W), nk), :].astype(BF16)
        q2 = jnp.concatenate([jnp.where(first, q, 0.0), jnp.where(first, 0.0, q)], axis=0).astype(BF16)
        s = _dot_nt(q2, k) * scale + bias_ref[0, delta]
        sc = _dot_nt(q2, kc) * scale
        o2 = _softmax_pv([s, sc], [v, vc])
        o_ref[0, pl.ds(pl.multiple_of(rr * GRID_W, GRID_W), GRID_W), :] = jnp.where(
            first, o2[0:GRID_W], o2[GRID_W:2 * GRID_W])

    lax.fori_loop(0, rb // NA_ROWS_PER_TRIP, body, 0)


def _na_attention(px, pc, bias8):
    b, s, _ = px.shape
    c = pc.shape[1]
    rows = s // GRID_W
    assert rows >= NA_WIN_R and s % GRID_W == 0
    rb = min(8, rows)
    nq, nkb, nvb = 0, HEAD_PAIRS, 2 * HEAD_PAIRS
    return pl.pallas_call(
        functools.partial(_na_kernel, rows=rows, rb=rb),
        grid=(b, HEAD_PAIRS, rows // rb),
        in_specs=[pl.BlockSpec((1, rb * GRID_W, LANES), lambda bi, hp, i: (bi, i, nq + hp)),
                  pl.BlockSpec((1, s, LANES), lambda bi, hp, i: (bi, 0, nkb + hp)),
                  pl.BlockSpec((1, s, LANES), lambda bi, hp, i: (bi, 0, nvb + hp)),
                  pl.BlockSpec((1, c, LANES), lambda bi, hp, i: (bi, 0, nkb + hp)),
                  pl.BlockSpec((1, c, LANES), lambda bi, hp, i: (bi, 0, nvb + hp)),
                  pl.BlockSpec((1, NA_WIN_R, 2 * GRID_W, NA_WIN_R * GRID_W),
                               lambda bi, hp, i: (hp, 0, 0, 0))],
        out_specs=pl.BlockSpec((1, rb * GRID_W, LANES), lambda bi, hp, i: (bi, i, hp)),
        out_shape=jax.ShapeDtypeStruct((b, s, NA_WIDTH), F32),
        compiler_params=_cparams("parallel", "parallel", "arbitrary"),
        name="na_attention",
    )(px, px, px, pc, pc, bias8)


def _ctx_attn_kernel(q_ref, k_ref, v_ref, o_ref):
    lane = lax.broadcasted_iota(jnp.int32, (1, LANES), 1)
    first = lane < HEAD_DIM
    q = q_ref[0]
    k = k_ref[0]
    v = v_ref[0]
    outs = []
    for h in range(2):
        qm = jnp.where(first if h == 0 else jnp.logical_not(first), q, 0.0)
        s = _dot_nt(qm, k) * HEAD_DIM ** -0.5
        outs.append(_softmax_pv([s], [v]))
    o_ref[0] = jnp.where(first, outs[0], outs[1])


def _ctx_attention(pc):
    b, c, _ = pc.shape
    return pl.pallas_call(
        _ctx_attn_kernel,
        grid=(b, HEAD_PAIRS),
        in_specs=[pl.BlockSpec((1, c, LANES), lambda bi, hp: (bi, 0, hp)),
                  pl.BlockSpec((1, c, LANES), lambda bi, hp: (bi, 0, HEAD_PAIRS + hp)),
                  pl.BlockSpec((1, c, LANES), lambda bi, hp: (bi, 0, 2 * HEAD_PAIRS + hp))],
        out_specs=pl.BlockSpec((1, c, LANES), lambda bi, hp: (bi, 0, hp)),
        out_shape=jax.ShapeDtypeStruct((b, c, NA_WIDTH), F32),
        compiler_params=_cparams("parallel", "parallel"),
        name="ctx_attention",
    )(pc, pc, pc)


RET_CHUNKS_PER_STEP = 8


def _ret_kernel(lg_ref, qf_ref, kf_ref, vf_ref, qb_ref, kb_ref, vb_ref,
                cf_ref, sf_ref, cb_ref, sb_ref, rf0_ref, rb0_ref,
                yf_ref, yb_ref, rfo_ref, rbo_ref, rf_scr, rb_scr, *, cpb):
    hp = pl.program_id(1)
    c = pl.program_id(2)
    nc = pl.num_programs(2)
    cs = RET_CHUNK

    @pl.when(c == 0)
    def _():
        rf_scr[...] = rf0_ref[0, 0]
        rb_scr[...] = rb0_ref[0, 0]

    lane = lax.broadcasted_iota(jnp.int32, (1, LANES), 1)
    first = lane < HEAD_DIM
    low = (lane % (2 * ROPE_PAIRS)) < ROPE_PAIRS
    pos = lax.broadcasted_iota(jnp.int32, (cs, 1), 0).astype(F32)
    ii = lax.broadcasted_iota(jnp.int32, (cs, cs), 0)
    jj = lax.broadcasted_iota(jnp.int32, (cs, cs), 1)
    diff = (ii - jj).astype(F32)
    same_head = (ii < HEAD_DIM) == (jj < HEAD_DIM)
    scale = HEAD_DIM ** -0.5

    def rope(x, cos, sin):
        swapped = jnp.where(low, pltpu.roll(x, LANES - ROPE_PAIRS, 1), pltpu.roll(x, ROPE_PAIRS, 1))
        return x * cos + swapped * sin

    def direction(d, q_ref, k_ref, v_ref, cos_ref, sin_ref, r_scr, y_ref):
        lg0 = lg_ref[d, 2 * hp]
        lg1 = lg_ref[d, 2 * hp + 1]
        lgv = jnp.where(first, lg0, lg1)
        decs = []
        for lg in (lg0, lg1):
            if d == 0:
                decs.append(jnp.where(diff >= 0, jnp.exp(jnp.maximum(diff, 0.0) * lg), 0.0))
            else:
                decs.append(jnp.where(diff <= 0, jnp.exp(jnp.maximum(-diff, 0.0) * lg), 0.0))
        dec2 = jnp.concatenate(decs, axis=0)
        if d == 0:
            xi = jnp.exp((pos + 1.0) * lgv)
            zeta = jnp.exp((cs - 1.0 - pos) * lgv)
        else:
            xi = jnp.exp((cs - pos) * lgv)
            zeta = jnp.exp(pos * lgv)
        chunk_decay = jnp.exp(cs * lgv)
        r = r_scr[...]
        for j in (range(cpb) if d == 0 else reversed(range(cpb))):
            rows = pl.ds(j * cs, cs)
            cos = cos_ref[rows, :]
            sin = sin_ref[rows, :]
            q = rope(q_ref[0, rows, :], cos, sin)
            k = rope(k_ref[0, rows, :], cos, sin) * scale
            v = v_ref[0, rows, :].astype(BF16)
            q2 = jnp.concatenate([jnp.where(first, q, 0.0), jnp.where(first, 0.0, q)], axis=0).astype(BF16)
            s2 = _dot_nt(q2, k.astype(BF16)) * dec2
            o2 = jnp.dot(s2.astype(BF16), v, preferred_element_type=F32)
            inner = jnp.where(first, o2[0:cs], o2[cs:2 * cs])
            y_ref[0, rows, :] = inner + jnp.dot((q * xi).astype(BF16), r.astype(BF16),
                                                preferred_element_type=F32)
            kv = lax.dot_general((k * zeta).astype(BF16), v, (((0,), (0,)), ((), ())),
                                 preferred_element_type=F32)
            r = chunk_decay * r + jnp.where(same_head, kv, 0.0)
        r_scr[...] = r

    direction(0, qf_ref, kf_ref, vf_ref, cf_ref, sf_ref, rf_scr, yf_ref)
    direction(1, qb_ref, kb_ref, vb_ref, cb_ref, sb_ref, rb_scr, yb_ref)

    @pl.when(c == nc - 1)
    def _():
        rfo_ref[0, 0] = rf_scr[...]
        rbo_ref[0, 0] = rb_scr[...]


def _retention(p, lg, cos_t, sin_t, rf0, rb0):
    b, t, _ = p.shape
    cpb = min(RET_CHUNKS_PER_STEP, t // RET_CHUNK)
    nc = t // (RET_CHUNK * cpb)
    qo, ko, vo = 3 * HEAD_PAIRS, 4 * HEAD_PAIRS, 5 * HEAD_PAIRS
    blk = (1, RET_CHUNK * cpb, LANES)
    fwd = lambda o: pl.BlockSpec(blk, lambda bi, hp, c: (bi, c, o + hp))
    bwd = lambda o: pl.BlockSpec(blk, lambda bi, hp, c: (bi, nc - 1 - c, o + hp))
    tab_f = pl.BlockSpec((RET_CHUNK * cpb, LANES), lambda bi, hp, c: (c, 0))
    tab_b = pl.BlockSpec((RET_CHUNK * cpb, LANES), lambda bi, hp, c: (nc - 1 - c, 0))
    st = pl.BlockSpec((1, 1, LANES, LANES), lambda bi, hp, c: (bi, hp, 0, 0))
    return pl.pallas_call(
        functools.partial(_ret_kernel, cpb=cpb),
        grid=(b, HEAD_PAIRS, nc),
        in_specs=[pl.BlockSpec(memory_space=pltpu.SMEM),
                  fwd(qo), fwd(ko), fwd(vo), bwd(qo), bwd(ko), bwd(vo),
                  tab_f, tab_f, tab_b, tab_b, st, st],
        out_specs=[pl.BlockSpec(blk, lambda bi, hp, c: (bi, c, hp)),
                   pl.BlockSpec(blk, lambda bi, hp, c: (bi, nc - 1 - c, hp)),
                   st, st],
        out_shape=[jax.ShapeDtypeStruct((b, t, RET_WIDTH), F32),
                   jax.ShapeDtypeStruct((b, t, RET_WIDTH), F32),
                   jax.ShapeDtypeStruct((b, HEAD_PAIRS, LANES, LANES), F32),
                   jax.ShapeDtypeStruct((b, HEAD_PAIRS, LANES, LANES), F32)],
        scratch_shapes=[pltpu.VMEM((LANES, LANES), F32), pltpu.VMEM((LANES, LANES), F32)],
        compiler_params=_cparams("parallel", "parallel", "arbitrary"),
        name="retention",
    )(lg, p, p, p, p, p, p, cos_t, sin_t, cos_t, sin_t, rf0, rb0)


def _pool_kernel(prev_ref, cur_ref, next_ref, w_ref, s_ref, o_ref, scr, *, t_total, tp):
    i = pl.program_id(1)
    n = pl.num_programs(1)
    halo = POOL_WINDOWS[-1] // 2
    x = cur_ref[0]
    scr[0:halo, :] = jnp.where(i > 0, prev_ref[0], 0.0)
    scr[halo:halo + tp, :] = x
    scr[halo + tp:2 * halo + tp, :] = jnp.where(i < n - 1, next_ref[0], 0.0)
    t = i * tp + lax.broadcasted_iota(jnp.int32, (tp, 1), 0)
    lane = lax.broadcasted_iota(jnp.int32, (1, POOL_WIDTH), 1)

    def shifted(s):
        return scr[halo + s:halo + s + tp, :]

    acc = None
    mean = None
    done = 0
    for g, w in enumerate(POOL_WINDOWS):
        half = w // 2
        for s in list(range(-half, -done)) + list(range(done, half)):
            sh = x if s == 0 else shifted(s)
            acc = sh if acc is None else acc + sh
        done = half
        cnt = (jnp.minimum(t + half, t_total) - jnp.maximum(t - half, 0)).astype(F32)
        mg = acc / cnt
        mean = mg if mean is None else jnp.where(lane >= g * POOL_GROUP, mg, mean)
    dlt = (mean - x).astype(BF16)
    o_ref[0] = jnp.dot(dlt, w_ref[...], preferred_element_type=F32) * s_ref[...]


def _pool(pin, w_bd_bf16, scale):
    b, t, _ = pin.shape
    tp = min(1024, t)
    halo = POOL_WINDOWS[-1] // 2
    nh = tp // halo
    last = t // halo - 1
    return pl.pallas_call(
        functools.partial(_pool_kernel, t_total=t, tp=tp),
        grid=(b, t // tp),
        in_specs=[pl.BlockSpec((1, halo, POOL_WIDTH), lambda bi, i: (bi, jnp.maximum(i * nh - 1, 0), 0)),
                  pl.BlockSpec((1, tp, POOL_WIDTH), lambda bi, i: (bi, i, 0)),
                  pl.BlockSpec((1, halo, POOL_WIDTH), lambda bi, i: (bi, jnp.minimum((i + 1) * nh, last), 0)),
                  pl.BlockSpec((POOL_WIDTH, POOL_WIDTH), lambda bi, i: (0, 0)),
                  pl.BlockSpec((1, POOL_WIDTH), lambda bi, i: (0, 0))],
        out_specs=pl.BlockSpec((1, tp, POOL_WIDTH), lambda bi, i: (bi, i, 0)),
        out_shape=jax.ShapeDtypeStruct((b, t, POOL_WIDTH), F32),
        scratch_shapes=[pltpu.VMEM((tp + 2 * halo, POOL_WIDTH), F32)],
        compiler_params=_cparams("parallel", "parallel"),
        name="multiscale_pool",
    )(pin, pin, pin, w_bd_bf16, scale.reshape(1, POOL_WIDTH))


def _out_kernel(na_ref, yf_ref, yb_ref, g_ref, pool_ref, x_ref, g1_ref, gnw_ref, avg_ref, wo_ref, o_ref):
    y = yf_ref[0] + yb_ref[0]
    avg = avg_ref[...]
    mu = jnp.dot(y, avg, precision=HIGHEST, preferred_element_type=F32)
    d = y - mu
    var = jnp.dot(d * d, avg, precision=HIGHEST, preferred_element_type=F32)
    yn = d * lax.rsqrt(var + NORM_EPS) * gnw_ref[...]
    g = g_ref[0]
    ret = yn * (g * jax.nn.sigmoid(g))
    mix = jnp.dot(na_ref[0].astype(BF16), wo_ref[0:NA_WIDTH, :], preferred_element_type=F32)
    mix += jnp.dot(ret.astype(BF16), wo_ref[NA_WIDTH:NA_WIDTH + RET_WIDTH, :], preferred_element_type=F32)
    mix += jnp.dot(pool_ref[0].astype(BF16), wo_ref[NA_WIDTH + RET_WIDTH:, :], preferred_element_type=F32)
    o_ref[0] = x_ref[0] + g1_ref[0] * mix


def _out_proj(na, yf, yb, p, pool, x, g1, gn_w, avg, wo_bf16):
    b, t, d = x.shape
    tm = min(512, t)
    row = lambda w: pl.BlockSpec((1, tm, w), lambda i, j: (i, j, 0))
    return pl.pallas_call(
        _out_kernel,
        grid=(b, t // tm),
        in_specs=[row(NA_WIDTH), row(RET_WIDTH), row(RET_WIDTH),
                  pl.BlockSpec((1, tm, RET_WIDTH), lambda i, j: (i, j, O_RET_G // RET_WIDTH)),
                  row(POOL_WIDTH), row(d),
                  pl.BlockSpec((1, 1, d), lambda i, j: (i, 0, 0)),
                  pl.BlockSpec((1, RET_WIDTH), lambda i, j: (0, 0)),
                  pl.BlockSpec((RET_WIDTH, RET_WIDTH), lambda i, j: (0, 0)),
                  pl.BlockSpec((d, d), lambda i, j: (0, 0))],
        out_specs=row(d),
        out_shape=jax.ShapeDtypeStruct((b, t, d), F32),
        compiler_params=_cparams("parallel", "parallel"),
        name="out_proj",
    )(na, yf, yb, p, pool, x, g1, gn_w.reshape(1, RET_WIDTH), avg, wo_bf16)


def _topk_rows(s, order=None, payload=None):
    n, m = s.shape
    if order is None:
        order = lax.broadcasted_iota(jnp.int32, (n, m), 0)
    krow = lax.broadcasted_iota(jnp.int32, (PEER_TOPK, m), 0)
    vals = jnp.zeros((PEER_TOPK, m), F32)
    idxs = jnp.zeros((PEER_TOPK, m), jnp.int32)
    for k in range(PEER_TOPK):
        mx = jnp.max(s, axis=0, keepdims=True)
        am = jnp.min(jnp.where(s == mx, order, jnp.iinfo(jnp.int32).max), axis=0, keepdims=True)
        sel = order == am
        pick = am if payload is None else jnp.sum(jnp.where(sel, payload, 0), axis=0, keepdims=True)
        vals = jnp.where(krow == k, mx, vals)
        idxs = jnp.where(krow == k, pick, idxs)
        s = jnp.where(sel, -jnp.inf, s)
    return vals, idxs


def _product_candidates(va, ia, vb, ib):
    k = PEER_TOPK
    sub = 8
    m = va.shape[1]
    row = lax.broadcasted_iota(jnp.int32, (sub, m), 0)
    cand, flat, eid = [], [], []

    def add(v, f, e, nvalid):
        cand.append(v if nvalid >= sub else jnp.where(row < nvalid, v, -jnp.inf))
        flat.append(f)
        eid.append(e)

    for i in range(sub):
        add(va[i:i + 1, :] + vb[0:sub, :], i * k + row, ia[i:i + 1, :] * PEER_NKEYS + ib[0:sub, :], k // (i + 1))
    add(va[0:1, :] + vb[sub:k, :], sub + row, ia[0:1, :] * PEER_NKEYS + ib[sub:k, :], sub)
    add(va[sub:k, :] + vb[0:1, :], (sub + row) * k, ia[sub:k, :] * PEER_NKEYS + ib[0:1, :], sub)
    return jnp.concatenate(cand, axis=0), jnp.concatenate(flat, axis=0), jnp.concatenate(eid, axis=0)


def _route_kernel(x_ref, nw_ref, sh_ref, sc_ref, wq_ref, keys_ref, idx_ref, gate_ref, q_scr, g_scr, i_scr, *, tm):
    h_hi, h_lo = _split_bf16(_rms_mod(x_ref[0], nw_ref[...], sh_ref[0], sc_ref[0]))
    q = (jnp.dot(h_hi, wq_ref[0], preferred_element_type=F32)
         + jnp.dot(h_hi, wq_ref[1], preferred_element_type=F32)
         + jnp.dot(h_lo, wq_ref[0], preferred_element_type=F32))
    for j in range(2 * PEER_HEADS):
        q_scr[j] = q[:, j * PEER_KEY_DIM:(j + 1) * PEER_KEY_DIM]

    def scores(half, hh):
        q_hi, q_lo = _split_bf16(q_scr[2 * hh + half])
        k_hi = keys_ref[0, half, hh]
        return _dot_nt(k_hi, q_hi) + _dot_nt(k_hi, q_lo) + _dot_nt(keys_ref[1, half, hh], q_hi)

    def head(hh, carry):
        sa = scores(0, hh)
        sb = scores(1, hh)
        va, ia = _topk_rows(sa)
        vb, ib = _topk_rows(sb)
        cand, flat, eid = _product_candidates(va, ia, vb, ib)
        sc, ei = _topk_rows(cand, order=flat, payload=eid)
        e = jnp.exp(sc - sc[0:1, :])
        off = pl.multiple_of(hh * PEER_TOPK, PEER_TOPK)
        g_scr[pl.ds(off, PEER_TOPK), :] = e / jnp.sum(e, axis=0, keepdims=True)
        i_scr[pl.ds(off, PEER_TOPK), :] = ei
        return carry

    lax.fori_loop(0, PEER_HEADS, head, 0)
    gate_ref[0] = g_scr[...].T
    idx_ref[0] = i_scr[...].T * TABLE_ROWS_PER_EXPERT


def _route(x, nw, shift, scale, wq, keys):
    b, t, d = x.shape
    tm = min(512, t)
    nt = t // tm
    nq = wq.shape[2]
    return pl.pallas_call(
        functools.partial(_route_kernel, tm=tm),
        grid=(b, nt),
        in_specs=[pl.BlockSpec((1, tm, d), lambda i, j: (i, j, 0)),
                  pl.BlockSpec((1, d), lambda i, j: (0, 0)),
                  pl.BlockSpec((1, 1, d), lambda i, j: (i, 0, 0)),
                  pl.BlockSpec((1, 1, d), lambda i, j: (i, 0, 0)),
                  pl.BlockSpec((2, d, nq), lambda i, j: (0, 0, 0)),
                  pl.BlockSpec((2, 2, PEER_HEADS, PEER_NKEYS, PEER_KEY_DIM), lambda i, j: (0, 0, 0, 0, 0))],
        out_specs=[pl.BlockSpec((1, tm, N_SEL), lambda i, j: (i, j, 0)),
                   pl.BlockSpec((1, tm, N_SEL), lambda i, j: (i, j, 0))],
        out_shape=[jax.ShapeDtypeStruct((b, t, N_SEL), jnp.int32),
                   jax.ShapeDtypeStruct((b, t, N_SEL), F32)],
        scratch_shapes=[pltpu.VMEM((2 * PEER_HEADS, tm, PEER_KEY_DIM), F32),
                        pltpu.VMEM((N_SEL, tm), F32),
                        pltpu.VMEM((N_SEL, tm), jnp.int32)],
        compiler_params=_cparams("parallel", "parallel"),
        name="peer_route",
    )(x, nw.reshape(1, d), shift, scale, wq, keys)


def _pack_kernel(t_ref, o_ref, *, te):
    x = t_ref[0]
    d = x.shape[1]

    def bf16_bits(a):
        return lax.bitcast_convert_type(a.astype(BF16).astype(F32), jnp.uint32)

    words = (bf16_bits(x[:, :d // 2]) >> 16) | bf16_bits(x[:, d // 2:])
    for q in range(TABLE_ROWS_PER_EXPERT):
        o_ref[pl.ds(q, te, stride=TABLE_ROWS_PER_EXPERT), :] = words[:, q * LANES:(q + 1) * LANES]


def _pack_table(tabs, layer):
    _, n, d = tabs.shape
    te = 512
    return pl.pallas_call(
        functools.partial(_pack_kernel, te=te),
        grid=(n // te,),
        in_specs=[pl.BlockSpec((1, te, d), lambda i: (layer, i, 0))],
        out_specs=pl.BlockSpec((te * TABLE_ROWS_PER_EXPERT, LANES), lambda i: (i, 0)),
        out_shape=jax.ShapeDtypeStruct((n * TABLE_ROWS_PER_EXPERT, LANES), jnp.uint32),
        compiler_params=_cparams("parallel"),
        name="pack_table",
    )(tabs)


def _token_loop(tb, token, per_trip):
    def trip(i, carry):
        for j in range(per_trip):
            token(i * per_trip + j, j)
        return carry

    lax.fori_loop(0, tb // per_trip, trip, 0)


def _unpack(slab):
    lo = lax.bitcast_convert_type(slab << 16, F32)
    hi = lax.bitcast_convert_type(slab & jnp.uint32(0xFFFF0000), F32)
    return lo, hi


def _peer_act_kernel(idx_ref, tab_ref, x_ref, nw_ref, sh_ref, sc_ref, gate_ref, w_ref, h_scr, act_scr, *, tb):
    h = _rms_mod(x_ref[...], nw_ref[...], sh_ref[0], sc_ref[0])
    half = CHUNKS // 2
    for c in range(CHUNKS):
        piece = h[:, c * LANES:(c + 1) * LANES]
        base = (c // half) * CHUNKS + c % half
        h_scr[pl.ds(base, tb, stride=2 * CHUNKS), :] = piece
        h_scr[pl.ds(base + half, tb, stride=2 * CHUNKS), :] = piece
    sub = lax.broadcasted_iota(jnp.int32, (CHUNKS, N_SEL), 0)
    lane = lax.broadcasted_iota(jnp.int32, (CHUNKS, N_SEL), 1)
    slot = lane - PAIR_STRIDE * (sub >= half).astype(jnp.int32)

    def token(t, _):
        off = pl.multiple_of(t * 2 * CHUNKS, 2 * CHUNKS)
        hlo = h_scr[pl.ds(off, CHUNKS), :]
        hhi = h_scr[pl.ds(off + CHUNKS, CHUNKS), :]
        acc = jnp.zeros((CHUNKS, N_SEL), F32)
        for a, slab in _slab_pairs(tab_ref, idx_ref, t):
            lo, hi = _unpack(slab)
            dot = jnp.sum(lo * hlo + hi * hhi, axis=-1, keepdims=True)
            acc = jnp.where(slot == a, dot, acc)
        act_scr[pl.ds(t, 1), :] = jnp.sum(acc, axis=0, keepdims=True)

    _token_loop(tb, token, 8)
    a = act_scr[...]
    w_ref[...] = gate_ref[...] * (0.5 * a * (1.0 + lax.erf(a * (2.0 ** -0.5))))


PAIR_STRIDE = N_SEL // 4


def _slab_pairs(tab_ref, idx_ref, t):
    views = [idx_ref.at[pl.ds(t * N_SEL + q * PAIR_STRIDE, PAIR_STRIDE)] for q in range(4)]
    for k in range(PAIR_STRIDE):
        rows = [pl.multiple_of(v[k], TABLE_ROWS_PER_EXPERT) for v in views]
        for q in (0, 2):
            yield q * PAIR_STRIDE + k, jnp.concatenate(
                [tab_ref[pl.ds(rows[q], TABLE_ROWS_PER_EXPERT), :],
                 tab_ref[pl.ds(rows[q + 1], TABLE_ROWS_PER_EXPERT), :]], axis=0)


def _peer_out_kernel(pk_ref, tab_ref, x_ref, g2_ref, fw_ref, o_ref, p_scr, *, tb, final):
    half = CHUNKS // 2
    upper = lax.broadcasted_iota(jnp.int32, (CHUNKS, LANES), 0) >= half

    def token(t, _):
        views = [pk_ref.at[pl.ds(t * N_SEL + q * PAIR_STRIDE, PAIR_STRIDE)] for q in range(4)]
        accs = [jnp.zeros((CHUNKS, LANES), F32) for _ in range(4)]
        for k in range(PAIR_STRIDE):
            words = [v[k] for v in views]
            for q in (0, 2):
                rows = [pl.multiple_of(words[q + i] & 0xFFFF, TABLE_ROWS_PER_EXPERT) for i in range(2)]
                slab = jnp.concatenate([tab_ref[pl.ds(r, TABLE_ROWS_PER_EXPERT), :] for r in rows], axis=0)
                lo, hi = _unpack(slab)
                both = jnp.where(upper, words[q + 1], words[q])
                wv = lax.bitcast_convert_type(both & jnp.int32(-65536), F32)
                accs[q] = accs[q] + wv * lo
                accs[q + 1] = accs[q + 1] + wv * hi
        alo = accs[0] + accs[2]
        ahi = accs[1] + accs[3]
        off = pl.multiple_of(t * CHUNKS, CHUNKS)
        p_scr[pl.ds(off, half), :] = alo[0:half] + alo[half:CHUNKS]
        p_scr[pl.ds(pl.multiple_of(off + half, half), half), :] = ahi[0:half] + ahi[half:CHUNKS]

    _token_loop(tb, token, 2)
    peer = jnp.concatenate([p_scr[pl.ds(c, tb, stride=CHUNKS), :] for c in range(CHUNKS)], axis=1)
    y = x_ref[...] + g2_ref[0] * peer
    if final:
        y = y * lax.rsqrt(jnp.mean(y * y, axis=-1, keepdims=True) + NORM_EPS) * fw_ref[...]
    o_ref[...] = y


def _peer_residual(x, nw, shift, scale, g2, wq, keys, u, v, final_w, final):
    b, t, d = x.shape
    assert d == CHUNKS * LANES
    idx, gate = _route(x, nw, shift, scale, wq, keys)
    n = b * t
    tb = TOK_BLOCK
    per_batch = t // tb
    x2 = x.reshape(n, d)
    idx1 = idx.reshape(n * N_SEL)
    smem_blk = pl.BlockSpec((tb * N_SEL,), lambda i: (i,), memory_space=pltpu.SMEM)
    table = pl.BlockSpec(u.shape, lambda i: (0, 0), pipeline_mode=pl.Buffered(1))
    rows = pl.BlockSpec((tb, d), lambda i: (i, 0))
    sel = pl.BlockSpec((tb, N_SEL), lambda i: (i, 0))
    vec = pl.BlockSpec((1, 1, d), lambda i: (i // per_batch, 0, 0))
    const = pl.BlockSpec((1, d), lambda i: (0, 0))
    w = pl.pallas_call(
        functools.partial(_peer_act_kernel, tb=tb),
        grid=(n // tb,),
        in_specs=[smem_blk, table, rows, const, vec, vec, sel],
        out_specs=sel,
        out_shape=jax.ShapeDtypeStruct((n, N_SEL), F32),
        scratch_shapes=[pltpu.VMEM((tb * 2 * CHUNKS, LANES), F32), pltpu.VMEM((tb, N_SEL), F32)],
        compiler_params=_cparams("arbitrary"),
        name="peer_act",
    )(idx1, u, x2, nw.reshape(1, d), shift, scale, gate.reshape(n, N_SEL))
    w_bits = lax.bitcast_convert_type(w.astype(BF16).astype(F32), jnp.int32) & jnp.int32(-65536)
    packed = (w_bits | idx.reshape(n, N_SEL)).reshape(n * N_SEL)
    y = pl.pallas_call(
        functools.partial(_peer_out_kernel, tb=tb, final=final),
        grid=(n // tb,),
        in_specs=[smem_blk, table, rows, vec, const],
        out_specs=rows,
        out_shape=jax.ShapeDtypeStruct((n, d), F32),
        scratch_shapes=[pltpu.VMEM((tb * CHUNKS, LANES), F32)],
        compiler_params=_cparams("arbitrary"),
        name="peer_out",
    )(packed, v, x2, g2, final_w.reshape(1, d))
    return y.reshape(b, t, d)


def _na_bias_table(rpb):
    cq = jnp.arange(GRID_W)
    coff = jnp.clip(cq[None, :] - cq[:, None] + (NA_WIN_C - 1), 0, 2 * NA_WIN_C - 2)
    c_start = jnp.clip(cq - NA_WIN_C // 2, 0, GRID_W - NA_WIN_C)
    ok = (cq[None, :] >= c_start[:, None]) & (cq[None, :] < c_start[:, None] + NA_WIN_C)
    roff = jnp.arange(NA_WIN_R)[:, None] + jnp.arange(NA_WIN_R)[None, :]
    bias = rpb.astype(F32)[:, :, coff][:, roff]
    bias = jnp.where(ok[None, None, None], bias, NEG_INF)
    bias = bias.transpose(0, 1, 3, 2, 4).reshape(HEAD_PAIRS, 2, NA_WIN_R, GRID_W, NA_WIN_R * GRID_W)
    return bias.transpose(0, 2, 1, 3, 4).reshape(HEAD_PAIRS, NA_WIN_R, 2 * GRID_W, NA_WIN_R * GRID_W)


def _rope_tables(s):
    t = jnp.arange(s)
    row = (t // GRID_W).astype(F32)
    col = (t % GRID_W).astype(F32)
    inv = ROPE_BASE ** (-jnp.arange(ROPE_PAIRS, dtype=F32) / ROPE_PAIRS)
    cr, sr = jnp.cos(row[:, None] * inv), jnp.sin(row[:, None] * inv)
    cc, sc = jnp.cos(col[:, None] * inv), jnp.sin(col[:, None] * inv)
    cos = jnp.concatenate([cr, cr, cc, cc], axis=-1)
    sin = jnp.concatenate([-sr, sr, -sc, sc], axis=-1)
    return jnp.tile(cos, (1, 2)), jnp.tile(sin, (1, 2))


def _block_diag(blocks):
    n, a, bb = blocks.shape
    eye = jnp.eye(n, dtype=blocks.dtype)
    return (eye[:, None, :, None] * blocks[:, :, None, :]).reshape(n * a, n * bb)


def kernel(x, c, ctx, c_ctx, norm1_w, norm2_w, w_ada, b_ada, w_in, w_out, na_rpb, ret_decay_fwd, ret_decay_bwd, ret_gn_w, pool_w, pool_scale, peer_wq, peer_keys, peer_u, peer_v, final_norm_w):
    b, s, d = x.shape
    depth = w_in.shape[0]
    clen = ctx.shape[1]
    cvec = jnp.concatenate([c, c_ctx[None, :], jnp.zeros((8 - b - 1, d), F32)], axis=0)
    mod = _modulation(cvec, w_ada, b_ada)
    cos_x, sin_x = _rope_tables(s)
    cos_c, sin_c = jnp.ones((clen, LANES), F32), jnp.zeros((clen, LANES), F32)
    avg = _block_diag(jnp.full((RET_HEADS, HEAD_DIM, HEAD_DIM), 1.0 / HEAD_DIM, F32))
    zero_state = jnp.zeros((b, HEAD_PAIRS, LANES, LANES), F32)

    for l in range(depth):
        last = l == depth - 1
        mx = mod[l, :b].reshape(b, 1, 6, d)
        mc = jnp.broadcast_to(mod[l, b].reshape(1, 1, 6, d), (b, 1, 6, d))
        sh1, sc1, g1, sh2, sc2, g2 = [mx[:, :, i] for i in range(6)]
        csh1, csc1, cg1, csh2, csc2, cg2 = [mc[:, :, i] for i in range(6)]
        lg = jnp.stack([jax.nn.log_sigmoid(ret_decay_fwd[l].astype(F32)),
                        jax.nn.log_sigmoid(ret_decay_bwd[l].astype(F32))], axis=0)
        wi = w_in[l].astype(BF16)
        wo = w_out[l].astype(BF16)
        wpool = _block_diag(pool_w[l]).astype(BF16)
        bias8 = _na_bias_table(na_rpb[l])

        pc, pcp = _in_proj(ctx, norm1_w[l], csh1, csc1, wi)
        ycf, ycb, r_f, r_b = _retention(pc, lg, cos_c, sin_c, zero_state, zero_state)

        px, pxp = _in_proj(x, norm1_w[l], sh1, sc1, wi)
        na = _na_attention(px, pc, bias8)
        yf, yb, _, _ = _retention(px, lg, cos_x, sin_x, r_f, r_b)
        pool = _pool(pxp, wpool, pool_scale[l])
        x = _out_proj(na, yf, yb, px, pool, x, g1, ret_gn_w[l], avg, wo)
        wq = jnp.stack(_split_bf16(peer_wq[l]))
        keys = jnp.stack(_split_bf16(peer_keys[l]))
        u_tab = _pack_table(peer_u, l)
        v_tab = _pack_table(peer_v, l)
        x = _peer_residual(x, norm2_w[l], sh2, sc2, g2, wq, keys, u_tab, v_tab, final_norm_w, last)

        if not last:
            na_c = _ctx_attention(pc)
            pool_c = _pool(pcp, wpool, pool_scale[l])
            ctx = _out_proj(na_c, ycf, ycb, pc, pool_c, ctx, cg1, ret_gn_w[l], avg, wo)
            ctx = _peer_residual(ctx, norm2_w[l], csh2, csc2, cg2, wq, keys, u_tab, v_tab, final_norm_w, False)
    return x
```

```python
import functools

import jax
import jax.numpy as jnp
from jax import lax
from jax.experimental import pallas as pl
from jax.experimental.pallas import tpu as pltpu

D_MODEL = 1024
GRID_W = 64
HEAD_DIM = 64
NA_HEADS = 6
NA_WIN_R = 8
NA_WIN_C = 16
RET_HEADS = 6
RET_CHUNK = 128
POOL_WINDOWS = (2, 4, 8, 16)
POOL_GROUP = 64
NA_WIDTH = NA_HEADS * HEAD_DIM
RET_WIDTH = RET_HEADS * HEAD_DIM
POOL_WIDTH = POOL_GROUP * len(POOL_WINDOWS)
O_RET_G = 3 * NA_WIDTH + 3 * RET_WIDTH
O_POOL = O_RET_G + RET_WIDTH
D_PROJ = O_POOL + POOL_WIDTH
ROPE_BASE = 10000.0
ROPE_PAIRS = HEAD_DIM // 4
PEER_HEADS = 8
PEER_NKEYS = 128
PEER_KEY_DIM = 128
PEER_TOPK = 16
N_SEL = PEER_HEADS * PEER_TOPK
NORM_EPS = 1e-6
NEG_INF = -1e30

LANES = 128
VMEM_LIMIT_BYTES = 56 * 1024 * 1024

F32 = jnp.float32
BF16 = jnp.bfloat16
HIGHEST = lax.Precision.HIGHEST
HEAD_PAIRS = NA_HEADS // 2
TOK_BLOCK = 128
CHUNKS = D_MODEL // LANES
TABLE_ROWS_PER_EXPERT = CHUNKS // 2


def _cparams(*sem):
    return pltpu.CompilerParams(dimension_semantics=sem, vmem_limit_bytes=VMEM_LIMIT_BYTES)


def _dot_nt(a, b, precision=None):
    return lax.dot_general(a, b, (((1,), (1,)), ((), ())), precision=precision,
                           preferred_element_type=F32)


def _split_bf16(a):
    hi = a.astype(BF16)
    return hi, (a - hi.astype(F32)).astype(BF16)


def _rms_mod(x, nw, shift, scale):
    y = x * lax.rsqrt(jnp.mean(x * x, axis=-1, keepdims=True) + NORM_EPS)
    return (y * nw) * (1.0 + scale) + shift


def _mod_kernel(c_ref, w_ref, b_ref, o_ref):
    c = c_ref[...]
    a = c * jax.nn.sigmoid(c)
    o_ref[0] = jnp.dot(a, w_ref[0], precision=HIGHEST, preferred_element_type=F32) + b_ref[0]


def _modulation(cvec, w_ada, b_ada):
    depth, d, n = w_ada.shape
    tn = 1536
    return pl.pallas_call(
        _mod_kernel,
        grid=(depth, n // tn),
        in_specs=[pl.BlockSpec((8, d), lambda l, j: (0, 0)),
                  pl.BlockSpec((1, d, tn), lambda l, j: (l, 0, j)),
                  pl.BlockSpec((1, 1, tn), lambda l, j: (l, 0, j))],
        out_specs=pl.BlockSpec((1, 8, tn), lambda l, j: (l, 0, j)),
        out_shape=jax.ShapeDtypeStruct((depth, 8, n), F32),
        compiler_params=_cparams("parallel", "parallel"),
        name="adaln_mod",
    )(cvec, w_ada, b_ada.reshape(depth, 1, n))


def _inproj_kernel(x_ref, nw_ref, sh_ref, sc_ref, w_ref, o_ref, p_ref):
    h = _rms_mod(x_ref[0], nw_ref[...], sh_ref[0], sc_ref[0])
    r = jnp.dot(h.astype(BF16), w_ref[...], preferred_element_type=F32)
    o_ref[0] = r[:, :O_POOL]
    p_ref[0] = r[:, O_POOL:]


def _in_proj(x, nw, shift, scale, w_bf16):
    b, t, d = x.shape
    tm = min(512, t)
    return pl.pallas_call(
        _inproj_kernel,
        grid=(b, t // tm),
        in_specs=[pl.BlockSpec((1, tm, d), lambda i, j: (i, j, 0)),
                  pl.BlockSpec((1, d), lambda i, j: (0, 0)),
                  pl.BlockSpec((1, 1, d), lambda i, j: (i, 0, 0)),
                  pl.BlockSpec((1, 1, d), lambda i, j: (i, 0, 0)),
                  pl.BlockSpec((d, D_PROJ), lambda i, j: (0, 0))],
        out_specs=[pl.BlockSpec((1, tm, O_POOL), lambda i, j: (i, j, 0)),
                   pl.BlockSpec((1, tm, POOL_WIDTH), lambda i, j: (i, j, 0))],
        out_shape=[jax.ShapeDtypeStruct((b, t, O_POOL), F32),
                   jax.ShapeDtypeStruct((b, t, POOL_WIDTH), F32)],
        compiler_params=_cparams("parallel", "parallel"),
        name="in_proj",
    )(x, nw.reshape(1, d), shift, scale, w_bf16)


def _softmax_pv(s_list, v_list):
    m = s_list[0].max(axis=-1, keepdims=True)
    for s in s_list[1:]:
        m = jnp.maximum(m, s.max(axis=-1, keepdims=True))
    num = None
    den = None
    for s, v in zip(s_list, v_list):
        p = jnp.exp(s - m)
        pv = jnp.dot(p.astype(BF16), v.astype(BF16), preferred_element_type=F32)
        ps = p.sum(axis=-1, keepdims=True)
        num = pv if num is None else num + pv
        den = ps if den is None else den + ps
    return num / den


NA_ROWS_PER_TRIP = 4


def _na_kernel(q_ref, k_ref, v_ref, kc_ref, vc_ref, bias_ref, o_ref, *, rows, rb):
    i = pl.program_id(2)
    lane = lax.broadcasted_iota(jnp.int32, (1, LANES), 1)
    first = lane < HEAD_DIM
    kc = kc_ref[0].astype(BF16)
    vc = vc_ref[0].astype(BF16)
    scale = HEAD_DIM ** -0.5
    nk = NA_WIN_R * GRID_W

    def body(it, carry):
        for u in range(NA_ROWS_PER_TRIP):
            one_row(it * NA_ROWS_PER_TRIP + u)
        return carry

    def one_row(rr):
        r = i * rb + rr
        rs = jnp.clip(r - NA_WIN_R // 2, 0, rows - NA_WIN_R)
        delta = rs - r + (NA_WIN_R - 1)
        q = q_ref[0, pl.ds(pl.multiple_of(rr * GRID_W, GRID_W), GRID_W), :]
        k = k_ref[0, pl.ds(pl.multiple_of(rs * GRID_W, GRID_W), nk), :].astype(BF16)
        v = v_ref[0, pl.ds(pl.multiple_of(rs * GRID_W, GRID_W), nk), :].astype(BF16)
        q2 = jnp.concatenate([jnp.where(first, q, 0.0), jnp.where(first, 0.0, q)], axis=0).astype(BF16)
        s = _dot_nt(q2, k) * scale + bias_ref[0, delta]
        sc = _dot_nt(q2, kc) * scale
        o2 = _softmax_pv([s, sc], [v, vc])
        o_ref[0, pl.ds(pl.multiple_of(rr * GRID_W, GRID_W), GRID_W), :] = jnp.where(
            first, o2[0:GRID_W], o2[GRID_W:2 * GRID_W])

    lax.fori_loop(0, rb // NA_ROWS_PER_TRIP, body, 0)


def _na_attention(px, pc, bias8):
    b, s, _ = px.shape
    c = pc.shape[1]
    rows = s // GRID_W
    assert rows >= NA_WIN_R and s % GRID_W == 0
    rb = min(8, rows)
    nq, nkb, nvb = 0, HEAD_PAIRS, 2 * HEAD_PAIRS
    return pl.pallas_call(
        functools.partial(_na_kernel, rows=rows, rb=rb),
        grid=(b, HEAD_PAIRS, rows // rb),
        in_specs=[pl.BlockSpec((1, rb * GRID_W, LANES), lambda bi, hp, i: (bi, i, nq + hp)),
                  pl.BlockSpec((1, s, LANES), lambda bi, hp, i: (bi, 0, nkb + hp)),
                  pl.BlockSpec((1, s, LANES), lambda bi, hp, i: (bi, 0, nvb + hp)),
                  pl.BlockSpec((1, c, LANES), lambda bi, hp, i: (bi, 0, nkb + hp)),
                  pl.BlockSpec((1, c, LANES), lambda bi, hp, i: (bi, 0, nvb + hp)),
                  pl.BlockSpec((1, NA_WIN_R, 2 * GRID_W, NA_WIN_R * GRID_W),
                               lambda bi, hp, i: (hp, 0, 0, 0))],
        out_specs=pl.BlockSpec((1, rb * GRID_W, LANES), lambda bi, hp, i: (bi, i, hp)),
        out_shape=jax.ShapeDtypeStruct((b, s, NA_WIDTH), F32),
        compiler_params=_cparams("parallel", "parallel", "arbitrary"),
        name="na_attention",
    )(px, px, px, pc, pc, bias8)


def _ctx_attn_kernel(q_ref, k_ref, v_ref, o_ref):
    lane = lax.broadcasted_iota(jnp.int32, (1, LANES), 1)
    first = lane < HEAD_DIM
    q = q_ref[0]
    k = k_ref[0]
    v = v_ref[0]
    outs = []
    for h in range(2):
        qm = jnp.where(first if h == 0 else jnp.logical_not(first), q, 0.0)
        s = _dot_nt(qm, k) * HEAD_DIM ** -0.5
        outs.append(_softmax_pv([s], [v]))
    o_ref[0] = jnp.where(first, outs[0], outs[1])


def _ctx_attention(pc):
    b, c, _ = pc.shape
    return pl.pallas_call(
        _ctx_attn_kernel,
        grid=(b, HEAD_PAIRS),
        in_specs=[pl.BlockSpec((1, c, LANES), lambda bi, hp: (bi, 0, hp)),
                  pl.BlockSpec((1, c, LANES), lambda bi, hp: (bi, 0, HEAD_PAIRS + hp)),
                  pl.BlockSpec((1, c, LANES), lambda bi, hp: (bi, 0, 2 * HEAD_PAIRS + hp))],
        out_specs=pl.BlockSpec((1, c, LANES), lambda bi, hp: (bi, 0, hp)),
        out_shape=jax.ShapeDtypeStruct((b, c, NA_WIDTH), F32),
        compiler_params=_cparams("parallel", "parallel"),
        name="ctx_attention",
    )(pc, pc, pc)


RET_CHUNKS_PER_STEP = 8


def _ret_kernel(lg_ref, qf_ref, kf_ref, vf_ref, qb_ref, kb_ref, vb_ref,
                cf_ref, sf_ref, cb_ref, sb_ref, rf0_ref, rb0_ref,
                yf_ref, yb_ref, rfo_ref, rbo_ref, rf_scr, rb_scr, *, cpb):
    hp = pl.program_id(1)
    c = pl.program_id(2)
    nc = pl.num_programs(2)
    cs = RET_CHUNK

    @pl.when(c == 0)
    def _():
        rf_scr[...] = rf0_ref[0, 0]
        rb_scr[...] = rb0_ref[0, 0]

    lane = lax.broadcasted_iota(jnp.int32, (1, LANES), 1)
    first = lane < HEAD_DIM
    low = (lane % (2 * ROPE_PAIRS)) < ROPE_PAIRS
    pos = lax.broadcasted_iota(jnp.int32, (cs, 1), 0).astype(F32)
    ii = lax.broadcasted_iota(jnp.int32, (cs, cs), 0)
    jj = lax.broadcasted_iota(jnp.int32, (cs, cs), 1)
    diff = (ii - jj).astype(F32)
    same_head = (ii < HEAD_DIM) == (jj < HEAD_DIM)
    scale = HEAD_DIM ** -0.5

    def rope(x, cos, sin):
        swapped = jnp.where(low, pltpu.roll(x, LANES - ROPE_PAIRS, 1), pltpu.roll(x, ROPE_PAIRS, 1))
        return x * cos + swapped * sin

    def direction(d, q_ref, k_ref, v_ref, cos_ref, sin_ref, r_scr, y_ref):
        lg0 = lg_ref[d, 2 * hp]
        lg1 = lg_ref[d, 2 * hp + 1]
        lgv = jnp.where(first, lg0, lg1)
        decs = []
        for lg in (lg0, lg1):
            if d == 0:
                decs.append(jnp.where(diff >= 0, jnp.exp(jnp.maximum(diff, 0.0) * lg), 0.0))
            else:
                decs.append(jnp.where(diff <= 0, jnp.exp(jnp.maximum(-diff, 0.0) * lg), 0.0))
        dec2 = jnp.concatenate(decs, axis=0)
        if d == 0:
            xi = jnp.exp((pos + 1.0) * lgv)
            zeta = jnp.exp((cs - 1.0 - pos) * lgv)
        else:
            xi = jnp.exp((cs - pos) * lgv)
            zeta = jnp.exp(pos * lgv)
        chunk_decay = jnp.exp(cs * lgv)
        r = r_scr[...]
        for j in (range(cpb) if d == 0 else reversed(range(cpb))):
            rows = pl.ds(j * cs, cs)
            cos = cos_ref[rows, :]
            sin = sin_ref[rows, :]
            q = rope(q_ref[0, rows, :], cos, sin)
            k = rope(k_ref[0, rows, :], cos, sin) * scale
            v = v_ref[0, rows, :].astype(BF16)
            q2 = jnp.concatenate([jnp.where(first, q, 0.0), jnp.where(first, 0.0, q)], axis=0).astype(BF16)
            s2 = _dot_nt(q2, k.astype(BF16)) * dec2
            o2 = jnp.dot(s2.astype(BF16), v, preferred_element_type=F32)
            inner = jnp.where(first, o2[0:cs], o2[cs:2 * cs])
            y_ref[0, rows, :] = inner + jnp.dot((q * xi).astype(BF16), r.astype(BF16),
                                                preferred_element_type=F32)
            kv = lax.dot_general((k * zeta).astype(BF16), v, (((0,), (0,)), ((), ())),
                                 preferred_element_type=F32)
            r = chunk_decay * r + jnp.where(same_head, kv, 0.0)
        r_scr[...] = r

    direction(0, qf_ref, kf_ref, vf_ref, cf_ref, sf_ref, rf_scr, yf_ref)
    direction(1, qb_ref, kb_ref, vb_ref, cb_ref, sb_ref, rb_scr, yb_ref)

    @pl.when(c == nc - 1)
    def _():
        rfo_ref[0, 0] = rf_scr[...]
        rbo_ref[0, 0] = rb_scr[...]


def _retention(p, lg, cos_t, sin_t, rf0, rb0):
    b, t, _ = p.shape
    cpb = min(RET_CHUNKS_PER_STEP, t // RET_CHUNK)
    nc = t // (RET_CHUNK * cpb)
    qo, ko, vo = 3 * HEAD_PAIRS, 4 * HEAD_PAIRS, 5 * HEAD_PAIRS
    blk = (1, RET_CHUNK * cpb, LANES)
    fwd = lambda o: pl.BlockSpec(blk, lambda bi, hp, c: (bi, c, o + hp))
    bwd = lambda o: pl.BlockSpec(blk, lambda bi, hp, c: (bi, nc - 1 - c, o + hp))
    tab_f = pl.BlockSpec((RET_CHUNK * cpb, LANES), lambda bi, hp, c: (c, 0))
    tab_b = pl.BlockSpec((RET_CHUNK * cpb, LANES), lambda bi, hp, c: (nc - 1 - c, 0))
    st = pl.BlockSpec((1, 1, LANES, LANES), lambda bi, hp, c: (bi, hp, 0, 0))
    return pl.pallas_call(
        functools.partial(_ret_kernel, cpb=cpb),
        grid=(b, HEAD_PAIRS, nc),
        in_specs=[pl.BlockSpec(memory_space=pltpu.SMEM),
                  fwd(qo), fwd(ko), fwd(vo), bwd(qo), bwd(ko), bwd(vo),
                  tab_f, tab_f, tab_b, tab_b, st, st],
        out_specs=[pl.BlockSpec(blk, lambda bi, hp, c: (bi, c, hp)),
                   pl.BlockSpec(blk, lambda bi, hp, c: (bi, nc - 1 - c, hp)),
                   st, st],
        out_shape=[jax.ShapeDtypeStruct((b, t, RET_WIDTH), F32),
                   jax.ShapeDtypeStruct((b, t, RET_WIDTH), F32),
                   jax.ShapeDtypeStruct((b, HEAD_PAIRS, LANES, LANES), F32),
                   jax.ShapeDtypeStruct((b, HEAD_PAIRS, LANES, LANES), F32)],
        scratch_shapes=[pltpu.VMEM((LANES, LANES), F32), pltpu.VMEM((LANES, LANES), F32)],
        compiler_params=_cparams("parallel", "parallel", "arbitrary"),
        name="retention",
    )(lg, p, p, p, p, p, p, cos_t, sin_t, cos_t, sin_t, rf0, rb0)


def _pool_kernel(prev_ref, cur_ref, next_ref, w_ref, s_ref, o_ref, scr, *, t_total, tp):
    i = pl.program_id(1)
    n = pl.num_programs(1)
    halo = POOL_WINDOWS[-1] // 2
    x = cur_ref[0]
    scr[0:halo, :] = jnp.where(i > 0, prev_ref[0], 0.0)
    scr[halo:halo + tp, :] = x
    scr[halo + tp:2 * halo + tp, :] = jnp.where(i < n - 1, next_ref[0], 0.0)
    t = i * tp + lax.broadcasted_iota(jnp.int32, (tp, 1), 0)
    lane = lax.broadcasted_iota(jnp.int32, (1, POOL_WIDTH), 1)

    def shifted(s):
        return scr[halo + s:halo + s + tp, :]

    acc = None
    mean = None
    done = 0
    for g, w in enumerate(POOL_WINDOWS):
        half = w // 2
        for s in list(range(-half, -done)) + list(range(done, half)):
            sh = x if s == 0 else shifted(s)
            acc = sh if acc is None else acc + sh
        done = half
        cnt = (jnp.minimum(t + half, t_total) - jnp.maximum(t - half, 0)).astype(F32)
        mg = acc / cnt
        mean = mg if mean is None else jnp.where(lane >= g * POOL_GROUP, mg, mean)
    dlt = (mean - x).astype(BF16)
    o_ref[0] = jnp.dot(dlt, w_ref[...], preferred_element_type=F32) * s_ref[...]


def _pool(pin, w_bd_bf16, scale):
    b, t, _ = pin.shape
    tp = min(1024, t)
    halo = POOL_WINDOWS[-1] // 2
    nh = tp // halo
    last = t // halo - 1
    return pl.pallas_call(
        functools.partial(_pool_kernel, t_total=t, tp=tp),
        grid=(b, t // tp),
        in_specs=[pl.BlockSpec((1, halo, POOL_WIDTH), lambda bi, i: (bi, jnp.maximum(i * nh - 1, 0), 0)),
                  pl.BlockSpec((1, tp, POOL_WIDTH), lambda bi, i: (bi, i, 0)),
                  pl.BlockSpec((1, halo, POOL_WIDTH), lambda bi, i: (bi, jnp.minimum((i + 1) * nh, last), 0)),
                  pl.BlockSpec((POOL_WIDTH, POOL_WIDTH), lambda bi, i: (0, 0)),
                  pl.BlockSpec((1, POOL_WIDTH), lambda bi, i: (0, 0))],
        out_specs=pl.BlockSpec((1, tp, POOL_WIDTH), lambda bi, i: (bi, i, 0)),
        out_shape=jax.ShapeDtypeStruct((b, t, POOL_WIDTH), F32),
        scratch_shapes=[pltpu.VMEM((tp + 2 * halo, POOL_WIDTH), F32)],
        compiler_params=_cparams("parallel", "parallel"),
        name="multiscale_pool",
    )(pin, pin, pin, w_bd_bf16, scale.reshape(1, POOL_WIDTH))


def _out_kernel(na_ref, yf_ref, yb_ref, g_ref, pool_ref, x_ref, g1_ref, gnw_ref, avg_ref, wo_ref, o_ref):
    y = yf_ref[0] + yb_ref[0]
    avg = avg_ref[...]
    mu = jnp.dot(y, avg, precision=HIGHEST, preferred_element_type=F32)
    d = y - mu
    var = jnp.dot(d * d, avg, precision=HIGHEST, preferred_element_type=F32)
    yn = d * lax.rsqrt(var + NORM_EPS) * gnw_ref[...]
    g = g_ref[0]
    ret = yn * (g * jax.nn.sigmoid(g))
    mix = jnp.dot(na_ref[0].astype(BF16), wo_ref[0:NA_WIDTH, :], preferred_element_type=F32)
    mix += jnp.dot(ret.astype(BF16), wo_ref[NA_WIDTH:NA_WIDTH + RET_WIDTH, :], preferred_element_type=F32)
    mix += jnp.dot(pool_ref[0].astype(BF16), wo_ref[NA_WIDTH + RET_WIDTH:, :], preferred_element_type=F32)
    o_ref[0] = x_ref[0] + g1_ref[0] * mix


def _out_proj(na, yf, yb, p, pool, x, g1, gn_w, avg, wo_bf16):
    b, t, d = x.shape
    tm = min(512, t)
    row = lambda w: pl.BlockSpec((1, tm, w), lambda i, j: (i, j, 0))
    return pl.pallas_call(
        _out_kernel,
        grid=(b, t // tm),
        in_specs=[row(NA_WIDTH), row(RET_WIDTH), row(RET_WIDTH),
                  pl.BlockSpec((1, tm, RET_WIDTH), lambda i, j: (i, j, O_RET_G // RET_WIDTH)),
                  row(POOL_WIDTH), row(d),
                  pl.BlockSpec((1, 1, d), lambda i, j: (i, 0, 0)),
                  pl.BlockSpec((1, RET_WIDTH), lambda i, j: (0, 0)),
                  pl.BlockSpec((RET_WIDTH, RET_WIDTH), lambda i, j: (0, 0)),
                  pl.BlockSpec((d, d), lambda i, j: (0, 0))],
        out_specs=row(d),
        out_shape=jax.ShapeDtypeStruct((b, t, d), F32),
        compiler_params=_cparams("parallel", "parallel"),
        name="out_proj",
    )(na, yf, yb, p, pool, x, g1, gn_w.reshape(1, RET_WIDTH), avg, wo_bf16)


def _topk_rows(s, order=None, payload=None):
    n, m = s.shape
    if order is None:
        order = lax.broadcasted_iota(jnp.int32, (n, m), 0)
    krow = lax.broadcasted_iota(jnp.int32, (PEER_TOPK, m), 0)
    vals = jnp.zeros((PEER_TOPK, m), F32)
    idxs = jnp.zeros((PEER_TOPK, m), jnp.int32)
    for k in range(PEER_TOPK):
        mx = jnp.max(s, axis=0, keepdims=True)
        am = jnp.min(jnp.where(s == mx, order, jnp.iinfo(jnp.int32).max), axis=0, keepdims=True)
        sel = order == am
        pick = am if payload is None else jnp.sum(jnp.where(sel, payload, 0), axis=0, keepdims=True)
        vals = jnp.where(krow == k, mx, vals)
        idxs = jnp.where(krow == k, pick, idxs)
        s = jnp.where(sel, -jnp.inf, s)
    return vals, idxs


def _product_candidates(va, ia, vb, ib):
    k = PEER_TOPK
    sub = 8
    m = va.shape[1]
    row = lax.broadcasted_iota(jnp.int32, (sub, m), 0)
    cand, flat, eid = [], [], []

    def add(v, f, e, nvalid):
        cand.append(v if nvalid >= sub else jnp.where(row < nvalid, v, -jnp.inf))
        flat.append(f)
        eid.append(e)

    for i in range(sub):
        add(va[i:i + 1, :] + vb[0:sub, :], i * k + row, ia[i:i + 1, :] * PEER_NKEYS + ib[0:sub, :], k // (i + 1))
    add(va[0:1, :] + vb[sub:k, :], sub + row, ia[0:1, :] * PEER_NKEYS + ib[sub:k, :], sub)
    add(va[sub:k, :] + vb[0:1, :], (sub + row) * k, ia[sub:k, :] * PEER_NKEYS + ib[0:1, :], sub)
    return jnp.concatenate(cand, axis=0), jnp.concatenate(flat, axis=0), jnp.concatenate(eid, axis=0)


def _route_kernel(x_ref, nw_ref, sh_ref, sc_ref, wq_ref, keys_ref, idx_ref, gate_ref, q_scr, g_scr, i_scr, *, tm):
    h_hi, h_lo = _split_bf16(_rms_mod(x_ref[0], nw_ref[...], sh_ref[0], sc_ref[0]))
    q = (jnp.dot(h_hi, wq_ref[0], preferred_element_type=F32)
         + jnp.dot(h_hi, wq_ref[1], preferred_element_type=F32)
         + jnp.dot(h_lo, wq_ref[0], preferred_element_type=F32))
    for j in range(2 * PEER_HEADS):
        q_scr[j] = q[:, j * PEER_KEY_DIM:(j + 1) * PEER_KEY_DIM]

    def scores(half, hh):
        q_hi, q_lo = _split_bf16(q_scr[2 * hh + half])
        k_hi = keys_ref[0, half, hh]
        return _dot_nt(k_hi, q_hi) + _dot_nt(k_hi, q_lo) + _dot_nt(keys_ref[1, half, hh], q_hi)

    def head(hh, carry):
        sa = scores(0, hh)
        sb = scores(1, hh)
        va, ia = _topk_rows(sa)
        vb, ib = _topk_rows(sb)
        cand, flat, eid = _product_candidates(va, ia, vb, ib)
        sc, ei = _topk_rows(cand, order=flat, payload=eid)
        e = jnp.exp(sc - sc[0:1, :])
        off = pl.multiple_of(hh * PEER_TOPK, PEER_TOPK)
        g_scr[pl.ds(off, PEER_TOPK), :] = e / jnp.sum(e, axis=0, keepdims=True)
        i_scr[pl.ds(off, PEER_TOPK), :] = ei
        return carry

    lax.fori_loop(0, PEER_HEADS, head, 0)
    gate_ref[0] = g_scr[...].T
    idx_ref[0] = i_scr[...].T * TABLE_ROWS_PER_EXPERT


def _route(x, nw, shift, scale, wq, keys):
    b, t, d = x.shape
    tm = min(1024, t)
    nt = t // tm
    nq = wq.shape[2]
    return pl.pallas_call(
        functools.partial(_route_kernel, tm=tm),
        grid=(b, nt),
        in_specs=[pl.BlockSpec((1, tm, d), lambda i, j: (i, j, 0)),
                  pl.BlockSpec((1, d), lambda i, j: (0, 0)),
                  pl.BlockSpec((1, 1, d), lambda i, j: (i, 0, 0)),
                  pl.BlockSpec((1, 1, d), lambda i, j: (i, 0, 0)),
                  pl.BlockSpec((2, d, nq), lambda i, j: (0, 0, 0)),
                  pl.BlockSpec((2, 2, PEER_HEADS, PEER_NKEYS, PEER_KEY_DIM), lambda i, j: (0, 0, 0, 0, 0))],
        out_specs=[pl.BlockSpec((1, tm, N_SEL), lambda i, j: (i, j, 0)),
                   pl.BlockSpec((1, tm, N_SEL), lambda i, j: (i, j, 0))],
        out_shape=[jax.ShapeDtypeStruct((b, t, N_SEL), jnp.int32),
                   jax.ShapeDtypeStruct((b, t, N_SEL), F32)],
        scratch_shapes=[pltpu.VMEM((2 * PEER_HEADS, tm, PEER_KEY_DIM), F32),
                        pltpu.VMEM((N_SEL, tm), F32),
                        pltpu.VMEM((N_SEL, tm), jnp.int32)],
        compiler_params=_cparams("parallel", "parallel"),
        name="peer_route",
    )(x, nw.reshape(1, d), shift, scale, wq, keys)


def _pack_kernel(t_ref, o_ref, *, te):
    x = t_ref[0]
    d = x.shape[1]

    def bf16_bits(a):
        return lax.bitcast_convert_type(a.astype(BF16).astype(F32), jnp.uint32)

    words = (bf16_bits(x[:, :d // 2]) >> 16) | bf16_bits(x[:, d // 2:])
    for q in range(TABLE_ROWS_PER_EXPERT):
        o_ref[pl.ds(q, te, stride=TABLE_ROWS_PER_EXPERT), :] = words[:, q * LANES:(q + 1) * LANES]


def _pack_table(tabs, layer):
    _, n, d = tabs.shape
    te = 512
    return pl.pallas_call(
        functools.partial(_pack_kernel, te=te),
        grid=(n // te,),
        in_specs=[pl.BlockSpec((1, te, d), lambda i: (layer, i, 0))],
        out_specs=pl.BlockSpec((te * TABLE_ROWS_PER_EXPERT, LANES), lambda i: (i, 0)),
        out_shape=jax.ShapeDtypeStruct((n * TABLE_ROWS_PER_EXPERT, LANES), jnp.uint32),
        compiler_params=_cparams("parallel"),
        name="pack_table",
    )(tabs)


def _token_loop(tb, token, per_trip):
    def trip(i, carry):
        for j in range(per_trip):
            token(i * per_trip + j, j)
        return carry

    lax.fori_loop(0, tb // per_trip, trip, 0)


def _unpack(slab):
    lo = lax.bitcast_convert_type(slab << 16, F32)
    hi = lax.bitcast_convert_type(slab & jnp.uint32(0xFFFF0000), F32)
    return lo, hi


def _peer_act_kernel(idx_ref, tab_ref, x_ref, nw_ref, sh_ref, sc_ref, gate_ref, w_ref, h_scr, act_scr, *, tb):
    h = _rms_mod(x_ref[...], nw_ref[...], sh_ref[0], sc_ref[0])
    half = CHUNKS // 2
    for c in range(CHUNKS):
        h_scr[pl.ds(c, tb, stride=CHUNKS), :] = h[:, c * LANES:(c + 1) * LANES]
    sub = lax.broadcasted_iota(jnp.int32, (CHUNKS, N_SEL), 0)
    lane = lax.broadcasted_iota(jnp.int32, (CHUNKS, N_SEL), 1)
    slot = lane - PAIR_STRIDE * (sub >= half).astype(jnp.int32)

    def token(t, _):
        ht = h_scr[pl.ds(pl.multiple_of(t * CHUNKS, CHUNKS), CHUNKS), :]
        hlo = jnp.concatenate([ht[0:half], ht[0:half]], axis=0)
        hhi = jnp.concatenate([ht[half:CHUNKS], ht[half:CHUNKS]], axis=0)
        acc = jnp.zeros((CHUNKS, N_SEL), F32)
        for a, slab in _slab_pairs(tab_ref, idx_ref, t):
            lo, hi = _unpack(slab)
            dot = jnp.sum(lo * hlo + hi * hhi, axis=-1, keepdims=True)
            acc = jnp.where(slot == a, dot, acc)
        act_scr[pl.ds(t, 1), :] = jnp.sum(acc, axis=0, keepdims=True)

    _token_loop(tb, token, 8)
    a = act_scr[...]
    w_ref[...] = gate_ref[...] * (0.5 * a * (1.0 + lax.erf(a * (2.0 ** -0.5))))


PAIR_STRIDE = N_SEL // 4


def _slab_pairs(tab_ref, idx_ref, t):
    views = [idx_ref.at[pl.ds(t * N_SEL + q * PAIR_STRIDE, PAIR_STRIDE)] for q in range(4)]
    for k in range(PAIR_STRIDE):
        rows = [pl.multiple_of(v[k], TABLE_ROWS_PER_EXPERT) for v in views]
        for q in (0, 2):
            yield q * PAIR_STRIDE + k, jnp.concatenate(
                [tab_ref[pl.ds(rows[q], TABLE_ROWS_PER_EXPERT), :],
                 tab_ref[pl.ds(rows[q + 1], TABLE_ROWS_PER_EXPERT), :]], axis=0)


def _peer_out_kernel(pk_ref, tab_ref, x_ref, g2_ref, fw_ref, o_ref, p_scr, *, tb, final):
    half = CHUNKS // 2
    upper = lax.broadcasted_iota(jnp.int32, (CHUNKS, LANES), 0) >= half

    def token(t, _):
        views = [pk_ref.at[pl.ds(t * N_SEL + q * PAIR_STRIDE, PAIR_STRIDE)] for q in range(4)]
        accs = [jnp.zeros((CHUNKS, LANES), F32) for _ in range(4)]
        for k in range(PAIR_STRIDE):
            words = [v[k] for v in views]
            for q in (0, 2):
                rows = [pl.multiple_of(words[q + i] & 0xFFFF, TABLE_ROWS_PER_EXPERT) for i in range(2)]
                slab = jnp.concatenate([tab_ref[pl.ds(r, TABLE_ROWS_PER_EXPERT), :] for r in rows], axis=0)
                lo, hi = _unpack(slab)
                both = jnp.where(upper, words[q + 1], words[q])
                wv = lax.bitcast_convert_type(both & jnp.int32(-65536), F32)
                accs[q] = accs[q] + wv * lo
                accs[q + 1] = accs[q + 1] + wv * hi
        alo = accs[0] + accs[2]
        ahi = accs[1] + accs[3]
        off = pl.multiple_of(t * CHUNKS, CHUNKS)
        p_scr[pl.ds(off, half), :] = alo[0:half] + alo[half:CHUNKS]
        p_scr[pl.ds(pl.multiple_of(off + half, half), half), :] = ahi[0:half] + ahi[half:CHUNKS]

    _token_loop(tb, token, 2)
    peer = jnp.concatenate([p_scr[pl.ds(c, tb, stride=CHUNKS), :] for c in range(CHUNKS)], axis=1)
    y = x_ref[...] + g2_ref[0] * peer
    if final:
        y = y * lax.rsqrt(jnp.mean(y * y, axis=-1, keepdims=True) + NORM_EPS) * fw_ref[...]
    o_ref[...] = y


def _peer_residual(x, nw, shift, scale, g2, wq, keys, u, v, final_w, final):
    b, t, d = x.shape
    assert d == CHUNKS * LANES
    idx, gate = _route(x, nw, shift, scale, wq, keys)
    n = b * t
    tb = TOK_BLOCK
    per_batch = t // tb
    x2 = x.reshape(n, d)
    idx1 = idx.reshape(n * N_SEL)
    smem_blk = pl.BlockSpec((tb * N_SEL,), lambda i: (i,), memory_space=pltpu.SMEM)
    table = pl.BlockSpec(u.shape, lambda i: (0, 0), pipeline_mode=pl.Buffered(1))
    rows = pl.BlockSpec((tb, d), lambda i: (i, 0))
    sel = pl.BlockSpec((tb, N_SEL), lambda i: (i, 0))
    vec = pl.BlockSpec((1, 1, d), lambda i: (i // per_batch, 0, 0))
    const = pl.BlockSpec((1, d), lambda i: (0, 0))
    w = pl.pallas_call(
        functools.partial(_peer_act_kernel, tb=tb),
        grid=(n // tb,),
        in_specs=[smem_blk, table, rows, const, vec, vec, sel],
        out_specs=sel,
        out_shape=jax.ShapeDtypeStruct((n, N_SEL), F32),
        scratch_shapes=[pltpu.VMEM((tb * CHUNKS, LANES), F32), pltpu.VMEM((tb, N_SEL), F32)],
        compiler_params=_cparams("arbitrary"),
        name="peer_act",
    )(idx1, u, x2, nw.reshape(1, d), shift, scale, gate.reshape(n, N_SEL))
    w_bits = lax.bitcast_convert_type(w.astype(BF16).astype(F32), jnp.int32) & jnp.int32(-65536)
    packed = (w_bits | idx.reshape(n, N_SEL)).reshape(n * N_SEL)
    y = pl.pallas_call(
        functools.partial(_peer_out_kernel, tb=tb, final=final),
        grid=(n // tb,),
        in_specs=[smem_blk, table, rows, vec, const],
        out_specs=rows,
        out_shape=jax.ShapeDtypeStruct((n, d), F32),
        scratch_shapes=[pltpu.VMEM((tb * CHUNKS, LANES), F32)],
        compiler_params=_cparams("arbitrary"),
        name="peer_out",
    )(packed, v, x2, g2, final_w.reshape(1, d))
    return y.reshape(b, t, d)


def _na_bias_table(rpb):
    cq = jnp.arange(GRID_W)
    coff = jnp.clip(cq[None, :] - cq[:, None] + (NA_WIN_C - 1), 0, 2 * NA_WIN_C - 2)
    c_start = jnp.clip(cq - NA_WIN_C // 2, 0, GRID_W - NA_WIN_C)
    ok = (cq[None, :] >= c_start[:, None]) & (cq[None, :] < c_start[:, None] + NA_WIN_C)
    roff = jnp.arange(NA_WIN_R)[:, None] + jnp.arange(NA_WIN_R)[None, :]
    bias = rpb.astype(F32)[:, :, coff][:, roff]
    bias = jnp.where(ok[None, None, None], bias, NEG_INF)
    bias = bias.transpose(0, 1, 3, 2, 4).reshape(HEAD_PAIRS, 2, NA_WIN_R, GRID_W, NA_WIN_R * GRID_W)
    return bias.transpose(0, 2, 1, 3, 4).reshape(HEAD_PAIRS, NA_WIN_R, 2 * GRID_W, NA_WIN_R * GRID_W)


def _rope_tables(s):
    t = jnp.arange(s)
    row = (t // GRID_W).astype(F32)
    col = (t % GRID_W).astype(F32)
    inv = ROPE_BASE ** (-jnp.arange(ROPE_PAIRS, dtype=F32) / ROPE_PAIRS)
    cr, sr = jnp.cos(row[:, None] * inv), jnp.sin(row[:, None] * inv)
    cc, sc = jnp.cos(col[:, None] * inv), jnp.sin(col[:, None] * inv)
    cos = jnp.concatenate([cr, cr, cc, cc], axis=-1)
    sin = jnp.concatenate([-sr, sr, -sc, sc], axis=-1)
    return jnp.tile(cos, (1, 2)), jnp.tile(sin, (1, 2))


def _block_diag(blocks):
    n, a, bb = blocks.shape
    eye = jnp.eye(n, dtype=blocks.dtype)
    return (eye[:, None, :, None] * blocks[:, :, None, :]).reshape(n * a, n * bb)


def kernel(x, c, ctx, c_ctx, norm1_w, norm2_w, w_ada, b_ada, w_in, w_out, na_rpb, ret_decay_fwd, ret_decay_bwd, ret_gn_w, pool_w, pool_scale, peer_wq, peer_keys, peer_u, peer_v, final_norm_w):
    b, s, d = x.shape
    depth = w_in.shape[0]
    clen = ctx.shape[1]
    cvec = jnp.concatenate([c, c_ctx[None, :], jnp.zeros((8 - b - 1, d), F32)], axis=0)
    mod = _modulation(cvec, w_ada, b_ada)
    cos_x, sin_x = _rope_tables(s)
    cos_c, sin_c = jnp.ones((clen, LANES), F32), jnp.zeros((clen, LANES), F32)
    avg = _block_diag(jnp.full((RET_HEADS, HEAD_DIM, HEAD_DIM), 1.0 / HEAD_DIM, F32))
    zero_state = jnp.zeros((b, HEAD_PAIRS, LANES, LANES), F32)

    for l in range(depth):
        last = l == depth - 1
        mx = mod[l, :b].reshape(b, 1, 6, d)
        mc = jnp.broadcast_to(mod[l, b].reshape(1, 1, 6, d), (b, 1, 6, d))
        sh1, sc1, g1, sh2, sc2, g2 = [mx[:, :, i] for i in range(6)]
        csh1, csc1, cg1, csh2, csc2, cg2 = [mc[:, :, i] for i in range(6)]
        lg = jnp.stack([jax.nn.log_sigmoid(ret_decay_fwd[l].astype(F32)),
                        jax.nn.log_sigmoid(ret_decay_bwd[l].astype(F32))], axis=0)
        wi = w_in[l].astype(BF16)
        wo = w_out[l].astype(BF16)
        wpool = _block_diag(pool_w[l]).astype(BF16)
        bias8 = _na_bias_table(na_rpb[l])

        pc, pcp = _in_proj(ctx, norm1_w[l], csh1, csc1, wi)
        ycf, ycb, r_f, r_b = _retention(pc, lg, cos_c, sin_c, zero_state, zero_state)

        px, pxp = _in_proj(x, norm1_w[l], sh1, sc1, wi)
        na = _na_attention(px, pc, bias8)
        yf, yb, _, _ = _retention(px, lg, cos_x, sin_x, r_f, r_b)
        pool = _pool(pxp, wpool, pool_scale[l])
        x = _out_proj(na, yf, yb, px, pool, x, g1, ret_gn_w[l], avg, wo)
        wq = jnp.stack(_split_bf16(peer_wq[l]))
        keys = jnp.stack(_split_bf16(peer_keys[l]))
        u_tab = _pack_table(peer_u, l)
        v_tab = _pack_table(peer_v, l)
        x = _peer_residual(x, norm2_w[l], sh2, sc2, g2, wq, keys, u_tab, v_tab, final_norm_w, last)

        if not last:
            na_c = _ctx_attention(pc)
            pool_c = _pool(pcp, wpool, pool_scale[l])
            ctx = _out_proj(na_c, ycf, ycb, pc, pool_c, ctx, cg1, ret_gn_w[l], avg, wo)
            ctx = _peer_residual(ctx, norm2_w[l], csh2, csc2, cg2, wq, keys, u_tab, v_tab, final_norm_w, False)
    return x
```

```python
import functools

import jax
import jax.numpy as jnp
from jax import lax
from jax.experimental import pallas as pl
from jax.experimental.pallas import tpu as pltpu

D_MODEL = 1024
GRID_W = 64
HEAD_DIM = 64
NA_HEADS = 6
NA_WIN_R = 8
NA_WIN_C = 16
RET_HEADS = 6
RET_CHUNK = 128
POOL_WINDOWS = (2, 4, 8, 16)
POOL_GROUP = 64
NA_WIDTH = NA_HEADS * HEAD_DIM
RET_WIDTH = RET_HEADS * HEAD_DIM
POOL_WIDTH = POOL_GROUP * len(POOL_WINDOWS)
O_RET_G = 3 * NA_WIDTH + 3 * RET_WIDTH
O_POOL = O_RET_G + RET_WIDTH
D_PROJ = O_POOL + POOL_WIDTH
ROPE_BASE = 10000.0
ROPE_PAIRS = HEAD_DIM // 4
PEER_HEADS = 8
PEER_NKEYS = 128
PEER_KEY_DIM = 128
PEER_TOPK = 16
N_SEL = PEER_HEADS * PEER_TOPK
NORM_EPS = 1e-6
NEG_INF = -1e30

LANES = 128
VMEM_LIMIT_BYTES = 56 * 1024 * 1024

F32 = jnp.float32
BF16 = jnp.bfloat16
HIGHEST = lax.Precision.HIGHEST
HEAD_PAIRS = NA_HEADS // 2
TOK_BLOCK = 128
CHUNKS = D_MODEL // LANES
TABLE_ROWS_PER_EXPERT = CHUNKS // 2


def _cparams(*sem):
    return pltpu.CompilerParams(dimension_semantics=sem, vmem_limit_bytes=VMEM_LIMIT_BYTES)


def _dot_nt(a, b, precision=None):
    return lax.dot_general(a, b, (((1,), (1,)), ((), ())), precision=precision,
                           preferred_element_type=F32)


def _split_bf16(a):
    hi = a.astype(BF16)
    return hi, (a - hi.astype(F32)).astype(BF16)


def _rms_mod(x, nw, shift, scale):
    y = x * lax.rsqrt(jnp.mean(x * x, axis=-1, keepdims=True) + NORM_EPS)
    return (y * nw) * (1.0 + scale) + shift


def _mod_kernel(c_ref, w_ref, b_ref, o_ref):
    c = c_ref[...]
    a = c * jax.nn.sigmoid(c)
    o_ref[0] = jnp.dot(a, w_ref[0], precision=HIGHEST, preferred_element_type=F32) + b_ref[0]


def _modulation(cvec, w_ada, b_ada):
    depth, d, n = w_ada.shape
    tn = 1536
    return pl.pallas_call(
        _mod_kernel,
        grid=(depth, n // tn),
        in_specs=[pl.BlockSpec((8, d), lambda l, j: (0, 0)),
                  pl.BlockSpec((1, d, tn), lambda l, j: (l, 0, j)),
                  pl.BlockSpec((1, 1, tn), lambda l, j: (l, 0, j))],
        out_specs=pl.BlockSpec((1, 8, tn), lambda l, j: (l, 0, j)),
        out_shape=jax.ShapeDtypeStruct((depth, 8, n), F32),
        compiler_params=_cparams("parallel", "parallel"),
        name="adaln_mod",
    )(cvec, w_ada, b_ada.reshape(depth, 1, n))


def _inproj_kernel(x_ref, nw_ref, sh_ref, sc_ref, w_ref, o_ref, p_ref):
    h = _rms_mod(x_ref[0], nw_ref[...], sh_ref[0], sc_ref[0])
    r = jnp.dot(h.astype(BF16), w_ref[...], preferred_element_type=F32)
    o_ref[0] = r[:, :O_POOL]
    p_ref[0] = r[:, O_POOL:]


def _in_proj(x, nw, shift, scale, w_bf16):
    b, t, d = x.shape
    tm = min(512, t)
    return pl.pallas_call(
        _inproj_kernel,
        grid=(b, t // tm),
        in_specs=[pl.BlockSpec((1, tm, d), lambda i, j: (i, j, 0)),
                  pl.BlockSpec((1, d), lambda i, j: (0, 0)),
                  pl.BlockSpec((1, 1, d), lambda i, j: (i, 0, 0)),
                  pl.BlockSpec((1, 1, d), lambda i, j: (i, 0, 0)),
                  pl.BlockSpec((d, D_PROJ), lambda i, j: (0, 0))],
        out_specs=[pl.BlockSpec((1, tm, O_POOL), lambda i, j: (i, j, 0)),
                   pl.BlockSpec((1, tm, POOL_WIDTH), lambda i, j: (i, j, 0))],
        out_shape=[jax.ShapeDtypeStruct((b, t, O_POOL), F32),
                   jax.ShapeDtypeStruct((b, t, POOL_WIDTH), F32)],
        compiler_params=_cparams("parallel", "parallel"),
        name="in_proj",
    )(x, nw.reshape(1, d), shift, scale, w_bf16)


def _softmax_pv(s_list, v_list):
    m = s_list[0].max(axis=-1, keepdims=True)
    for s in s_list[1:]:
        m = jnp.maximum(m, s.max(axis=-1, keepdims=True))
    num = None
    den = None
    for s, v in zip(s_list, v_list):
        p = jnp.exp(s - m)
        pv = jnp.dot(p.astype(BF16), v.astype(BF16), preferred_element_type=F32)
        ps = p.sum(axis=-1, keepdims=True)
        num = pv if num is None else num + pv
        den = ps if den is None else den + ps
    return num / den


NA_ROWS_PER_TRIP = 4


def _na_kernel(q_ref, k_ref, v_ref, kc_ref, vc_ref, bias_ref, o_ref, *, rows, rb):
    i = pl.program_id(2)
    lane = lax.broadcasted_iota(jnp.int32, (1, LANES), 1)
    first = lane < HEAD_DIM
    kc = kc_ref[0].astype(BF16)
    vc = vc_ref[0].astype(BF16)
    scale = HEAD_DIM ** -0.5
    nk = NA_WIN_R * GRID_W

    def body(it, carry):
        for u in range(NA_ROWS_PER_TRIP):
            one_row(it * NA_ROWS_PER_TRIP + u)
        return carry

    def one_row(rr):
        r = i * rb + rr
        rs = jnp.clip(r - NA_WIN_R // 2, 0, rows - NA_WIN_R)
        delta = rs - r + (NA_WIN_R - 1)
        q = q_ref[0, pl.ds(pl.multiple_of(rr * GRID_W, GRID_W), GRID_W), :]
        k = k_ref[0, pl.ds(pl.multiple_of(rs * GRID_W, GRID_W), nk), :].astype(BF16)
        v = v_ref[0, pl.ds(pl.multiple_of(rs * GRID_W, GRID_W), nk), :].astype(BF16)
        q2 = jnp.concatenate([jnp.where(first, q, 0.0), jnp.where(first, 0.0, q)], axis=0).astype(BF16)
        s = _dot_nt(q2, k) * scale + bias_ref[0, delta]
        sc = _dot_nt(q2, kc) * scale
        o2 = _softmax_pv([s, sc], [v, vc])
        o_ref[0, pl.ds(pl.multiple_of(rr * GRID_W, GRID_W), GRID_W), :] = jnp.where(
            first, o2[0:GRID_W], o2[GRID_W:2 * GRID_W])

    lax.fori_loop(0, rb // NA_ROWS_PER_TRIP, body, 0)


def _na_attention(px, pc, bias8):
    b, s, _ = px.shape
    c = pc.shape[1]
    rows = s // GRID_W
    assert rows >= NA_WIN_R and s % GRID_W == 0
    rb = min(8, rows)
    nq, nkb, nvb = 0, HEAD_PAIRS, 2 * HEAD_PAIRS
    return pl.pallas_call(
        functools.partial(_na_kernel, rows=rows, rb=rb),
        grid=(b, HEAD_PAIRS, rows // rb),
        in_specs=[pl.BlockSpec((1, rb * GRID_W, LANES), lambda bi, hp, i: (bi, i, nq + hp)),
                  pl.BlockSpec((1, s, LANES), lambda bi, hp, i: (bi, 0, nkb + hp)),
                  pl.BlockSpec((1, s, LANES), lambda bi, hp, i: (bi, 0, nvb + hp)),
                  pl.BlockSpec((1, c, LANES), lambda bi, hp, i: (bi, 0, nkb + hp)),
                  pl.BlockSpec((1, c, LANES), lambda bi, hp, i: (bi, 0, nvb + hp)),
                  pl.BlockSpec((1, NA_WIN_R, 2 * GRID_W, NA_WIN_R * GRID_W),
                               lambda bi, hp, i: (hp, 0, 0, 0))],
        out_specs=pl.BlockSpec((1, rb * GRID_W, LANES), lambda bi, hp, i: (bi, i, hp)),
        out_shape=jax.ShapeDtypeStruct((b, s, NA_WIDTH), F32),
        compiler_params=_cparams("parallel", "parallel", "arbitrary"),
        name="na_attention",
    )(px, px, px, pc, pc, bias8)


def _ctx_attn_kernel(q_ref, k_ref, v_ref, o_ref):
    lane = lax.broadcasted_iota(jnp.int32, (1, LANES), 1)
    first = lane < HEAD_DIM
    q = q_ref[0]
    k = k_ref[0]
    v = v_ref[0]
    outs = []
    for h in range(2):
        qm = jnp.where(first if h == 0 else jnp.logical_not(first), q, 0.0)
        s = _dot_nt(qm, k) * HEAD_DIM ** -0.5
        outs.append(_softmax_pv([s], [v]))
    o_ref[0] = jnp.where(first, outs[0], outs[1])


def _ctx_attention(pc):
    b, c, _ = pc.shape
    return pl.pallas_call(
        _ctx_attn_kernel,
        grid=(b, HEAD_PAIRS),
        in_specs=[pl.BlockSpec((1, c, LANES), lambda bi, hp: (bi, 0, hp)),
                  pl.BlockSpec((1, c, LANES), lambda bi, hp: (bi, 0, HEAD_PAIRS + hp)),
                  pl.BlockSpec((1, c, LANES), lambda bi, hp: (bi, 0, 2 * HEAD_PAIRS + hp))],
        out_specs=pl.BlockSpec((1, c, LANES), lambda bi, hp: (bi, 0, hp)),
        out_shape=jax.ShapeDtypeStruct((b, c, NA_WIDTH), F32),
        compiler_params=_cparams("parallel", "parallel"),
        name="ctx_attention",
    )(pc, pc, pc)


RET_CHUNKS_PER_STEP = 8


def _ret_kernel(lg_ref, qf_ref, kf_ref, vf_ref, qb_ref, kb_ref, vb_ref,
                cf_ref, sf_ref, cb_ref, sb_ref, rf0_ref, rb0_ref,
                yf_ref, yb_ref, rfo_ref, rbo_ref, rf_scr, rb_scr, *, cpb):
    hp = pl.program_id(1)
    c = pl.program_id(2)
    nc = pl.num_programs(2)
    cs = RET_CHUNK

    @pl.when(c == 0)
    def _():
        rf_scr[...] = rf0_ref[0, 0]
        rb_scr[...] = rb0_ref[0, 0]

    lane = lax.broadcasted_iota(jnp.int32, (1, LANES), 1)
    first = lane < HEAD_DIM
    low = (lane % (2 * ROPE_PAIRS)) < ROPE_PAIRS
    pos = lax.broadcasted_iota(jnp.int32, (cs, 1), 0).astype(F32)
    ii = lax.broadcasted_iota(jnp.int32, (cs, cs), 0)
    jj = lax.broadcasted_iota(jnp.int32, (cs, cs), 1)
    diff = (ii - jj).astype(F32)
    same_head = (ii < HEAD_DIM) == (jj < HEAD_DIM)
    scale = HEAD_DIM ** -0.5

    def rope(x, cos, sin):
        swapped = jnp.where(low, pltpu.roll(x, LANES - ROPE_PAIRS, 1), pltpu.roll(x, ROPE_PAIRS, 1))
        return x * cos + swapped * sin

    def direction(d, q_ref, k_ref, v_ref, cos_ref, sin_ref, r_scr, y_ref):
        lg0 = lg_ref[d, 2 * hp]
        lg1 = lg_ref[d, 2 * hp + 1]
        lgv = jnp.where(first, lg0, lg1)
        decs = []
        for lg in (lg0, lg1):
            if d == 0:
                decs.append(jnp.where(diff >= 0, jnp.exp(jnp.maximum(diff, 0.0) * lg), 0.0))
            else:
                decs.append(jnp.where(diff <= 0, jnp.exp(jnp.maximum(-diff, 0.0) * lg), 0.0))
        dec2 = jnp.concatenate(decs, axis=0)
        if d == 0:
            xi = jnp.exp((pos + 1.0) * lgv)
            zeta = jnp.exp((cs - 1.0 - pos) * lgv)
        else:
            xi = jnp.exp((cs - pos) * lgv)
            zeta = jnp.exp(pos * lgv)
        chunk_decay = jnp.exp(cs * lgv)
        r = r_scr[...]
        for j in (range(cpb) if d == 0 else reversed(range(cpb))):
            rows = pl.ds(j * cs, cs)
            cos = cos_ref[rows, :]
            sin = sin_ref[rows, :]
            q = rope(q_ref[0, rows, :], cos, sin)
            k = rope(k_ref[0, rows, :], cos, sin) * scale
            v = v_ref[0, rows, :].astype(BF16)
            q2 = jnp.concatenate([jnp.where(first, q, 0.0), jnp.where(first, 0.0, q)], axis=0).astype(BF16)
            s2 = _dot_nt(q2, k.astype(BF16)) * dec2
            o2 = jnp.dot(s2.astype(BF16), v, preferred_element_type=F32)
            inner = jnp.where(first, o2[0:cs], o2[cs:2 * cs])
            y_ref[0, rows, :] = inner + jnp.dot((q * xi).astype(BF16), r.astype(BF16),
                                                preferred_element_type=F32)
            kv = lax.dot_general((k * zeta).astype(BF16), v, (((0,), (0,)), ((), ())),
                                 preferred_element_type=F32)
            r = chunk_decay * r + jnp.where(same_head, kv, 0.0)
        r_scr[...] = r

    direction(0, qf_ref, kf_ref, vf_ref, cf_ref, sf_ref, rf_scr, yf_ref)
    direction(1, qb_ref, kb_ref, vb_ref, cb_ref, sb_ref, rb_scr, yb_ref)

    @pl.when(c == nc - 1)
    def _():
        rfo_ref[0, 0] = rf_scr[...]
        rbo_ref[0, 0] = rb_scr[...]


def _retention(p, lg, cos_t, sin_t, rf0, rb0):
    b, t, _ = p.shape
    cpb = min(RET_CHUNKS_PER_STEP, t // RET_CHUNK)
    nc = t // (RET_CHUNK * cpb)
    qo, ko, vo = 3 * HEAD_PAIRS, 4 * HEAD_PAIRS, 5 * HEAD_PAIRS
    blk = (1, RET_CHUNK * cpb, LANES)
    fwd = lambda o: pl.BlockSpec(blk, lambda bi, hp, c: (bi, c, o + hp))
    bwd = lambda o: pl.BlockSpec(blk, lambda bi, hp, c: (bi, nc - 1 - c, o + hp))
    tab_f = pl.BlockSpec((RET_CHUNK * cpb, LANES), lambda bi, hp, c: (c, 0))
    tab_b = pl.BlockSpec((RET_CHUNK * cpb, LANES), lambda bi, hp, c: (nc - 1 - c, 0))
    st = pl.BlockSpec((1, 1, LANES, LANES), lambda bi, hp, c: (bi, hp, 0, 0))
    return pl.pallas_call(
        functools.partial(_ret_kernel, cpb=cpb),
        grid=(b, HEAD_PAIRS, nc),
        in_specs=[pl.BlockSpec(memory_space=pltpu.SMEM),
                  fwd(qo), fwd(ko), fwd(vo), bwd(qo), bwd(ko), bwd(vo),
                  tab_f, tab_f, tab_b, tab_b, st, st],
        out_specs=[pl.BlockSpec(blk, lambda bi, hp, c: (bi, c, hp)),
                   pl.BlockSpec(blk, lambda bi, hp, c: (bi, nc - 1 - c, hp)),
                   st, st],
        out_shape=[jax.ShapeDtypeStruct((b, t, RET_WIDTH), F32),
                   jax.ShapeDtypeStruct((b, t, RET_WIDTH), F32),
                   jax.ShapeDtypeStruct((b, HEAD_PAIRS, LANES, LANES), F32),
                   jax.ShapeDtypeStruct((b, HEAD_PAIRS, LANES, LANES), F32)],
        scratch_shapes=[pltpu.VMEM((LANES, LANES), F32), pltpu.VMEM((LANES, LANES), F32)],
        compiler_params=_cparams("parallel", "parallel", "arbitrary"),
        name="retention",
    )(lg, p, p, p, p, p, p, cos_t, sin_t, cos_t, sin_t, rf0, rb0)


def _pool_kernel(prev_ref, cur_ref, next_ref, w_ref, s_ref, o_ref, scr, *, t_total, tp):
    i = pl.program_id(1)
    n = pl.num_programs(1)
    halo = POOL_WINDOWS[-1] // 2
    x = cur_ref[0]
    scr[0:halo, :] = jnp.where(i > 0, prev_ref[0], 0.0)
    scr[halo:halo + tp, :] = x
    scr[halo + tp:2 * halo + tp, :] = jnp.where(i < n - 1, next_ref[0], 0.0)
    t = i * tp + lax.broadcasted_iota(jnp.int32, (tp, 1), 0)
    lane = lax.broadcasted_iota(jnp.int32, (1, POOL_WIDTH), 1)

    def shifted(s):
        return scr[halo + s:halo + s + tp, :]

    acc = None
    mean = None
    done = 0
    for g, w in enumerate(POOL_WINDOWS):
        half = w // 2
        for s in list(range(-half, -done)) + list(range(done, half)):
            sh = x if s == 0 else shifted(s)
            acc = sh if acc is None else acc + sh
        done = half
        cnt = (jnp.minimum(t + half, t_total) - jnp.maximum(t - half, 0)).astype(F32)
        mg = acc / cnt
        mean = mg if mean is None else jnp.where(lane >= g * POOL_GROUP, mg, mean)
    dlt = (mean - x).astype(BF16)
    o_ref[0] = jnp.dot(dlt, w_ref[...], preferred_element_type=F32) * s_ref[...]


def _pool(pin, w_bd_bf16, scale):
    b, t, _ = pin.shape
    tp = min(1024, t)
    halo = POOL_WINDOWS[-1] // 2
    nh = tp // halo
    last = t // halo - 1
    return pl.pallas_call(
        functools.partial(_pool_kernel, t_total=t, tp=tp),
        grid=(b, t // tp),
        in_specs=[pl.BlockSpec((1, halo, POOL_WIDTH), lambda bi, i: (bi, jnp.maximum(i * nh - 1, 0), 0)),
                  pl.BlockSpec((1, tp, POOL_WIDTH), lambda bi, i: (bi, i, 0)),
                  pl.BlockSpec((1, halo, POOL_WIDTH), lambda bi, i: (bi, jnp.minimum((i + 1) * nh, last), 0)),
                  pl.BlockSpec((POOL_WIDTH, POOL_WIDTH), lambda bi, i: (0, 0)),
                  pl.BlockSpec((1, POOL_WIDTH), lambda bi, i: (0, 0))],
        out_specs=pl.BlockSpec((1, tp, POOL_WIDTH), lambda bi, i: (bi, i, 0)),
        out_shape=jax.ShapeDtypeStruct((b, t, POOL_WIDTH), F32),
        scratch_shapes=[pltpu.VMEM((tp + 2 * halo, POOL_WIDTH), F32)],
        compiler_params=_cparams("parallel", "parallel"),
        name="multiscale_pool",
    )(pin, pin, pin, w_bd_bf16, scale.reshape(1, POOL_WIDTH))


def _out_kernel(na_ref, yf_ref, yb_ref, g_ref, pool_ref, x_ref, g1_ref, gnw_ref, avg_ref, wo_ref, o_ref):
    y = yf_ref[0] + yb_ref[0]
    avg = avg_ref[...]

    def group_mean(a):
        hi = a.astype(BF16)
        rest = a - hi.astype(F32)
        mid = rest.astype(BF16)
        lo = (rest - mid.astype(F32)).astype(BF16)
        return (jnp.dot(hi, avg, preferred_element_type=F32) + jnp.dot(mid, avg, preferred_element_type=F32)
                + jnp.dot(lo, avg, preferred_element_type=F32))

    mu = group_mean(y)
    d = y - mu
    var = group_mean(d * d)
    yn = d * lax.rsqrt(var + NORM_EPS) * gnw_ref[...]
    g = g_ref[0]
    ret = yn * (g * jax.nn.sigmoid(g))
    mix = jnp.dot(na_ref[0].astype(BF16), wo_ref[0:NA_WIDTH, :], preferred_element_type=F32)
    mix += jnp.dot(ret.astype(BF16), wo_ref[NA_WIDTH:NA_WIDTH + RET_WIDTH, :], preferred_element_type=F32)
    mix += jnp.dot(pool_ref[0].astype(BF16), wo_ref[NA_WIDTH + RET_WIDTH:, :], preferred_element_type=F32)
    o_ref[0] = x_ref[0] + g1_ref[0] * mix


def _out_proj(na, yf, yb, p, pool, x, g1, gn_w, avg, wo_bf16):
    b, t, d = x.shape
    tm = min(512, t)
    row = lambda w: pl.BlockSpec((1, tm, w), lambda i, j: (i, j, 0))
    return pl.pallas_call(
        _out_kernel,
        grid=(b, t // tm),
        in_specs=[row(NA_WIDTH), row(RET_WIDTH), row(RET_WIDTH),
                  pl.BlockSpec((1, tm, RET_WIDTH), lambda i, j: (i, j, O_RET_G // RET_WIDTH)),
                  row(POOL_WIDTH), row(d),
                  pl.BlockSpec((1, 1, d), lambda i, j: (i, 0, 0)),
                  pl.BlockSpec((1, RET_WIDTH), lambda i, j: (0, 0)),
                  pl.BlockSpec((RET_WIDTH, RET_WIDTH), lambda i, j: (0, 0)),
                  pl.BlockSpec((d, d), lambda i, j: (0, 0))],
        out_specs=row(d),
        out_shape=jax.ShapeDtypeStruct((b, t, d), F32),
        compiler_params=_cparams("parallel", "parallel"),
        name="out_proj",
    )(na, yf, yb, p, pool, x, g1, gn_w.reshape(1, RET_WIDTH), avg, wo_bf16)


def _topk_rows(s, order=None, payload=None):
    n, m = s.shape
    if order is None:
        order = lax.broadcasted_iota(jnp.int32, (n, m), 0)
    krow = lax.broadcasted_iota(jnp.int32, (PEER_TOPK, m), 0)
    vals = jnp.zeros((PEER_TOPK, m), F32)
    idxs = jnp.zeros((PEER_TOPK, m), jnp.int32)
    for k in range(PEER_TOPK):
        mx = jnp.max(s, axis=0, keepdims=True)
        am = jnp.min(jnp.where(s == mx, order, jnp.iinfo(jnp.int32).max), axis=0, keepdims=True)
        sel = order == am
        pick = am if payload is None else jnp.sum(jnp.where(sel, payload, 0), axis=0, keepdims=True)
        vals = jnp.where(krow == k, mx, vals)
        idxs = jnp.where(krow == k, pick, idxs)
        s = jnp.where(sel, -jnp.inf, s)
    return vals, idxs


def _product_candidates(va, ia, vb, ib):
    k = PEER_TOPK
    sub = 8
    m = va.shape[1]
    row = lax.broadcasted_iota(jnp.int32, (sub, m), 0)
    cand, flat, eid = [], [], []

    def add(v, f, e, nvalid):
        cand.append(v if nvalid >= sub else jnp.where(row < nvalid, v, -jnp.inf))
        flat.append(f)
        eid.append(e)

    for i in range(sub):
        add(va[i:i + 1, :] + vb[0:sub, :], i * k + row, ia[i:i + 1, :] * PEER_NKEYS + ib[0:sub, :], k // (i + 1))
    add(va[0:1, :] + vb[sub:k, :], sub + row, ia[0:1, :] * PEER_NKEYS + ib[sub:k, :], sub)
    add(va[sub:k, :] + vb[0:1, :], (sub + row) * k, ia[sub:k, :] * PEER_NKEYS + ib[0:1, :], sub)
    return jnp.concatenate(cand, axis=0), jnp.concatenate(flat, axis=0), jnp.concatenate(eid, axis=0)


def _route_kernel(x_ref, nw_ref, sh_ref, sc_ref, wq_ref, keys_ref, idx_ref, gate_ref, q_scr, g_scr, i_scr, *, tm):
    h_hi, h_lo = _split_bf16(_rms_mod(x_ref[0], nw_ref[...], sh_ref[0], sc_ref[0]))
    q = (jnp.dot(h_hi, wq_ref[0], preferred_element_type=F32)
         + jnp.dot(h_hi, wq_ref[1], preferred_element_type=F32)
         + jnp.dot(h_lo, wq_ref[0], preferred_element_type=F32))
    for j in range(2 * PEER_HEADS):
        q_scr[j] = q[:, j * PEER_KEY_DIM:(j + 1) * PEER_KEY_DIM]

    def scores(half, hh):
        q_hi, q_lo = _split_bf16(q_scr[2 * hh + half])
        k_hi = keys_ref[0, half, hh]
        return _dot_nt(k_hi, q_hi) + _dot_nt(k_hi, q_lo) + _dot_nt(keys_ref[1, half, hh], q_hi)

    def head(hh, carry):
        sa = scores(0, hh)
        sb = scores(1, hh)
        va, ia = _topk_rows(sa)
        vb, ib = _topk_rows(sb)
        cand, flat, eid = _product_candidates(va, ia, vb, ib)
        sc, ei = _topk_rows(cand, order=flat, payload=eid)
        e = jnp.exp(sc - sc[0:1, :])
        off = pl.multiple_of(hh * PEER_TOPK, PEER_TOPK)
        g_scr[pl.ds(off, PEER_TOPK), :] = e / jnp.sum(e, axis=0, keepdims=True)
        i_scr[pl.ds(off, PEER_TOPK), :] = ei
        return carry

    def head_pair(i, carry):
        head(2 * i, carry)
        return head(2 * i + 1, carry)

    lax.fori_loop(0, PEER_HEADS // 2, head_pair, 0)
    gate_ref[0] = g_scr[...].T
    idx_ref[0] = i_scr[...].T * TABLE_ROWS_PER_EXPERT


def _route(x, nw, shift, scale, wq, keys):
    b, t, d = x.shape
    tm = min(1024, t)
    nt = t // tm
    nq = wq.shape[2]
    return pl.pallas_call(
        functools.partial(_route_kernel, tm=tm),
        grid=(b, nt),
        in_specs=[pl.BlockSpec((1, tm, d), lambda i, j: (i, j, 0)),
                  pl.BlockSpec((1, d), lambda i, j: (0, 0)),
                  pl.BlockSpec((1, 1, d), lambda i, j: (i, 0, 0)),
                  pl.BlockSpec((1, 1, d), lambda i, j: (i, 0, 0)),
                  pl.BlockSpec((2, d, nq), lambda i, j: (0, 0, 0)),
                  pl.BlockSpec((2, 2, PEER_HEADS, PEER_NKEYS, PEER_KEY_DIM), lambda i, j: (0, 0, 0, 0, 0))],
        out_specs=[pl.BlockSpec((1, tm, N_SEL), lambda i, j: (i, j, 0)),
                   pl.BlockSpec((1, tm, N_SEL), lambda i, j: (i, j, 0))],
        out_shape=[jax.ShapeDtypeStruct((b, t, N_SEL), jnp.int32),
                   jax.ShapeDtypeStruct((b, t, N_SEL), F32)],
        scratch_shapes=[pltpu.VMEM((2 * PEER_HEADS, tm, PEER_KEY_DIM), F32),
                        pltpu.VMEM((N_SEL, tm), F32),
                        pltpu.VMEM((N_SEL, tm), jnp.int32)],
        compiler_params=_cparams("parallel", "parallel"),
        name="peer_route",
    )(x, nw.reshape(1, d), shift, scale, wq, keys)


def _pack_kernel(t_ref, o_ref, *, te):
    x = t_ref[0]
    d = x.shape[1]

    def bf16_bits(a):
        return lax.bitcast_convert_type(a.astype(BF16).astype(F32), jnp.uint32)

    words = (bf16_bits(x[:, :d // 2]) >> 16) | bf16_bits(x[:, d // 2:])
    for q in range(TABLE_ROWS_PER_EXPERT):
        o_ref[pl.ds(q, te, stride=TABLE_ROWS_PER_EXPERT), :] = words[:, q * LANES:(q + 1) * LANES]


def _pack_table(tabs, layer):
    _, n, d = tabs.shape
    te = 512
    return pl.pallas_call(
        functools.partial(_pack_kernel, te=te),
        grid=(n // te,),
        in_specs=[pl.BlockSpec((1, te, d), lambda i: (layer, i, 0))],
        out_specs=pl.BlockSpec((te * TABLE_ROWS_PER_EXPERT, LANES), lambda i: (i, 0)),
        out_shape=jax.ShapeDtypeStruct((n * TABLE_ROWS_PER_EXPERT, LANES), jnp.uint32),
        compiler_params=_cparams("parallel"),
        name="pack_table",
    )(tabs)


def _token_loop(tb, token, per_trip):
    def trip(i, carry):
        for j in range(per_trip):
            token(i * per_trip + j, j)
        return carry

    lax.fori_loop(0, tb // per_trip, trip, 0)


def _unpack(slab):
    lo = lax.bitcast_convert_type(slab << 16, F32)
    hi = lax.bitcast_convert_type(slab & jnp.uint32(0xFFFF0000), F32)
    return lo, hi


def _peer_act_kernel(idx_ref, tab_ref, x_ref, nw_ref, sh_ref, sc_ref, gate_ref, w_ref, h_scr, act_scr, *, tb):
    h = _rms_mod(x_ref[...], nw_ref[...], sh_ref[0], sc_ref[0])
    half = CHUNKS // 2
    for c in range(CHUNKS):
        h_scr[pl.ds(c, tb, stride=CHUNKS), :] = h[:, c * LANES:(c + 1) * LANES]
    sub = lax.broadcasted_iota(jnp.int32, (CHUNKS, N_SEL), 0)
    lane = lax.broadcasted_iota(jnp.int32, (CHUNKS, N_SEL), 1)
    slot = lane - PAIR_STRIDE * (sub >= half).astype(jnp.int32)

    def token(t, _):
        ht = h_scr[pl.ds(pl.multiple_of(t * CHUNKS, CHUNKS), CHUNKS), :]
        hlo = jnp.concatenate([ht[0:half], ht[0:half]], axis=0)
        hhi = jnp.concatenate([ht[half:CHUNKS], ht[half:CHUNKS]], axis=0)
        acc = jnp.zeros((CHUNKS, N_SEL), F32)
        for a, slab in _slab_pairs(tab_ref, idx_ref, t):
            lo, hi = _unpack(slab)
            dot = jnp.sum(lo * hlo + hi * hhi, axis=-1, keepdims=True)
            acc = jnp.where(slot == a, dot, acc)
        act_scr[pl.ds(t, 1), :] = jnp.sum(acc, axis=0, keepdims=True)

    _token_loop(tb, token, 8)
    a = act_scr[...]
    w_ref[...] = gate_ref[...] * (0.5 * a * (1.0 + lax.erf(a * (2.0 ** -0.5))))


PAIR_STRIDE = N_SEL // 4


def _slab_pairs(tab_ref, idx_ref, t):
    views = [idx_ref.at[pl.ds(t * N_SEL + q * PAIR_STRIDE, PAIR_STRIDE)] for q in range(4)]
    for k in range(PAIR_STRIDE):
        rows = [pl.multiple_of(v[k], TABLE_ROWS_PER_EXPERT) for v in views]
        for q in (0, 2):
            yield q * PAIR_STRIDE + k, jnp.concatenate(
                [tab_ref[pl.ds(rows[q], TABLE_ROWS_PER_EXPERT), :],
                 tab_ref[pl.ds(rows[q + 1], TABLE_ROWS_PER_EXPERT), :]], axis=0)


def _peer_out_kernel(pk_ref, tab_ref, x_ref, g2_ref, fw_ref, o_ref, p_scr, *, tb, final):
    half = CHUNKS // 2
    upper = lax.broadcasted_iota(jnp.int32, (CHUNKS, LANES), 0) >= half

    def token(t, _):
        views = [pk_ref.at[pl.ds(t * N_SEL + q * PAIR_STRIDE, PAIR_STRIDE)] for q in range(4)]
        accs = [jnp.zeros((CHUNKS, LANES), F32) for _ in range(4)]
        for k in range(PAIR_STRIDE):
            words = [v[k] for v in views]
            for q in (0, 2):
                rows = [pl.multiple_of(words[q + i] & 0xFFFF, TABLE_ROWS_PER_EXPERT) for i in range(2)]
                slab = jnp.concatenate([tab_ref[pl.ds(r, TABLE_ROWS_PER_EXPERT), :] for r in rows], axis=0)
                lo, hi = _unpack(slab)
                both = jnp.where(upper, words[q + 1], words[q])
                wv = lax.bitcast_convert_type(both & jnp.int32(-65536), F32)
                accs[q] = accs[q] + wv * lo
                accs[q + 1] = accs[q + 1] + wv * hi
        alo = accs[0] + accs[2]
        ahi = accs[1] + accs[3]
        off = pl.multiple_of(t * CHUNKS, CHUNKS)
        p_scr[pl.ds(off, half), :] = alo[0:half] + alo[half:CHUNKS]
        p_scr[pl.ds(pl.multiple_of(off + half, half), half), :] = ahi[0:half] + ahi[half:CHUNKS]

    _token_loop(tb, token, 2)
    peer = jnp.concatenate([p_scr[pl.ds(c, tb, stride=CHUNKS), :] for c in range(CHUNKS)], axis=1)
    y = x_ref[...] + g2_ref[0] * peer
    if final:
        y = y * lax.rsqrt(jnp.mean(y * y, axis=-1, keepdims=True) + NORM_EPS) * fw_ref[...]
    o_ref[...] = y


def _peer_residual(x, nw, shift, scale, g2, wq, keys, u, v, final_w, final):
    b, t, d = x.shape
    assert d == CHUNKS * LANES
    idx, gate = _route(x, nw, shift, scale, wq, keys)
    n = b * t
    tb = TOK_BLOCK
    per_batch = t // tb
    x2 = x.reshape(n, d)
    idx1 = idx.reshape(n * N_SEL)
    smem_blk = pl.BlockSpec((tb * N_SEL,), lambda i: (i,), memory_space=pltpu.SMEM)
    table = pl.BlockSpec(u.shape, lambda i: (0, 0), pipeline_mode=pl.Buffered(1))
    rows = pl.BlockSpec((tb, d), lambda i: (i, 0))
    sel = pl.BlockSpec((tb, N_SEL), lambda i: (i, 0))
    vec = pl.BlockSpec((1, 1, d), lambda i: (i // per_batch, 0, 0))
    const = pl.BlockSpec((1, d), lambda i: (0, 0))
    w = pl.pallas_call(
        functools.partial(_peer_act_kernel, tb=tb),
        grid=(n // tb,),
        in_specs=[smem_blk, table, rows, const, vec, vec, sel],
        out_specs=sel,
        out_shape=jax.ShapeDtypeStruct((n, N_SEL), F32),
        scratch_shapes=[pltpu.VMEM((tb * CHUNKS, LANES), F32), pltpu.VMEM((tb, N_SEL), F32)],
        compiler_params=_cparams("arbitrary"),
        name="peer_act",
    )(idx1, u, x2, nw.reshape(1, d), shift, scale, gate.reshape(n, N_SEL))
    w_bits = lax.bitcast_convert_type(w.astype(BF16).astype(F32), jnp.int32) & jnp.int32(-65536)
    packed = (w_bits | idx.reshape(n, N_SEL)).reshape(n * N_SEL)
    y = pl.pallas_call(
        functools.partial(_peer_out_kernel, tb=tb, final=final),
        grid=(n // tb,),
        in_specs=[smem_blk, table, rows, vec, const],
        out_specs=rows,
        out_shape=jax.ShapeDtypeStruct((n, d), F32),
        scratch_shapes=[pltpu.VMEM((tb * CHUNKS, LANES), F32)],
        compiler_params=_cparams("arbitrary"),
        name="peer_out",
    )(packed, v, x2, g2, final_w.reshape(1, d))
    return y.reshape(b, t, d)


def _na_bias_table(rpb):
    cq = jnp.arange(GRID_W)
    coff = jnp.clip(cq[None, :] - cq[:, None] + (NA_WIN_C - 1), 0, 2 * NA_WIN_C - 2)
    c_start = jnp.clip(cq - NA_WIN_C // 2, 0, GRID_W - NA_WIN_C)
    ok = (cq[None, :] >= c_start[:, None]) & (cq[None, :] < c_start[:, None] + NA_WIN_C)
    roff = jnp.arange(NA_WIN_R)[:, None] + jnp.arange(NA_WIN_R)[None, :]
    bias = rpb.astype(F32)[:, :, coff][:, roff]
    bias = jnp.where(ok[None, None, None], bias, NEG_INF)
    bias = bias.transpose(0, 1, 3, 2, 4).reshape(HEAD_PAIRS, 2, NA_WIN_R, GRID_W, NA_WIN_R * GRID_W)
    return bias.transpose(0, 2, 1, 3, 4).reshape(HEAD_PAIRS, NA_WIN_R, 2 * GRID_W, NA_WIN_R * GRID_W)


def _rope_tables(s):
    t = jnp.arange(s)
    row = (t // GRID_W).astype(F32)
    col = (t % GRID_W).astype(F32)
    inv = ROPE_BASE ** (-jnp.arange(ROPE_PAIRS, dtype=F32) / ROPE_PAIRS)
    cr, sr = jnp.cos(row[:, None] * inv), jnp.sin(row[:, None] * inv)
    cc, sc = jnp.cos(col[:, None] * inv), jnp.sin(col[:, None] * inv)
    cos = jnp.concatenate([cr, cr, cc, cc], axis=-1)
    sin = jnp.concatenate([-sr, sr, -sc, sc], axis=-1)
    return jnp.tile(cos, (1, 2)), jnp.tile(sin, (1, 2))


def _block_diag(blocks):
    n, a, bb = blocks.shape
    eye = jnp.eye(n, dtype=blocks.dtype)
    return (eye[:, None, :, None] * blocks[:, :, None, :]).reshape(n * a, n * bb)


def kernel(x, c, ctx, c_ctx, norm1_w, norm2_w, w_ada, b_ada, w_in, w_out, na_rpb, ret_decay_fwd, ret_decay_bwd, ret_gn_w, pool_w, pool_scale, peer_wq, peer_keys, peer_u, peer_v, final_norm_w):
    b, s, d = x.shape
    depth = w_in.shape[0]
    clen = ctx.shape[1]
    cvec = jnp.concatenate([c, c_ctx[None, :], jnp.zeros((8 - b - 1, d), F32)], axis=0)
    mod = _modulation(cvec, w_ada, b_ada)
    cos_x, sin_x = _rope_tables(s)
    cos_c, sin_c = jnp.ones((clen, LANES), F32), jnp.zeros((clen, LANES), F32)
    avg = _block_diag(jnp.full((RET_HEADS, HEAD_DIM, HEAD_DIM), 1.0 / HEAD_DIM, BF16))
    zero_state = jnp.zeros((b, HEAD_PAIRS, LANES, LANES), F32)

    for l in range(depth):
        last = l == depth - 1
        mx = mod[l, :b].reshape(b, 1, 6, d)
        mc = jnp.broadcast_to(mod[l, b].reshape(1, 1, 6, d), (b, 1, 6, d))
        sh1, sc1, g1, sh2, sc2, g2 = [mx[:, :, i] for i in range(6)]
        csh1, csc1, cg1, csh2, csc2, cg2 = [mc[:, :, i] for i in range(6)]
        lg = jnp.stack([jax.nn.log_sigmoid(ret_decay_fwd[l].astype(F32)),
                        jax.nn.log_sigmoid(ret_decay_bwd[l].astype(F32))], axis=0)
        wi = w_in[l].astype(BF16)
        wo = w_out[l].astype(BF16)
        wpool = _block_diag(pool_w[l]).astype(BF16)
        bias8 = _na_bias_table(na_rpb[l])

        pc, pcp = _in_proj(ctx, norm1_w[l], csh1, csc1, wi)
        ycf, ycb, r_f, r_b = _retention(pc, lg, cos_c, sin_c, zero_state, zero_state)

        px, pxp = _in_proj(x, norm1_w[l], sh1, sc1, wi)
        na = _na_attention(px, pc, bias8)
        yf, yb, _, _ = _retention(px, lg, cos_x, sin_x, r_f, r_b)
        pool = _pool(pxp, wpool, pool_scale[l])
        x = _out_proj(na, yf, yb, px, pool, x, g1, ret_gn_w[l], avg, wo)
        wq = jnp.stack(_split_bf16(peer_wq[l]))
        keys = jnp.stack(_split_bf16(peer_keys[l]))
        u_tab = _pack_table(peer_u, l)
        v_tab = _pack_table(peer_v, l)
        x = _peer_residual(x, norm2_w[l], sh2, sc2, g2, wq, keys, u_tab, v_tab, final_norm_w, last)

        if not last:
            na_c = _ctx_attention(pc)
            pool_c = _pool(pcp, wpool, pool_scale[l])
            ctx = _out_proj(na_c, ycf, ycb, pc, pool_c, ctx, cg1, ret_gn_w[l], avg, wo)
            ctx = _peer_residual(ctx, norm2_w[l], csh2, csc2, cg2, wq, keys, u_tab, v_tab, final_norm_w, False)
    return x
```

```python
import functools

import jax
import jax.numpy as jnp
from jax import lax
from jax.experimental import pallas as pl
from jax.experimental.pallas import tpu as pltpu

D_MODEL = 1024
GRID_W = 64
HEAD_DIM = 64
NA_HEADS = 6
NA_WIN_R = 8
NA_WIN_C = 16
RET_HEADS = 6
RET_CHUNK = 128
POOL_WINDOWS = (2, 4, 8, 16)
POOL_GROUP = 64
NA_WIDTH = NA_HEADS * HEAD_DIM
RET_WIDTH = RET_HEADS * HEAD_DIM
POOL_WIDTH = POOL_GROUP * len(POOL_WINDOWS)
O_RET_G = 3 * NA_WIDTH + 3 * RET_WIDTH
O_POOL = O_RET_G + RET_WIDTH
D_PROJ = O_POOL + POOL_WIDTH
ROPE_BASE = 10000.0
ROPE_PAIRS = HEAD_DIM // 4
PEER_HEADS = 8
PEER_NKEYS = 128
PEER_KEY_DIM = 128
PEER_TOPK = 16
N_SEL = PEER_HEADS * PEER_TOPK
NORM_EPS = 1e-6
NEG_INF = -1e30

LANES = 128
VMEM_LIMIT_BYTES = 56 * 1024 * 1024

F32 = jnp.float32
BF16 = jnp.bfloat16
HIGHEST = lax.Precision.HIGHEST
HEAD_PAIRS = NA_HEADS // 2
TOK_BLOCK = 128
CHUNKS = D_MODEL // LANES
TABLE_ROWS_PER_EXPERT = CHUNKS // 2


def _cparams(*sem):
    return pltpu.CompilerParams(dimension_semantics=sem, vmem_limit_bytes=VMEM_LIMIT_BYTES)


def _dot_nt(a, b, precision=None):
    return lax.dot_general(a, b, (((1,), (1,)), ((), ())), precision=precision,
                           preferred_element_type=F32)


def _split_bf16(a):
    hi = a.astype(BF16)
    return hi, (a - hi.astype(F32)).astype(BF16)


def _rms_mod(x, nw, shift, scale):
    y = x * lax.rsqrt(jnp.mean(x * x, axis=-1, keepdims=True) + NORM_EPS)
    return (y * nw) * (1.0 + scale) + shift


def _mod_kernel(c_ref, w_ref, b_ref, o_ref):
    c = c_ref[...]
    a = c * jax.nn.sigmoid(c)
    o_ref[0] = jnp.dot(a, w_ref[0], precision=HIGHEST, preferred_element_type=F32) + b_ref[0]


def _modulation(cvec, w_ada, b_ada):
    depth, d, n = w_ada.shape
    tn = 1536
    return pl.pallas_call(
        _mod_kernel,
        grid=(depth, n // tn),
        in_specs=[pl.BlockSpec((8, d), lambda l, j: (0, 0)),
                  pl.BlockSpec((1, d, tn), lambda l, j: (l, 0, j)),
                  pl.BlockSpec((1, 1, tn), lambda l, j: (l, 0, j))],
        out_specs=pl.BlockSpec((1, 8, tn), lambda l, j: (l, 0, j)),
        out_shape=jax.ShapeDtypeStruct((depth, 8, n), F32),
        compiler_params=_cparams("parallel", "parallel"),
        name="adaln_mod",
    )(cvec, w_ada, b_ada.reshape(depth, 1, n))


def _inproj_kernel(x_ref, nw_ref, sh_ref, sc_ref, w_ref, o_ref, p_ref):
    h = _rms_mod(x_ref[0], nw_ref[...], sh_ref[0], sc_ref[0])
    r = jnp.dot(h.astype(BF16), w_ref[...], preferred_element_type=F32)
    o_ref[0] = r[:, :O_POOL]
    p_ref[0] = r[:, O_POOL:]


def _in_proj(x, nw, shift, scale, w_bf16):
    b, t, d = x.shape
    tm = min(512, t)
    return pl.pallas_call(
        _inproj_kernel,
        grid=(b, t // tm),
        in_specs=[pl.BlockSpec((1, tm, d), lambda i, j: (i, j, 0)),
                  pl.BlockSpec((1, d), lambda i, j: (0, 0)),
                  pl.BlockSpec((1, 1, d), lambda i, j: (i, 0, 0)),
                  pl.BlockSpec((1, 1, d), lambda i, j: (i, 0, 0)),
                  pl.BlockSpec((d, D_PROJ), lambda i, j: (0, 0))],
        out_specs=[pl.BlockSpec((1, tm, O_POOL), lambda i, j: (i, j, 0)),
                   pl.BlockSpec((1, tm, POOL_WIDTH), lambda i, j: (i, j, 0))],
        out_shape=[jax.ShapeDtypeStruct((b, t, O_POOL), F32),
                   jax.ShapeDtypeStruct((b, t, POOL_WIDTH), F32)],
        compiler_params=_cparams("parallel", "parallel"),
        name="in_proj",
    )(x, nw.reshape(1, d), shift, scale, w_bf16)


def _softmax_pv(s_list, v_list):
    m = s_list[0].max(axis=-1, keepdims=True)
    for s in s_list[1:]:
        m = jnp.maximum(m, s.max(axis=-1, keepdims=True))
    num = None
    den = None
    for s, v in zip(s_list, v_list):
        p = jnp.exp(s - m)
        pv = jnp.dot(p.astype(BF16), v.astype(BF16), preferred_element_type=F32)
        ps = p.sum(axis=-1, keepdims=True)
        num = pv if num is None else num + pv
        den = ps if den is None else den + ps
    return num / den


NA_ROWS_PER_TRIP = 4


def _na_kernel(q_ref, k_ref, v_ref, kc_ref, vc_ref, bias_ref, o_ref, *, rows, rb):
    i = pl.program_id(2)
    lane = lax.broadcasted_iota(jnp.int32, (1, LANES), 1)
    first = lane < HEAD_DIM
    kc = kc_ref[0].astype(BF16)
    vc = vc_ref[0].astype(BF16)
    scale = HEAD_DIM ** -0.5
    nk = NA_WIN_R * GRID_W

    def body(it, carry):
        for u in range(NA_ROWS_PER_TRIP):
            one_row(it * NA_ROWS_PER_TRIP + u)
        return carry

    def one_row(rr):
        r = i * rb + rr
        rs = jnp.clip(r - NA_WIN_R // 2, 0, rows - NA_WIN_R)
        delta = rs - r + (NA_WIN_R - 1)
        q = q_ref[0, pl.ds(pl.multiple_of(rr * GRID_W, GRID_W), GRID_W), :]
        k = k_ref[0, pl.ds(pl.multiple_of(rs * GRID_W, GRID_W), nk), :].astype(BF16)
        v = v_ref[0, pl.ds(pl.multiple_of(rs * GRID_W, GRID_W), nk), :].astype(BF16)
        q2 = jnp.concatenate([jnp.where(first, q, 0.0), jnp.where(first, 0.0, q)], axis=0).astype(BF16)
        s = _dot_nt(q2, k) * scale + bias_ref[0, delta]
        sc = _dot_nt(q2, kc) * scale
        o2 = _softmax_pv([s, sc], [v, vc])
        o_ref[0, pl.ds(pl.multiple_of(rr * GRID_W, GRID_W), GRID_W), :] = jnp.where(
            first, o2[0:GRID_W], o2[GRID_W:2 * GRID_W])

    lax.fori_loop(0, rb // NA_ROWS_PER_TRIP, body, 0)


def _na_attention(px, pc, bias8):
    b, s, _ = px.shape
    c = pc.shape[1]
    rows = s // GRID_W
    assert rows >= NA_WIN_R and s % GRID_W == 0
    rb = min(8, rows)
    nq, nkb, nvb = 0, HEAD_PAIRS, 2 * HEAD_PAIRS
    return pl.pallas_call(
        functools.partial(_na_kernel, rows=rows, rb=rb),
        grid=(b, HEAD_PAIRS, rows // rb),
        in_specs=[pl.BlockSpec((1, rb * GRID_W, LANES), lambda bi, hp, i: (bi, i, nq + hp)),
                  pl.BlockSpec((1, s, LANES), lambda bi, hp, i: (bi, 0, nkb + hp)),
                  pl.BlockSpec((1, s, LANES), lambda bi, hp, i: (bi, 0, nvb + hp)),
                  pl.BlockSpec((1, c, LANES), lambda bi, hp, i: (bi, 0, nkb + hp)),
                  pl.BlockSpec((1, c, LANES), lambda bi, hp, i: (bi, 0, nvb + hp)),
                  pl.BlockSpec((1, NA_WIN_R, 2 * GRID_W, NA_WIN_R * GRID_W),
                               lambda bi, hp, i: (hp, 0, 0, 0))],
        out_specs=pl.BlockSpec((1, rb * GRID_W, LANES), lambda bi, hp, i: (bi, i, hp)),
        out_shape=jax.ShapeDtypeStruct((b, s, NA_WIDTH), F32),
        compiler_params=_cparams("parallel", "parallel", "arbitrary"),
        name="na_attention",
    )(px, px, px, pc, pc, bias8)


def _ctx_attn_kernel(q_ref, k_ref, v_ref, o_ref):
    lane = lax.broadcasted_iota(jnp.int32, (1, LANES), 1)
    first = lane < HEAD_DIM
    q = q_ref[0]
    k = k_ref[0]
    v = v_ref[0]
    outs = []
    for h in range(2):
        qm = jnp.where(first if h == 0 else jnp.logical_not(first), q, 0.0)
        s = _dot_nt(qm, k) * HEAD_DIM ** -0.5
        outs.append(_softmax_pv([s], [v]))
    o_ref[0] = jnp.where(first, outs[0], outs[1])


def _ctx_attention(pc):
    b, c, _ = pc.shape
    return pl.pallas_call(
        _ctx_attn_kernel,
        grid=(b, HEAD_PAIRS),
        in_specs=[pl.BlockSpec((1, c, LANES), lambda bi, hp: (bi, 0, hp)),
                  pl.BlockSpec((1, c, LANES), lambda bi, hp: (bi, 0, HEAD_PAIRS + hp)),
                  pl.BlockSpec((1, c, LANES), lambda bi, hp: (bi, 0, 2 * HEAD_PAIRS + hp))],
        out_specs=pl.BlockSpec((1, c, LANES), lambda bi, hp: (bi, 0, hp)),
        out_shape=jax.ShapeDtypeStruct((b, c, NA_WIDTH), F32),
        compiler_params=_cparams("parallel", "parallel"),
        name="ctx_attention",
    )(pc, pc, pc)


RET_CHUNKS_PER_STEP = 8


def _ret_kernel(lg_ref, qf_ref, kf_ref, vf_ref, qb_ref, kb_ref, vb_ref,
                cf_ref, sf_ref, cb_ref, sb_ref, rf0_ref, rb0_ref,
                yf_ref, yb_ref, rfo_ref, rbo_ref, rf_scr, rb_scr, *, cpb):
    hp = pl.program_id(1)
    c = pl.program_id(2)
    nc = pl.num_programs(2)
    cs = RET_CHUNK

    @pl.when(c == 0)
    def _():
        rf_scr[...] = rf0_ref[0, 0]
        rb_scr[...] = rb0_ref[0, 0]

    lane = lax.broadcasted_iota(jnp.int32, (1, LANES), 1)
    first = lane < HEAD_DIM
    low = (lane % (2 * ROPE_PAIRS)) < ROPE_PAIRS
    pos = lax.broadcasted_iota(jnp.int32, (cs, 1), 0).astype(F32)
    ii = lax.broadcasted_iota(jnp.int32, (cs, cs), 0)
    jj = lax.broadcasted_iota(jnp.int32, (cs, cs), 1)
    diff = (ii - jj).astype(F32)
    same_head = (ii < HEAD_DIM) == (jj < HEAD_DIM)
    scale = HEAD_DIM ** -0.5

    def rope(x, cos, sin):
        swapped = jnp.where(low, pltpu.roll(x, LANES - ROPE_PAIRS, 1), pltpu.roll(x, ROPE_PAIRS, 1))
        return x * cos + swapped * sin

    def direction(d, q_ref, k_ref, v_ref, cos_ref, sin_ref, r_scr, y_ref):
        lg0 = lg_ref[d, 2 * hp]
        lg1 = lg_ref[d, 2 * hp + 1]
        lgv = jnp.where(first, lg0, lg1)
        decs = []
        for lg in (lg0, lg1):
            if d == 0:
                decs.append(jnp.where(diff >= 0, jnp.exp(jnp.maximum(diff, 0.0) * lg), 0.0))
            else:
                decs.append(jnp.where(diff <= 0, jnp.exp(jnp.maximum(-diff, 0.0) * lg), 0.0))
        dec2 = jnp.concatenate(decs, axis=0)
        if d == 0:
            xi = jnp.exp((pos + 1.0) * lgv)
            zeta = jnp.exp((cs - 1.0 - pos) * lgv)
        else:
            xi = jnp.exp((cs - pos) * lgv)
            zeta = jnp.exp(pos * lgv)
        chunk_decay = jnp.exp(cs * lgv)
        r = r_scr[...]
        for j in (range(cpb) if d == 0 else reversed(range(cpb))):
            rows = pl.ds(j * cs, cs)
            cos = cos_ref[rows, :]
            sin = sin_ref[rows, :]
            q = rope(q_ref[0, rows, :], cos, sin)
            k = rope(k_ref[0, rows, :], cos, sin) * scale
            v = v_ref[0, rows, :].astype(BF16)
            q2 = jnp.concatenate([jnp.where(first, q, 0.0), jnp.where(first, 0.0, q)], axis=0).astype(BF16)
            s2 = _dot_nt(q2, k.astype(BF16)) * dec2
            o2 = jnp.dot(s2.astype(BF16), v, preferred_element_type=F32)
            inner = jnp.where(first, o2[0:cs], o2[cs:2 * cs])
            y_ref[0, rows, :] = inner + jnp.dot((q * xi).astype(BF16), r.astype(BF16),
                                                preferred_element_type=F32)
            kv = lax.dot_general((k * zeta).astype(BF16), v, (((0,), (0,)), ((), ())),
                                 preferred_element_type=F32)
            r = chunk_decay * r + jnp.where(same_head, kv, 0.0)
        r_scr[...] = r

    direction(0, qf_ref, kf_ref, vf_ref, cf_ref, sf_ref, rf_scr, yf_ref)
    direction(1, qb_ref, kb_ref, vb_ref, cb_ref, sb_ref, rb_scr, yb_ref)

    @pl.when(c == nc - 1)
    def _():
        rfo_ref[0, 0] = rf_scr[...]
        rbo_ref[0, 0] = rb_scr[...]


def _retention(p, lg, cos_t, sin_t, rf0, rb0):
    b, t, _ = p.shape
    cpb = min(RET_CHUNKS_PER_STEP, t // RET_CHUNK)
    nc = t // (RET_CHUNK * cpb)
    qo, ko, vo = 3 * HEAD_PAIRS, 4 * HEAD_PAIRS, 5 * HEAD_PAIRS
    blk = (1, RET_CHUNK * cpb, LANES)
    fwd = lambda o: pl.BlockSpec(blk, lambda bi, hp, c: (bi, c, o + hp))
    bwd = lambda o: pl.BlockSpec(blk, lambda bi, hp, c: (bi, nc - 1 - c, o + hp))
    tab_f = pl.BlockSpec((RET_CHUNK * cpb, LANES), lambda bi, hp, c: (c, 0))
    tab_b = pl.BlockSpec((RET_CHUNK * cpb, LANES), lambda bi, hp, c: (nc - 1 - c, 0))
    st = pl.BlockSpec((1, 1, LANES, LANES), lambda bi, hp, c: (bi, hp, 0, 0))
    return pl.pallas_call(
        functools.partial(_ret_kernel, cpb=cpb),
        grid=(b, HEAD_PAIRS, nc),
        in_specs=[pl.BlockSpec(memory_space=pltpu.SMEM),
                  fwd(qo), fwd(ko), fwd(vo), bwd(qo), bwd(ko), bwd(vo),
                  tab_f, tab_f, tab_b, tab_b, st, st],
        out_specs=[pl.BlockSpec(blk, lambda bi, hp, c: (bi, c, hp)),
                   pl.BlockSpec(blk, lambda bi, hp, c: (bi, nc - 1 - c, hp)),
                   st, st],
        out_shape=[jax.ShapeDtypeStruct((b, t, RET_WIDTH), F32),
                   jax.ShapeDtypeStruct((b, t, RET_WIDTH), F32),
                   jax.ShapeDtypeStruct((b, HEAD_PAIRS, LANES, LANES), F32),
                   jax.ShapeDtypeStruct((b, HEAD_PAIRS, LANES, LANES), F32)],
        scratch_shapes=[pltpu.VMEM((LANES, LANES), F32), pltpu.VMEM((LANES, LANES), F32)],
        compiler_params=_cparams("parallel", "parallel", "arbitrary"),
        name="retention",
    )(lg, p, p, p, p, p, p, cos_t, sin_t, cos_t, sin_t, rf0, rb0)


def _pool_kernel(prev_ref, cur_ref, next_ref, w_ref, s_ref, o_ref, scr, *, t_total, tp):
    i = pl.program_id(1)
    n = pl.num_programs(1)
    halo = POOL_WINDOWS[-1] // 2
    x = cur_ref[0]
    scr[0:halo, :] = jnp.where(i > 0, prev_ref[0], 0.0)
    scr[halo:halo + tp, :] = x
    scr[halo + tp:2 * halo + tp, :] = jnp.where(i < n - 1, next_ref[0], 0.0)
    t = i * tp + lax.broadcasted_iota(jnp.int32, (tp, 1), 0)
    lane = lax.broadcasted_iota(jnp.int32, (1, POOL_WIDTH), 1)

    def shifted(s):
        return scr[halo + s:halo + s + tp, :]

    acc = None
    mean = None
    done = 0
    for g, w in enumerate(POOL_WINDOWS):
        half = w // 2
        for s in list(range(-half, -done)) + list(range(done, half)):
            sh = x if s == 0 else shifted(s)
            acc = sh if acc is None else acc + sh
        done = half
        cnt = (jnp.minimum(t + half, t_total) - jnp.maximum(t - half, 0)).astype(F32)
        mg = acc / cnt
        mean = mg if mean is None else jnp.where(lane >= g * POOL_GROUP, mg, mean)
    dlt = (mean - x).astype(BF16)
    o_ref[0] = jnp.dot(dlt, w_ref[...], preferred_element_type=F32) * s_ref[...]


def _pool(pin, w_bd_bf16, scale):
    b, t, _ = pin.shape
    tp = min(1024, t)
    halo = POOL_WINDOWS[-1] // 2
    nh = tp // halo
    last = t // halo - 1
    return pl.pallas_call(
        functools.partial(_pool_kernel, t_total=t, tp=tp),
        grid=(b, t // tp),
        in_specs=[pl.BlockSpec((1, halo, POOL_WIDTH), lambda bi, i: (bi, jnp.maximum(i * nh - 1, 0), 0)),
                  pl.BlockSpec((1, tp, POOL_WIDTH), lambda bi, i: (bi, i, 0)),
                  pl.BlockSpec((1, halo, POOL_WIDTH), lambda bi, i: (bi, jnp.minimum((i + 1) * nh, last), 0)),
                  pl.BlockSpec((POOL_WIDTH, POOL_WIDTH), lambda bi, i: (0, 0)),
                  pl.BlockSpec((1, POOL_WIDTH), lambda bi, i: (0, 0))],
        out_specs=pl.BlockSpec((1, tp, POOL_WIDTH), lambda bi, i: (bi, i, 0)),
        out_shape=jax.ShapeDtypeStruct((b, t, POOL_WIDTH), F32),
        scratch_shapes=[pltpu.VMEM((tp + 2 * halo, POOL_WIDTH), F32)],
        compiler_params=_cparams("parallel", "parallel"),
        name="multiscale_pool",
    )(pin, pin, pin, w_bd_bf16, scale.reshape(1, POOL_WIDTH))


def _out_kernel(na_ref, yf_ref, yb_ref, g_ref, pool_ref, x_ref, g1_ref, gnw_ref, avg_ref, wo_ref, o_ref):
    y = yf_ref[0] + yb_ref[0]
    avg = avg_ref[...]

    def group_mean(a):
        hi = a.astype(BF16)
        rest = a - hi.astype(F32)
        mid = rest.astype(BF16)
        lo = (rest - mid.astype(F32)).astype(BF16)
        return (jnp.dot(hi, avg, preferred_element_type=F32) + jnp.dot(mid, avg, preferred_element_type=F32)
                + jnp.dot(lo, avg, preferred_element_type=F32))

    mu = group_mean(y)
    d = y - mu
    var = group_mean(d * d)
    yn = d * lax.rsqrt(var + NORM_EPS) * gnw_ref[...]
    g = g_ref[0]
    ret = yn * (g * jax.nn.sigmoid(g))
    mix = jnp.dot(na_ref[0].astype(BF16), wo_ref[0:NA_WIDTH, :], preferred_element_type=F32)
    mix += jnp.dot(ret.astype(BF16), wo_ref[NA_WIDTH:NA_WIDTH + RET_WIDTH, :], preferred_element_type=F32)
    mix += jnp.dot(pool_ref[0].astype(BF16), wo_ref[NA_WIDTH + RET_WIDTH:, :], preferred_element_type=F32)
    o_ref[0] = x_ref[0] + g1_ref[0] * mix


def _out_proj(na, yf, yb, p, pool, x, g1, gn_w, avg, wo_bf16):
    b, t, d = x.shape
    tm = min(512, t)
    row = lambda w: pl.BlockSpec((1, tm, w), lambda i, j: (i, j, 0))
    return pl.pallas_call(
        _out_kernel,
        grid=(b, t // tm),
        in_specs=[row(NA_WIDTH), row(RET_WIDTH), row(RET_WIDTH),
                  pl.BlockSpec((1, tm, RET_WIDTH), lambda i, j: (i, j, O_RET_G // RET_WIDTH)),
                  row(POOL_WIDTH), row(d),
                  pl.BlockSpec((1, 1, d), lambda i, j: (i, 0, 0)),
                  pl.BlockSpec((1, RET_WIDTH), lambda i, j: (0, 0)),
                  pl.BlockSpec((RET_WIDTH, RET_WIDTH), lambda i, j: (0, 0)),
                  pl.BlockSpec((d, d), lambda i, j: (0, 0))],
        out_specs=row(d),
        out_shape=jax.ShapeDtypeStruct((b, t, d), F32),
        compiler_params=_cparams("parallel", "parallel"),
        name="out_proj",
    )(na, yf, yb, p, pool, x, g1, gn_w.reshape(1, RET_WIDTH), avg, wo_bf16)


def _topk_rows(s, order=None, payload=None):
    n, m = s.shape
    if order is None:
        order = lax.broadcasted_iota(jnp.int32, (n, m), 0)
    krow = lax.broadcasted_iota(jnp.int32, (PEER_TOPK, m), 0)
    vals = jnp.zeros((PEER_TOPK, m), F32)
    idxs = jnp.zeros((PEER_TOPK, m), jnp.int32)
    for k in range(PEER_TOPK):
        mx = jnp.max(s, axis=0, keepdims=True)
        am = jnp.min(jnp.where(s == mx, order, jnp.iinfo(jnp.int32).max), axis=0, keepdims=True)
        sel = order == am
        pick = am if payload is None else jnp.sum(jnp.where(sel, payload, 0), axis=0, keepdims=True)
        vals = jnp.where(krow == k, mx, vals)
        idxs = jnp.where(krow == k, pick, idxs)
        s = jnp.where(sel, -jnp.inf, s)
    return vals, idxs


def _product_candidates(va, ia, vb, ib):
    k = PEER_TOPK
    sub = 8
    m = va.shape[1]
    row = lax.broadcasted_iota(jnp.int32, (sub, m), 0)
    cand, flat, eid = [], [], []

    def add(v, f, e, nvalid):
        cand.append(v if nvalid >= sub else jnp.where(row < nvalid, v, -jnp.inf))
        flat.append(f)
        eid.append(e)

    for i in range(sub):
        add(va[i:i + 1, :] + vb[0:sub, :], i * k + row, ia[i:i + 1, :] * PEER_NKEYS + ib[0:sub, :], k // (i + 1))
    add(va[0:1, :] + vb[sub:k, :], sub + row, ia[0:1, :] * PEER_NKEYS + ib[sub:k, :], sub)
    add(va[sub:k, :] + vb[0:1, :], (sub + row) * k, ia[sub:k, :] * PEER_NKEYS + ib[0:1, :], sub)
    return jnp.concatenate(cand, axis=0), jnp.concatenate(flat, axis=0), jnp.concatenate(eid, axis=0)


def _route_kernel(x_ref, nw_ref, sh_ref, sc_ref, wq_ref, keys_ref, idx_ref, gate_ref, q_scr, g_scr, i_scr, *, tm):
    h_hi, h_lo = _split_bf16(_rms_mod(x_ref[0], nw_ref[...], sh_ref[0], sc_ref[0]))
    q = (jnp.dot(h_hi, wq_ref[0], preferred_element_type=F32)
         + jnp.dot(h_hi, wq_ref[1], preferred_element_type=F32)
         + jnp.dot(h_lo, wq_ref[0], preferred_element_type=F32))
    for j in range(2 * PEER_HEADS):
        q_scr[j] = q[:, j * PEER_KEY_DIM:(j + 1) * PEER_KEY_DIM]

    def scores(half, hh):
        q_hi, q_lo = _split_bf16(q_scr[2 * hh + half])
        k_hi = keys_ref[0, half, hh]
        return _dot_nt(k_hi, q_hi) + _dot_nt(k_hi, q_lo) + _dot_nt(keys_ref[1, half, hh], q_hi)

    def head(hh, carry):
        sa = scores(0, hh)
        sb = scores(1, hh)
        va, ia = _topk_rows(sa)
        vb, ib = _topk_rows(sb)
        cand, flat, eid = _product_candidates(va, ia, vb, ib)
        sc, ei = _topk_rows(cand, order=flat, payload=eid)
        e = jnp.exp(sc - sc[0:1, :])
        off = pl.multiple_of(hh * PEER_TOPK, PEER_TOPK)
        g_scr[pl.ds(off, PEER_TOPK), :] = e / jnp.sum(e, axis=0, keepdims=True)
        i_scr[pl.ds(off, PEER_TOPK), :] = ei
        return carry

    def head_pair(i, carry):
        head(2 * i, carry)
        return head(2 * i + 1, carry)

    lax.fori_loop(0, PEER_HEADS // 2, head_pair, 0)
    gate_ref[0] = g_scr[...].T
    idx_ref[0] = i_scr[...].T * TABLE_ROWS_PER_EXPERT


def _route(x, nw, shift, scale, wq, keys):
    b, t, d = x.shape
    tm = min(1024, t)
    nt = t // tm
    nq = wq.shape[2]
    return pl.pallas_call(
        functools.partial(_route_kernel, tm=tm),
        grid=(b, nt),
        in_specs=[pl.BlockSpec((1, tm, d), lambda i, j: (i, j, 0)),
                  pl.BlockSpec((1, d), lambda i, j: (0, 0)),
                  pl.BlockSpec((1, 1, d), lambda i, j: (i, 0, 0)),
                  pl.BlockSpec((1, 1, d), lambda i, j: (i, 0, 0)),
                  pl.BlockSpec((2, d, nq), lambda i, j: (0, 0, 0)),
                  pl.BlockSpec((2, 2, PEER_HEADS, PEER_NKEYS, PEER_KEY_DIM), lambda i, j: (0, 0, 0, 0, 0))],
        out_specs=[pl.BlockSpec((1, tm, N_SEL), lambda i, j: (i, j, 0)),
                   pl.BlockSpec((1, tm, N_SEL), lambda i, j: (i, j, 0))],
        out_shape=[jax.ShapeDtypeStruct((b, t, N_SEL), jnp.int32),
                   jax.ShapeDtypeStruct((b, t, N_SEL), F32)],
        scratch_shapes=[pltpu.VMEM((2 * PEER_HEADS, tm, PEER_KEY_DIM), F32),
                        pltpu.VMEM((N_SEL, tm), F32),
                        pltpu.VMEM((N_SEL, tm), jnp.int32)],
        compiler_params=_cparams("parallel", "parallel"),
        name="peer_route",
    )(x, nw.reshape(1, d), shift, scale, wq, keys)


def _pack_kernel(t_ref, o_ref, *, te):
    x = t_ref[0]
    d = x.shape[1]

    def bf16_bits(a):
        return lax.bitcast_convert_type(a.astype(BF16).astype(F32), jnp.uint32)

    words = (bf16_bits(x[:, :d // 2]) >> 16) | bf16_bits(x[:, d // 2:])
    for q in range(TABLE_ROWS_PER_EXPERT):
        o_ref[pl.ds(q, te, stride=TABLE_ROWS_PER_EXPERT), :] = words[:, q * LANES:(q + 1) * LANES]


def _pack_table(tabs, layer):
    _, n, d = tabs.shape
    te = 512
    return pl.pallas_call(
        functools.partial(_pack_kernel, te=te),
        grid=(n // te,),
        in_specs=[pl.BlockSpec((1, te, d), lambda i: (layer, i, 0))],
        out_specs=pl.BlockSpec((te * TABLE_ROWS_PER_EXPERT, LANES), lambda i: (i, 0)),
        out_shape=jax.ShapeDtypeStruct((n * TABLE_ROWS_PER_EXPERT, LANES), jnp.uint32),
        compiler_params=_cparams("parallel"),
        name="pack_table",
    )(tabs)


def _token_loop(tb, token, per_trip):
    def trip(i, carry):
        for j in range(per_trip):
            token(i * per_trip + j, j)
        return carry

    lax.fori_loop(0, tb // per_trip, trip, 0)


def _unpack(slab):
    lo = lax.bitcast_convert_type(slab << 16, F32)
    hi = lax.bitcast_convert_type(slab & jnp.uint32(0xFFFF0000), F32)
    return lo, hi


def _peer_act_kernel(idx_ref, tab_ref, x_ref, nw_ref, sh_ref, sc_ref, gate_ref, w_ref, h_scr, act_scr, *, tb):
    h = _rms_mod(x_ref[...], nw_ref[...], sh_ref[0], sc_ref[0])
    half = CHUNKS // 2
    for c in range(CHUNKS):
        h_scr[pl.ds(c, tb, stride=CHUNKS), :] = h[:, c * LANES:(c + 1) * LANES]
    sub = lax.broadcasted_iota(jnp.int32, (CHUNKS, N_SEL), 0)
    lane = lax.broadcasted_iota(jnp.int32, (CHUNKS, N_SEL), 1)
    slot = lane - PAIR_STRIDE * (sub >= half).astype(jnp.int32)

    def token(t, _):
        ht = h_scr[pl.ds(pl.multiple_of(t * CHUNKS, CHUNKS), CHUNKS), :]
        hlo = jnp.concatenate([ht[0:half], ht[0:half]], axis=0)
        hhi = jnp.concatenate([ht[half:CHUNKS], ht[half:CHUNKS]], axis=0)
        acc = jnp.zeros((CHUNKS, N_SEL), F32)
        for a, slab in _slab_pairs(tab_ref, idx_ref, t):
            lo, hi = _unpack(slab)
            dot = jnp.sum(lo * hlo + hi * hhi, axis=-1, keepdims=True)
            acc = jnp.where(slot == a, dot, acc)
        act_scr[pl.ds(t, 1), :] = jnp.sum(acc, axis=0, keepdims=True)

    _token_loop(tb, token, 16)
    a = act_scr[...]
    w_ref[...] = gate_ref[...] * (0.5 * a * (1.0 + lax.erf(a * (2.0 ** -0.5))))


PAIR_STRIDE = N_SEL // 4


def _slab_pairs(tab_ref, idx_ref, t):
    views = [idx_ref.at[pl.ds(t * N_SEL + q * PAIR_STRIDE, PAIR_STRIDE)] for q in range(4)]
    for k in range(PAIR_STRIDE):
        rows = [pl.multiple_of(v[k], TABLE_ROWS_PER_EXPERT) for v in views]
        for q in (0, 2):
            yield q * PAIR_STRIDE + k, jnp.concatenate(
                [tab_ref[pl.ds(rows[q], TABLE_ROWS_PER_EXPERT), :],
                 tab_ref[pl.ds(rows[q + 1], TABLE_ROWS_PER_EXPERT), :]], axis=0)


def _peer_out_kernel(pk_ref, tab_ref, x_ref, g2_ref, fw_ref, o_ref, p_scr, *, tb, final):
    half = CHUNKS // 2
    upper = lax.broadcasted_iota(jnp.int32, (CHUNKS, LANES), 0) >= half

    def token(t, _):
        views = [pk_ref.at[pl.ds(t * N_SEL + q * PAIR_STRIDE, PAIR_STRIDE)] for q in range(4)]
        accs = [jnp.zeros((CHUNKS, LANES), F32) for _ in range(4)]
        for k in range(PAIR_STRIDE):
            words = [v[k] for v in views]
            for q in (0, 2):
                rows = [pl.multiple_of(words[q + i] & 0xFFFF, TABLE_ROWS_PER_EXPERT) for i in range(2)]
                slab = jnp.concatenate([tab_ref[pl.ds(r, TABLE_ROWS_PER_EXPERT), :] for r in rows], axis=0)
                lo, hi = _unpack(slab)
                both = jnp.where(upper, words[q + 1], words[q])
                wv = lax.bitcast_convert_type(both & jnp.int32(-65536), F32)
                accs[q] = accs[q] + wv * lo
                accs[q + 1] = accs[q + 1] + wv * hi
        alo = accs[0] + accs[2]
        ahi = accs[1] + accs[3]
        off = pl.multiple_of(t * CHUNKS, CHUNKS)
        p_scr[pl.ds(off, half), :] = alo[0:half] + alo[half:CHUNKS]
        p_scr[pl.ds(pl.multiple_of(off + half, half), half), :] = ahi[0:half] + ahi[half:CHUNKS]

    _token_loop(tb, token, 2)
    peer = jnp.concatenate([p_scr[pl.ds(c, tb, stride=CHUNKS), :] for c in range(CHUNKS)], axis=1)
    y = x_ref[...] + g2_ref[0] * peer
    if final:
        y = y * lax.rsqrt(jnp.mean(y * y, axis=-1, keepdims=True) + NORM_EPS) * fw_ref[...]
    o_ref[...] = y


def _peer_residual(x, nw, shift, scale, g2, wq, keys, u, v, final_w, final):
    b, t, d = x.shape
    assert d == CHUNKS * LANES
    idx, gate = _route(x, nw, shift, scale, wq, keys)
    n = b * t
    tb = TOK_BLOCK
    per_batch = t // tb
    x2 = x.reshape(n, d)
    idx1 = idx.reshape(n * N_SEL)
    smem_blk = pl.BlockSpec((tb * N_SEL,), lambda i: (i,), memory_space=pltpu.SMEM)
    table = pl.BlockSpec(u.shape, lambda i: (0, 0), pipeline_mode=pl.Buffered(1))
    rows = pl.BlockSpec((tb, d), lambda i: (i, 0))
    sel = pl.BlockSpec((tb, N_SEL), lambda i: (i, 0))
    vec = pl.BlockSpec((1, 1, d), lambda i: (i // per_batch, 0, 0))
    const = pl.BlockSpec((1, d), lambda i: (0, 0))
    w = pl.pallas_call(
        functools.partial(_peer_act_kernel, tb=tb),
        grid=(n // tb,),
        in_specs=[smem_blk, table, rows, const, vec, vec, sel],
        out_specs=sel,
        out_shape=jax.ShapeDtypeStruct((n, N_SEL), F32),
        scratch_shapes=[pltpu.VMEM((tb * CHUNKS, LANES), F32), pltpu.VMEM((tb, N_SEL), F32)],
        compiler_params=_cparams("arbitrary"),
        name="peer_act",
    )(idx1, u, x2, nw.reshape(1, d), shift, scale, gate.reshape(n, N_SEL))
    w_bits = lax.bitcast_convert_type(w.astype(BF16).astype(F32), jnp.int32) & jnp.int32(-65536)
    packed = (w_bits | idx.reshape(n, N_SEL)).reshape(n * N_SEL)
    y = pl.pallas_call(
        functools.partial(_peer_out_kernel, tb=tb, final=final),
        grid=(n // tb,),
        in_specs=[smem_blk, table, rows, vec, const],
        out_specs=rows,
        out_shape=jax.ShapeDtypeStruct((n, d), F32),
        scratch_shapes=[pltpu.VMEM((tb * CHUNKS, LANES), F32)],
        compiler_params=_cparams("arbitrary"),
        name="peer_out",
    )(packed, v, x2, g2, final_w.reshape(1, d))
    return y.reshape(b, t, d)


def _na_bias_table(rpb):
    cq = jnp.arange(GRID_W)
    coff = jnp.clip(cq[None, :] - cq[:, None] + (NA_WIN_C - 1), 0, 2 * NA_WIN_C - 2)
    c_start = jnp.clip(cq - NA_WIN_C // 2, 0, GRID_W - NA_WIN_C)
    ok = (cq[None, :] >= c_start[:, None]) & (cq[None, :] < c_start[:, None] + NA_WIN_C)
    roff = jnp.arange(NA_WIN_R)[:, None] + jnp.arange(NA_WIN_R)[None, :]
    bias = rpb.astype(F32)[:, :, coff][:, roff]
    bias = jnp.where(ok[None, None, None], bias, NEG_INF)
    bias = bias.transpose(0, 1, 3, 2, 4).reshape(HEAD_PAIRS, 2, NA_WIN_R, GRID_W, NA_WIN_R * GRID_W)
    return bias.transpose(0, 2, 1, 3, 4).reshape(HEAD_PAIRS, NA_WIN_R, 2 * GRID_W, NA_WIN_R * GRID_W)


def _rope_tables(s):
    t = jnp.arange(s)
    row = (t // GRID_W).astype(F32)
    col = (t % GRID_W).astype(F32)
    inv = ROPE_BASE ** (-jnp.arange(ROPE_PAIRS, dtype=F32) / ROPE_PAIRS)
    cr, sr = jnp.cos(row[:, None] * inv), jnp.sin(row[:, None] * inv)
    cc, sc = jnp.cos(col[:, None] * inv), jnp.sin(col[:, None] * inv)
    cos = jnp.concatenate([cr, cr, cc, cc], axis=-1)
    sin = jnp.concatenate([-sr, sr, -sc, sc], axis=-1)
    return jnp.tile(cos, (1, 2)), jnp.tile(sin, (1, 2))


def _block_diag(blocks):
    n, a, bb = blocks.shape
    eye = jnp.eye(n, dtype=blocks.dtype)
    return (eye[:, None, :, None] * blocks[:, :, None, :]).reshape(n * a, n * bb)


def kernel(x, c, ctx, c_ctx, norm1_w, norm2_w, w_ada, b_ada, w_in, w_out, na_rpb, ret_decay_fwd, ret_decay_bwd, ret_gn_w, pool_w, pool_scale, peer_wq, peer_keys, peer_u, peer_v, final_norm_w):
    b, s, d = x.shape
    depth = w_in.shape[0]
    clen = ctx.shape[1]
    cvec = jnp.concatenate([c, c_ctx[None, :], jnp.zeros((8 - b - 1, d), F32)], axis=0)
    mod = _modulation(cvec, w_ada, b_ada)
    cos_x, sin_x = _rope_tables(s)
    cos_c, sin_c = jnp.ones((clen, LANES), F32), jnp.zeros((clen, LANES), F32)
    avg = _block_diag(jnp.full((RET_HEADS, HEAD_DIM, HEAD_DIM), 1.0 / HEAD_DIM, BF16))
    zero_state = jnp.zeros((b, HEAD_PAIRS, LANES, LANES), F32)

    for l in range(depth):
        last = l == depth - 1
        mx = mod[l, :b].reshape(b, 1, 6, d)
        mc = jnp.broadcast_to(mod[l, b].reshape(1, 1, 6, d), (b, 1, 6, d))
        sh1, sc1, g1, sh2, sc2, g2 = [mx[:, :, i] for i in range(6)]
        csh1, csc1, cg1, csh2, csc2, cg2 = [mc[:, :, i] for i in range(6)]
        lg = jnp.stack([jax.nn.log_sigmoid(ret_decay_fwd[l].astype(F32)),
                        jax.nn.log_sigmoid(ret_decay_bwd[l].astype(F32))], axis=0)
        wi = w_in[l].astype(BF16)
        wo = w_out[l].astype(BF16)
        wpool = _block_diag(pool_w[l]).astype(BF16)
        bias8 = _na_bias_table(na_rpb[l])

        pc, pcp = _in_proj(ctx, norm1_w[l], csh1, csc1, wi)
        ycf, ycb, r_f, r_b = _retention(pc, lg, cos_c, sin_c, zero_state, zero_state)

        px, pxp = _in_proj(x, norm1_w[l], sh1, sc1, wi)
        na = _na_attention(px, pc, bias8)
        yf, yb, _, _ = _retention(px, lg, cos_x, sin_x, r_f, r_b)
        pool = _pool(pxp, wpool, pool_scale[l])
        x = _out_proj(na, yf, yb, px, pool, x, g1, ret_gn_w[l], avg, wo)
        wq = jnp.stack(_split_bf16(peer_wq[l]))
        keys = jnp.stack(_split_bf16(peer_keys[l]))
        u_tab = _pack_table(peer_u, l)
        v_tab = _pack_table(peer_v, l)
        x = _peer_residual(x, norm2_w[l], sh2, sc2, g2, wq, keys, u_tab, v_tab, final_norm_w, last)

        if not last:
            na_c = _ctx_attention(pc)
            pool_c = _pool(pcp, wpool, pool_scale[l])
            ctx = _out_proj(na_c, ycf, ycb, pc, pool_c, ctx, cg1, ret_gn_w[l], avg, wo)
            ctx = _peer_residual(ctx, norm2_w[l], csh2, csc2, cg2, wq, keys, u_tab, v_tab, final_norm_w, False)
    return x
```

```python
import functools

import jax
import jax.numpy as jnp
from jax import lax
from jax.experimental import pallas as pl
from jax.experimental.pallas import tpu as pltpu

D_MODEL = 1024
GRID_W = 64
HEAD_DIM = 64
NA_HEADS = 6
NA_WIN_R = 8
NA_WIN_C = 16
RET_HEADS = 6
RET_CHUNK = 128
POOL_WINDOWS = (2, 4, 8, 16)
POOL_GROUP = 64
NA_WIDTH = NA_HEADS * HEAD_DIM
RET_WIDTH = RET_HEADS * HEAD_DIM
POOL_WIDTH = POOL_GROUP * len(POOL_WINDOWS)
O_RET_G = 3 * NA_WIDTH + 3 * RET_WIDTH
O_POOL = O_RET_G + RET_WIDTH
D_PROJ = O_POOL + POOL_WIDTH
ROPE_BASE = 10000.0
ROPE_PAIRS = HEAD_DIM // 4
PEER_HEADS = 8
PEER_NKEYS = 128
PEER_KEY_DIM = 128
PEER_TOPK = 16
N_SEL = PEER_HEADS * PEER_TOPK
NORM_EPS = 1e-6
NEG_INF = -1e30

LANES = 128
VMEM_LIMIT_BYTES = 56 * 1024 * 1024

F32 = jnp.float32
BF16 = jnp.bfloat16
HIGHEST = lax.Precision.HIGHEST
HEAD_PAIRS = NA_HEADS // 2
TOK_BLOCK = 128
CHUNKS = D_MODEL // LANES
TABLE_ROWS_PER_EXPERT = CHUNKS // 2


def _cparams(*sem):
    return pltpu.CompilerParams(dimension_semantics=sem, vmem_limit_bytes=VMEM_LIMIT_BYTES)


def _dot_nt(a, b, precision=None):
    return lax.dot_general(a, b, (((1,), (1,)), ((), ())), precision=precision,
                           preferred_element_type=F32)


def _split_bf16(a):
    hi = a.astype(BF16)
    return hi, (a - hi.astype(F32)).astype(BF16)


def _rms_mod(x, nw, shift, scale):
    y = x * lax.rsqrt(jnp.mean(x * x, axis=-1, keepdims=True) + NORM_EPS)
    return (y * nw) * (1.0 + scale) + shift


def _mod_kernel(c_ref, w_ref, b_ref, o_ref):
    c = c_ref[...]
    a = c * jax.nn.sigmoid(c)
    o_ref[0] = jnp.dot(a, w_ref[0], precision=HIGHEST, preferred_element_type=F32) + b_ref[0]


def _modulation(cvec, w_ada, b_ada):
    depth, d, n = w_ada.shape
    tn = 1536
    return pl.pallas_call(
        _mod_kernel,
        grid=(depth, n // tn),
        in_specs=[pl.BlockSpec((8, d), lambda l, j: (0, 0)),
                  pl.BlockSpec((1, d, tn), lambda l, j: (l, 0, j)),
                  pl.BlockSpec((1, 1, tn), lambda l, j: (l, 0, j))],
        out_specs=pl.BlockSpec((1, 8, tn), lambda l, j: (l, 0, j)),
        out_shape=jax.ShapeDtypeStruct((depth, 8, n), F32),
        compiler_params=_cparams("parallel", "parallel"),
        name="adaln_mod",
    )(cvec, w_ada, b_ada.reshape(depth, 1, n))


def _inproj_kernel(x_ref, nw_ref, sh_ref, sc_ref, w_ref, o_ref, p_ref):
    h = _rms_mod(x_ref[0], nw_ref[...], sh_ref[0], sc_ref[0])
    r = jnp.dot(h.astype(BF16), w_ref[...], preferred_element_type=F32)
    o_ref[0] = r[:, :O_POOL]
    p_ref[0] = r[:, O_POOL:]


def _in_proj(x, nw, shift, scale, w_bf16):
    b, t, d = x.shape
    tm = min(512, t)
    return pl.pallas_call(
        _inproj_kernel,
        grid=(b, t // tm),
        in_specs=[pl.BlockSpec((1, tm, d), lambda i, j: (i, j, 0)),
                  pl.BlockSpec((1, d), lambda i, j: (0, 0)),
                  pl.BlockSpec((1, 1, d), lambda i, j: (i, 0, 0)),
                  pl.BlockSpec((1, 1, d), lambda i, j: (i, 0, 0)),
                  pl.BlockSpec((d, D_PROJ), lambda i, j: (0, 0))],
        out_specs=[pl.BlockSpec((1, tm, O_POOL), lambda i, j: (i, j, 0)),
                   pl.BlockSpec((1, tm, POOL_WIDTH), lambda i, j: (i, j, 0))],
        out_shape=[jax.ShapeDtypeStruct((b, t, O_POOL), F32),
                   jax.ShapeDtypeStruct((b, t, POOL_WIDTH), F32)],
        compiler_params=_cparams("parallel", "parallel"),
        name="in_proj",
    )(x, nw.reshape(1, d), shift, scale, w_bf16)


def _softmax_pv(s_list, v_list):
    m = s_list[0].max(axis=-1, keepdims=True)
    for s in s_list[1:]:
        m = jnp.maximum(m, s.max(axis=-1, keepdims=True))
    num = None
    den = None
    for s, v in zip(s_list, v_list):
        p = jnp.exp(s - m)
        pv = jnp.dot(p.astype(BF16), v.astype(BF16), preferred_element_type=F32)
        ps = p.sum(axis=-1, keepdims=True)
        num = pv if num is None else num + pv
        den = ps if den is None else den + ps
    return num / den


NA_ROWS_PER_TRIP = 4


def _na_kernel(q_ref, k_ref, v_ref, kc_ref, vc_ref, bias_ref, o_ref, *, rows, rb):
    i = pl.program_id(2)
    lane = lax.broadcasted_iota(jnp.int32, (1, LANES), 1)
    first = lane < HEAD_DIM
    kc = kc_ref[0].astype(BF16)
    vc = vc_ref[0].astype(BF16)
    scale = HEAD_DIM ** -0.5
    nk = NA_WIN_R * GRID_W

    def body(it, carry):
        for u in range(NA_ROWS_PER_TRIP):
            one_row(it * NA_ROWS_PER_TRIP + u)
        return carry

    def one_row(rr):
        r = i * rb + rr
        rs = jnp.clip(r - NA_WIN_R // 2, 0, rows - NA_WIN_R)
        delta = rs - r + (NA_WIN_R - 1)
        q = q_ref[0, pl.ds(pl.multiple_of(rr * GRID_W, GRID_W), GRID_W), :]
        k = k_ref[0, pl.ds(pl.multiple_of(rs * GRID_W, GRID_W), nk), :].astype(BF16)
        v = v_ref[0, pl.ds(pl.multiple_of(rs * GRID_W, GRID_W), nk), :].astype(BF16)
        q2 = jnp.concatenate([jnp.where(first, q, 0.0), jnp.where(first, 0.0, q)], axis=0).astype(BF16)
        s = _dot_nt(q2, k) * scale + bias_ref[0, delta]
        sc = _dot_nt(q2, kc) * scale
        o2 = _softmax_pv([s, sc], [v, vc])
        o_ref[0, pl.ds(pl.multiple_of(rr * GRID_W, GRID_W), GRID_W), :] = jnp.where(
            first, o2[0:GRID_W], o2[GRID_W:2 * GRID_W])

    lax.fori_loop(0, rb // NA_ROWS_PER_TRIP, body, 0)


def _na_attention(px, pc, bias8):
    b, s, _ = px.shape
    c = pc.shape[1]
    rows = s // GRID_W
    assert rows >= NA_WIN_R and s % GRID_W == 0
    rb = min(16, rows)
    nq, nkb, nvb = 0, HEAD_PAIRS, 2 * HEAD_PAIRS
    return pl.pallas_call(
        functools.partial(_na_kernel, rows=rows, rb=rb),
        grid=(b, HEAD_PAIRS, rows // rb),
        in_specs=[pl.BlockSpec((1, rb * GRID_W, LANES), lambda bi, hp, i: (bi, i, nq + hp)),
                  pl.BlockSpec((1, s, LANES), lambda bi, hp, i: (bi, 0, nkb + hp)),
                  pl.BlockSpec((1, s, LANES), lambda bi, hp, i: (bi, 0, nvb + hp)),
                  pl.BlockSpec((1, c, LANES), lambda bi, hp, i: (bi, 0, nkb + hp)),
                  pl.BlockSpec((1, c, LANES), lambda bi, hp, i: (bi, 0, nvb + hp)),
                  pl.BlockSpec((1, NA_WIN_R, 2 * GRID_W, NA_WIN_R * GRID_W),
                               lambda bi, hp, i: (hp, 0, 0, 0))],
        out_specs=pl.BlockSpec((1, rb * GRID_W, LANES), lambda bi, hp, i: (bi, i, hp)),
        out_shape=jax.ShapeDtypeStruct((b, s, NA_WIDTH), F32),
        compiler_params=_cparams("parallel", "parallel", "arbitrary"),
        name="na_attention",
    )(px, px, px, pc, pc, bias8)


def _ctx_attn_kernel(q_ref, k_ref, v_ref, o_ref):
    lane = lax.broadcasted_iota(jnp.int32, (1, LANES), 1)
    first = lane < HEAD_DIM
    q = q_ref[0]
    k = k_ref[0]
    v = v_ref[0]
    outs = []
    for h in range(2):
        qm = jnp.where(first if h == 0 else jnp.logical_not(first), q, 0.0)
        s = _dot_nt(qm, k) * HEAD_DIM ** -0.5
        outs.append(_softmax_pv([s], [v]))
    o_ref[0] = jnp.where(first, outs[0], outs[1])


def _ctx_attention(pc):
    b, c, _ = pc.shape
    return pl.pallas_call(
        _ctx_attn_kernel,
        grid=(b, HEAD_PAIRS),
        in_specs=[pl.BlockSpec((1, c, LANES), lambda bi, hp: (bi, 0, hp)),
                  pl.BlockSpec((1, c, LANES), lambda bi, hp: (bi, 0, HEAD_PAIRS + hp)),
                  pl.BlockSpec((1, c, LANES), lambda bi, hp: (bi, 0, 2 * HEAD_PAIRS + hp))],
        out_specs=pl.BlockSpec((1, c, LANES), lambda bi, hp: (bi, 0, hp)),
        out_shape=jax.ShapeDtypeStruct((b, c, NA_WIDTH), F32),
        compiler_params=_cparams("parallel", "parallel"),
        name="ctx_attention",
    )(pc, pc, pc)


RET_CHUNKS_PER_STEP = 8


def _ret_kernel(lg_ref, qf_ref, kf_ref, vf_ref, qb_ref, kb_ref, vb_ref,
                cf_ref, sf_ref, cb_ref, sb_ref, rf0_ref, rb0_ref,
                yf_ref, yb_ref, rfo_ref, rbo_ref, rf_scr, rb_scr, *, cpb):
    hp = pl.program_id(1)
    c = pl.program_id(2)
    nc = pl.num_programs(2)
    cs = RET_CHUNK

    @pl.when(c == 0)
    def _():
        rf_scr[...] = rf0_ref[0, 0]
        rb_scr[...] = rb0_ref[0, 0]

    lane = lax.broadcasted_iota(jnp.int32, (1, LANES), 1)
    first = lane < HEAD_DIM
    low = (lane % (2 * ROPE_PAIRS)) < ROPE_PAIRS
    pos = lax.broadcasted_iota(jnp.int32, (cs, 1), 0).astype(F32)
    ii = lax.broadcasted_iota(jnp.int32, (cs, cs), 0)
    jj = lax.broadcasted_iota(jnp.int32, (cs, cs), 1)
    diff = (ii - jj).astype(F32)
    same_head = (ii < HEAD_DIM) == (jj < HEAD_DIM)
    scale = HEAD_DIM ** -0.5

    def rope(x, cos, sin):
        swapped = jnp.where(low, pltpu.roll(x, LANES - ROPE_PAIRS, 1), pltpu.roll(x, ROPE_PAIRS, 1))
        return x * cos + swapped * sin

    def direction(d, q_ref, k_ref, v_ref, cos_ref, sin_ref, r_scr, y_ref):
        lg0 = lg_ref[d, 2 * hp]
        lg1 = lg_ref[d, 2 * hp + 1]
        lgv = jnp.where(first, lg0, lg1)
        decs = []
        for lg in (lg0, lg1):
            if d == 0:
                decs.append(jnp.where(diff >= 0, jnp.exp(jnp.maximum(diff, 0.0) * lg), 0.0))
            else:
                decs.append(jnp.where(diff <= 0, jnp.exp(jnp.maximum(-diff, 0.0) * lg), 0.0))
        dec2 = jnp.concatenate(decs, axis=0)
        if d == 0:
            xi = jnp.exp((pos + 1.0) * lgv)
            zeta = jnp.exp((cs - 1.0 - pos) * lgv)
        else:
            xi = jnp.exp((cs - pos) * lgv)
            zeta = jnp.exp(pos * lgv)
        chunk_decay = jnp.exp(cs * lgv)
        r = r_scr[...]
        for j in (range(cpb) if d == 0 else reversed(range(cpb))):
            rows = pl.ds(j * cs, cs)
            cos = cos_ref[rows, :]
            sin = sin_ref[rows, :]
            q = rope(q_ref[0, rows, :], cos, sin)
            k = rope(k_ref[0, rows, :], cos, sin) * scale
            v = v_ref[0, rows, :].astype(BF16)
            q2 = jnp.concatenate([jnp.where(first, q, 0.0), jnp.where(first, 0.0, q)], axis=0).astype(BF16)
            s2 = _dot_nt(q2, k.astype(BF16)) * dec2
            o2 = jnp.dot(s2.astype(BF16), v, preferred_element_type=F32)
            inner = jnp.where(first, o2[0:cs], o2[cs:2 * cs])
            y_ref[0, rows, :] = inner + jnp.dot((q * xi).astype(BF16), r.astype(BF16),
                                                preferred_element_type=F32)
            kv = lax.dot_general((k * zeta).astype(BF16), v, (((0,), (0,)), ((), ())),
                                 preferred_element_type=F32)
            r = chunk_decay * r + jnp.where(same_head, kv, 0.0)
        r_scr[...] = r

    direction(0, qf_ref, kf_ref, vf_ref, cf_ref, sf_ref, rf_scr, yf_ref)
    direction(1, qb_ref, kb_ref, vb_ref, cb_ref, sb_ref, rb_scr, yb_ref)

    @pl.when(c == nc - 1)
    def _():
        rfo_ref[0, 0] = rf_scr[...]
        rbo_ref[0, 0] = rb_scr[...]


def _retention(p, lg, cos_t, sin_t, rf0, rb0):
    b, t, _ = p.shape
    cpb = min(RET_CHUNKS_PER_STEP, t // RET_CHUNK)
    nc = t // (RET_CHUNK * cpb)
    qo, ko, vo = 3 * HEAD_PAIRS, 4 * HEAD_PAIRS, 5 * HEAD_PAIRS
    blk = (1, RET_CHUNK * cpb, LANES)
    fwd = lambda o: pl.BlockSpec(blk, lambda bi, hp, c: (bi, c, o + hp))
    bwd = lambda o: pl.BlockSpec(blk, lambda bi, hp, c: (bi, nc - 1 - c, o + hp))
    tab_f = pl.BlockSpec((RET_CHUNK * cpb, LANES), lambda bi, hp, c: (c, 0))
    tab_b = pl.BlockSpec((RET_CHUNK * cpb, LANES), lambda bi, hp, c: (nc - 1 - c, 0))
    st = pl.BlockSpec((1, 1, LANES, LANES), lambda bi, hp, c: (bi, hp, 0, 0))
    return pl.pallas_call(
        functools.partial(_ret_kernel, cpb=cpb),
        grid=(b, HEAD_PAIRS, nc),
        in_specs=[pl.BlockSpec(memory_space=pltpu.SMEM),
                  fwd(qo), fwd(ko), fwd(vo), bwd(qo), bwd(ko), bwd(vo),
                  tab_f, tab_f, tab_b, tab_b, st, st],
        out_specs=[pl.BlockSpec(blk, lambda bi, hp, c: (bi, c, hp)),
                   pl.BlockSpec(blk, lambda bi, hp, c: (bi, nc - 1 - c, hp)),
                   st, st],
        out_shape=[jax.ShapeDtypeStruct((b, t, RET_WIDTH), F32),
                   jax.ShapeDtypeStruct((b, t, RET_WIDTH), F32),
                   jax.ShapeDtypeStruct((b, HEAD_PAIRS, LANES, LANES), F32),
                   jax.ShapeDtypeStruct((b, HEAD_PAIRS, LANES, LANES), F32)],
        scratch_shapes=[pltpu.VMEM((LANES, LANES), F32), pltpu.VMEM((LANES, LANES), F32)],
        compiler_params=_cparams("parallel", "parallel", "arbitrary"),
        name="retention",
    )(lg, p, p, p, p, p, p, cos_t, sin_t, cos_t, sin_t, rf0, rb0)


def _pool_kernel(prev_ref, cur_ref, next_ref, w_ref, s_ref, o_ref, scr, *, t_total, tp):
    i = pl.program_id(1)
    n = pl.num_programs(1)
    halo = POOL_WINDOWS[-1] // 2
    x = cur_ref[0]
    scr[0:halo, :] = jnp.where(i > 0, prev_ref[0], 0.0)
    scr[halo:halo + tp, :] = x
    scr[halo + tp:2 * halo + tp, :] = jnp.where(i < n - 1, next_ref[0], 0.0)
    t = i * tp + lax.broadcasted_iota(jnp.int32, (tp, 1), 0)
    lane = lax.broadcasted_iota(jnp.int32, (1, POOL_WIDTH), 1)

    def shifted(s):
        return scr[halo + s:halo + s + tp, :]

    acc = None
    mean = None
    done = 0
    for g, w in enumerate(POOL_WINDOWS):
        half = w // 2
        for s in list(range(-half, -done)) + list(range(done, half)):
            sh = x if s == 0 else shifted(s)
            acc = sh if acc is None else acc + sh
        done = half
        cnt = (jnp.minimum(t + half, t_total) - jnp.maximum(t - half, 0)).astype(F32)
        mg = acc / cnt
        mean = mg if mean is None else jnp.where(lane >= g * POOL_GROUP, mg, mean)
    dlt = (mean - x).astype(BF16)
    o_ref[0] = jnp.dot(dlt, w_ref[...], preferred_element_type=F32) * s_ref[...]


def _pool(pin, w_bd_bf16, scale):
    b, t, _ = pin.shape
    tp = min(1024, t)
    halo = POOL_WINDOWS[-1] // 2
    nh = tp // halo
    last = t // halo - 1
    return pl.pallas_call(
        functools.partial(_pool_kernel, t_total=t, tp=tp),
        grid=(b, t // tp),
        in_specs=[pl.BlockSpec((1, halo, POOL_WIDTH), lambda bi, i: (bi, jnp.maximum(i * nh - 1, 0), 0)),
                  pl.BlockSpec((1, tp, POOL_WIDTH), lambda bi, i: (bi, i, 0)),
                  pl.BlockSpec((1, halo, POOL_WIDTH), lambda bi, i: (bi, jnp.minimum((i + 1) * nh, last), 0)),
                  pl.BlockSpec((POOL_WIDTH, POOL_WIDTH), lambda bi, i: (0, 0)),
                  pl.BlockSpec((1, POOL_WIDTH), lambda bi, i: (0, 0))],
        out_specs=pl.BlockSpec((1, tp, POOL_WIDTH), lambda bi, i: (bi, i, 0)),
        out_shape=jax.ShapeDtypeStruct((b, t, POOL_WIDTH), F32),
        scratch_shapes=[pltpu.VMEM((tp + 2 * halo, POOL_WIDTH), F32)],
        compiler_params=_cparams("parallel", "parallel"),
        name="multiscale_pool",
    )(pin, pin, pin, w_bd_bf16, scale.reshape(1, POOL_WIDTH))


def _out_kernel(na_ref, yf_ref, yb_ref, g_ref, pool_ref, x_ref, g1_ref, gnw_ref, avg_ref, wo_ref, o_ref):
    y = yf_ref[0] + yb_ref[0]
    avg = avg_ref[...]

    def group_mean(a):
        hi = a.astype(BF16)
        rest = a - hi.astype(F32)
        mid = rest.astype(BF16)
        lo = (rest - mid.astype(F32)).astype(BF16)
        return (jnp.dot(hi, avg, preferred_element_type=F32) + jnp.dot(mid, avg, preferred_element_type=F32)
                + jnp.dot(lo, avg, preferred_element_type=F32))

    mu = group_mean(y)
    d = y - mu
    var = group_mean(d * d)
    yn = d * lax.rsqrt(var + NORM_EPS) * gnw_ref[...]
    g = g_ref[0]
    ret = yn * (g * jax.nn.sigmoid(g))
    mix = jnp.dot(na_ref[0].astype(BF16), wo_ref[0:NA_WIDTH, :], preferred_element_type=F32)
    mix += jnp.dot(ret.astype(BF16), wo_ref[NA_WIDTH:NA_WIDTH + RET_WIDTH, :], preferred_element_type=F32)
    mix += jnp.dot(pool_ref[0].astype(BF16), wo_ref[NA_WIDTH + RET_WIDTH:, :], preferred_element_type=F32)
    o_ref[0] = x_ref[0] + g1_ref[0] * mix


def _out_proj(na, yf, yb, p, pool, x, g1, gn_w, avg, wo_bf16):
    b, t, d = x.shape
    tm = min(512, t)
    row = lambda w: pl.BlockSpec((1, tm, w), lambda i, j: (i, j, 0))
    return pl.pallas_call(
        _out_kernel,
        grid=(b, t // tm),
        in_specs=[row(NA_WIDTH), row(RET_WIDTH), row(RET_WIDTH),
                  pl.BlockSpec((1, tm, RET_WIDTH), lambda i, j: (i, j, O_RET_G // RET_WIDTH)),
                  row(POOL_WIDTH), row(d),
                  pl.BlockSpec((1, 1, d), lambda i, j: (i, 0, 0)),
                  pl.BlockSpec((1, RET_WIDTH), lambda i, j: (0, 0)),
                  pl.BlockSpec((RET_WIDTH, RET_WIDTH), lambda i, j: (0, 0)),
                  pl.BlockSpec((d, d), lambda i, j: (0, 0))],
        out_specs=row(d),
        out_shape=jax.ShapeDtypeStruct((b, t, d), F32),
        compiler_params=_cparams("parallel", "parallel"),
        name="out_proj",
    )(na, yf, yb, p, pool, x, g1, gn_w.reshape(1, RET_WIDTH), avg, wo_bf16)


def _topk_rows(s, order=None, payload=None):
    n, m = s.shape
    if order is None:
        order = lax.broadcasted_iota(jnp.int32, (n, m), 0)
    krow = lax.broadcasted_iota(jnp.int32, (PEER_TOPK, m), 0)
    vals = jnp.zeros((PEER_TOPK, m), F32)
    idxs = jnp.zeros((PEER_TOPK, m), jnp.int32)
    for k in range(PEER_TOPK):
        mx = jnp.max(s, axis=0, keepdims=True)
        am = jnp.min(jnp.where(s == mx, order, jnp.iinfo(jnp.int32).max), axis=0, keepdims=True)
        sel = order == am
        pick = am if payload is None else jnp.sum(jnp.where(sel, payload, 0), axis=0, keepdims=True)
        vals = jnp.where(krow == k, mx, vals)
        idxs = jnp.where(krow == k, pick, idxs)
        s = jnp.where(sel, -jnp.inf, s)
    return vals, idxs


def _product_candidates(va, ia, vb, ib):
    k = PEER_TOPK
    sub = 8
    m = va.shape[1]
    row = lax.broadcasted_iota(jnp.int32, (sub, m), 0)
    cand, flat, eid = [], [], []

    def add(v, f, e, nvalid):
        cand.append(v if nvalid >= sub else jnp.where(row < nvalid, v, -jnp.inf))
        flat.append(f)
        eid.append(e)

    for i in range(sub):
        add(va[i:i + 1, :] + vb[0:sub, :], i * k + row, ia[i:i + 1, :] * PEER_NKEYS + ib[0:sub, :], k // (i + 1))
    add(va[0:1, :] + vb[sub:k, :], sub + row, ia[0:1, :] * PEER_NKEYS + ib[sub:k, :], sub)
    add(va[sub:k, :] + vb[0:1, :], (sub + row) * k, ia[sub:k, :] * PEER_NKEYS + ib[0:1, :], sub)
    return jnp.concatenate(cand, axis=0), jnp.concatenate(flat, axis=0), jnp.concatenate(eid, axis=0)


def _route_kernel(x_ref, nw_ref, sh_ref, sc_ref, wq_ref, keys_ref, idx_ref, gate_ref, q_scr, g_scr, i_scr, *, tm):
    h_hi, h_lo = _split_bf16(_rms_mod(x_ref[0], nw_ref[...], sh_ref[0], sc_ref[0]))
    q = (jnp.dot(h_hi, wq_ref[0], preferred_element_type=F32)
         + jnp.dot(h_hi, wq_ref[1], preferred_element_type=F32)
         + jnp.dot(h_lo, wq_ref[0], preferred_element_type=F32))
    for j in range(2 * PEER_HEADS):
        q_scr[j] = q[:, j * PEER_KEY_DIM:(j + 1) * PEER_KEY_DIM]

    def scores(half, hh):
        q_hi, q_lo = _split_bf16(q_scr[2 * hh + half])
        k_hi = keys_ref[0, half, hh]
        return _dot_nt(k_hi, q_hi) + _dot_nt(k_hi, q_lo) + _dot_nt(keys_ref[1, half, hh], q_hi)

    def head(hh, carry):
        sa = scores(0, hh)
        sb = scores(1, hh)
        va, ia = _topk_rows(sa)
        vb, ib = _topk_rows(sb)
        cand, flat, eid = _product_candidates(va, ia, vb, ib)
        sc, ei = _topk_rows(cand, order=flat, payload=eid)
        e = jnp.exp(sc - sc[0:1, :])
        off = pl.multiple_of(hh * PEER_TOPK, PEER_TOPK)
        g_scr[pl.ds(off, PEER_TOPK), :] = e / jnp.sum(e, axis=0, keepdims=True)
        i_scr[pl.ds(off, PEER_TOPK), :] = ei
        return carry

    def head_pair(i, carry):
        head(2 * i, carry)
        return head(2 * i + 1, carry)

    lax.fori_loop(0, PEER_HEADS // 2, head_pair, 0)
    gate_ref[0] = g_scr[...].T
    idx_ref[0] = i_scr[...].T * TABLE_ROWS_PER_EXPERT


def _route(x, nw, shift, scale, wq, keys):
    b, t, d = x.shape
    tm = min(1024, t)
    nt = t // tm
    nq = wq.shape[2]
    return pl.pallas_call(
        functools.partial(_route_kernel, tm=tm),
        grid=(b, nt),
        in_specs=[pl.BlockSpec((1, tm, d), lambda i, j: (i, j, 0)),
                  pl.BlockSpec((1, d), lambda i, j: (0, 0)),
                  pl.BlockSpec((1, 1, d), lambda i, j: (i, 0, 0)),
                  pl.BlockSpec((1, 1, d), lambda i, j: (i, 0, 0)),
                  pl.BlockSpec((2, d, nq), lambda i, j: (0, 0, 0)),
                  pl.BlockSpec((2, 2, PEER_HEADS, PEER_NKEYS, PEER_KEY_DIM), lambda i, j: (0, 0, 0, 0, 0))],
        out_specs=[pl.BlockSpec((1, tm, N_SEL), lambda i, j: (i, j, 0)),
                   pl.BlockSpec((1, tm, N_SEL), lambda i, j: (i, j, 0))],
        out_shape=[jax.ShapeDtypeStruct((b, t, N_SEL), jnp.int32),
                   jax.ShapeDtypeStruct((b, t, N_SEL), F32)],
        scratch_shapes=[pltpu.VMEM((2 * PEER_HEADS, tm, PEER_KEY_DIM), F32),
                        pltpu.VMEM((N_SEL, tm), F32),
                        pltpu.VMEM((N_SEL, tm), jnp.int32)],
        compiler_params=_cparams("parallel", "parallel"),
        name="peer_route",
    )(x, nw.reshape(1, d), shift, scale, wq, keys)


def _pack_kernel(t_ref, o_ref, *, te):
    x = t_ref[0]
    d = x.shape[1]

    def bf16_bits(a):
        return lax.bitcast_convert_type(a.astype(BF16).astype(F32), jnp.uint32)

    words = (bf16_bits(x[:, :d // 2]) >> 16) | bf16_bits(x[:, d // 2:])
    for q in range(TABLE_ROWS_PER_EXPERT):
        o_ref[pl.ds(q, te, stride=TABLE_ROWS_PER_EXPERT), :] = words[:, q * LANES:(q + 1) * LANES]


def _pack_table(tabs, layer):
    _, n, d = tabs.shape
    te = 512
    return pl.pallas_call(
        functools.partial(_pack_kernel, te=te),
        grid=(n // te,),
        in_specs=[pl.BlockSpec((1, te, d), lambda i: (layer, i, 0))],
        out_specs=pl.BlockSpec((te * TABLE_ROWS_PER_EXPERT, LANES), lambda i: (i, 0)),
        out_shape=jax.ShapeDtypeStruct((n * TABLE_ROWS_PER_EXPERT, LANES), jnp.uint32),
        compiler_params=_cparams("parallel"),
        name="pack_table",
    )(tabs)


def _token_loop(tb, token, per_trip):
    def trip(i, carry):
        for j in range(per_trip):
            token(i * per_trip + j, j)
        return carry

    lax.fori_loop(0, tb // per_trip, trip, 0)


def _unpack(slab):
    lo = lax.bitcast_convert_type(slab << 16, F32)
    hi = lax.bitcast_convert_type(slab & jnp.uint32(0xFFFF0000), F32)
    return lo, hi


def _peer_act_kernel(idx_ref, tab_ref, x_ref, nw_ref, sh_ref, sc_ref, gate_ref, w_ref, h_scr, act_scr, *, tb):
    h = _rms_mod(x_ref[...], nw_ref[...], sh_ref[0], sc_ref[0])
    half = CHUNKS // 2
    for c in range(CHUNKS):
        h_scr[pl.ds(c, tb, stride=CHUNKS), :] = h[:, c * LANES:(c + 1) * LANES]
    sub = lax.broadcasted_iota(jnp.int32, (CHUNKS, N_SEL), 0)
    lane = lax.broadcasted_iota(jnp.int32, (CHUNKS, N_SEL), 1)
    slot = lane - PAIR_STRIDE * (sub >= half).astype(jnp.int32)

    def token(t, _):
        ht = h_scr[pl.ds(pl.multiple_of(t * CHUNKS, CHUNKS), CHUNKS), :]
        hlo = jnp.concatenate([ht[0:half], ht[0:half]], axis=0)
        hhi = jnp.concatenate([ht[half:CHUNKS], ht[half:CHUNKS]], axis=0)
        acc = jnp.zeros((CHUNKS, N_SEL), F32)
        for a, slab in _slab_pairs(tab_ref, idx_ref, t):
            lo, hi = _unpack(slab)
            dot = jnp.sum(lo * hlo + hi * hhi, axis=-1, keepdims=True)
            acc = jnp.where(slot == a, dot, acc)
        act_scr[pl.ds(t, 1), :] = jnp.sum(acc, axis=0, keepdims=True)

    _token_loop(tb, token, 32)
    a = act_scr[...]
    w_ref[...] = gate_ref[...] * (0.5 * a * (1.0 + lax.erf(a * (2.0 ** -0.5))))


PAIR_STRIDE = N_SEL // 4


def _slab_pairs(tab_ref, idx_ref, t):
    views = [idx_ref.at[pl.ds(t * N_SEL + q * PAIR_STRIDE, PAIR_STRIDE)] for q in range(4)]
    for k in range(PAIR_STRIDE):
        rows = [pl.multiple_of(v[k], TABLE_ROWS_PER_EXPERT) for v in views]
        for q in (0, 2):
            yield q * PAIR_STRIDE + k, jnp.concatenate(
                [tab_ref[pl.ds(rows[q], TABLE_ROWS_PER_EXPERT), :],
                 tab_ref[pl.ds(rows[q + 1], TABLE_ROWS_PER_EXPERT), :]], axis=0)


def _peer_out_kernel(pk_ref, tab_ref, x_ref, g2_ref, fw_ref, o_ref, p_scr, *, tb, final):
    half = CHUNKS // 2
    upper = lax.broadcasted_iota(jnp.int32, (CHUNKS, LANES), 0) >= half

    def token(t, _):
        views = [pk_ref.at[pl.ds(t * N_SEL + q * PAIR_STRIDE, PAIR_STRIDE)] for q in range(4)]
        accs = [jnp.zeros((CHUNKS, LANES), F32) for _ in range(4)]
        for k in range(PAIR_STRIDE):
            words = [v[k] for v in views]
            for q in (0, 2):
                rows = [pl.multiple_of(words[q + i] & 0xFFFF, TABLE_ROWS_PER_EXPERT) for i in range(2)]
                slab = jnp.concatenate([tab_ref[pl.ds(r, TABLE_ROWS_PER_EXPERT), :] for r in rows], axis=0)
                lo, hi = _unpack(slab)
                both = jnp.where(upper, words[q + 1], words[q])
                wv = lax.bitcast_convert_type(both & jnp.int32(-65536), F32)
                accs[q] = accs[q] + wv * lo
                accs[q + 1] = accs[q + 1] + wv * hi
        alo = accs[0] + accs[2]
        ahi = accs[1] + accs[3]
        off = pl.multiple_of(t * CHUNKS, CHUNKS)
        p_scr[pl.ds(off, half), :] = alo[0:half] + alo[half:CHUNKS]
        p_scr[pl.ds(pl.multiple_of(off + half, half), half), :] = ahi[0:half] + ahi[half:CHUNKS]

    _token_loop(tb, token, 2)
    peer = jnp.concatenate([p_scr[pl.ds(c, tb, stride=CHUNKS), :] for c in range(CHUNKS)], axis=1)
    y = x_ref[...] + g2_ref[0] * peer
    if final:
        y = y * lax.rsqrt(jnp.mean(y * y, axis=-1, keepdims=True) + NORM_EPS) * fw_ref[...]
    o_ref[...] = y


def _peer_residual(x, nw, shift, scale, g2, wq, keys, u, v, final_w, final):
    b, t, d = x.shape
    assert d == CHUNKS * LANES
    idx, gate = _route(x, nw, shift, scale, wq, keys)
    n = b * t
    tb = TOK_BLOCK
    per_batch = t // tb
    x2 = x.reshape(n, d)
    idx1 = idx.reshape(n * N_SEL)
    smem_blk = pl.BlockSpec((tb * N_SEL,), lambda i: (i,), memory_space=pltpu.SMEM)
    table = pl.BlockSpec(u.shape, lambda i: (0, 0), pipeline_mode=pl.Buffered(1))
    rows = pl.BlockSpec((tb, d), lambda i: (i, 0))
    sel = pl.BlockSpec((tb, N_SEL), lambda i: (i, 0))
    vec = pl.BlockSpec((1, 1, d), lambda i: (i // per_batch, 0, 0))
    const = pl.BlockSpec((1, d), lambda i: (0, 0))
    w = pl.pallas_call(
        functools.partial(_peer_act_kernel, tb=tb),
        grid=(n // tb,),
        in_specs=[smem_blk, table, rows, const, vec, vec, sel],
        out_specs=sel,
        out_shape=jax.ShapeDtypeStruct((n, N_SEL), F32),
        scratch_shapes=[pltpu.VMEM((tb * CHUNKS, LANES), F32), pltpu.VMEM((tb, N_SEL), F32)],
        compiler_params=_cparams("arbitrary"),
        name="peer_act",
    )(idx1, u, x2, nw.reshape(1, d), shift, scale, gate.reshape(n, N_SEL))
    w_bits = lax.bitcast_convert_type(w.astype(BF16).astype(F32), jnp.int32) & jnp.int32(-65536)
    packed = (w_bits | idx.reshape(n, N_SEL)).reshape(n * N_SEL)
    y = pl.pallas_call(
        functools.partial(_peer_out_kernel, tb=tb, final=final),
        grid=(n // tb,),
        in_specs=[smem_blk, table, rows, vec, const],
        out_specs=rows,
        out_shape=jax.ShapeDtypeStruct((n, d), F32),
        scratch_shapes=[pltpu.VMEM((tb * CHUNKS, LANES), F32)],
        compiler_params=_cparams("arbitrary"),
        name="peer_out",
    )(packed, v, x2, g2, final_w.reshape(1, d))
    return y.reshape(b, t, d)


def _na_bias_table(rpb):
    cq = jnp.arange(GRID_W)
    coff = jnp.clip(cq[None, :] - cq[:, None] + (NA_WIN_C - 1), 0, 2 * NA_WIN_C - 2)
    c_start = jnp.clip(cq - NA_WIN_C // 2, 0, GRID_W - NA_WIN_C)
    ok = (cq[None, :] >= c_start[:, None]) & (cq[None, :] < c_start[:, None] + NA_WIN_C)
    roff = jnp.arange(NA_WIN_R)[:, None] + jnp.arange(NA_WIN_R)[None, :]
    bias = rpb.astype(F32)[:, :, coff][:, roff]
    bias = jnp.where(ok[None, None, None], bias, NEG_INF)
    bias = bias.transpose(0, 1, 3, 2, 4).reshape(HEAD_PAIRS, 2, NA_WIN_R, GRID_W, NA_WIN_R * GRID_W)
    return bias.transpose(0, 2, 1, 3, 4).reshape(HEAD_PAIRS, NA_WIN_R, 2 * GRID_W, NA_WIN_R * GRID_W)


def _rope_tables(s):
    t = jnp.arange(s)
    row = (t // GRID_W).astype(F32)
    col = (t % GRID_W).astype(F32)
    inv = ROPE_BASE ** (-jnp.arange(ROPE_PAIRS, dtype=F32) / ROPE_PAIRS)
    cr, sr = jnp.cos(row[:, None] * inv), jnp.sin(row[:, None] * inv)
    cc, sc = jnp.cos(col[:, None] * inv), jnp.sin(col[:, None] * inv)
    cos = jnp.concatenate([cr, cr, cc, cc], axis=-1)
    sin = jnp.concatenate([-sr, sr, -sc, sc], axis=-1)
    return jnp.tile(cos, (1, 2)), jnp.tile(sin, (1, 2))


def _block_diag(blocks):
    n, a, bb = blocks.shape
    eye = jnp.eye(n, dtype=blocks.dtype)
    return (eye[:, None, :, None] * blocks[:, :, None, :]).reshape(n * a, n * bb)


def kernel(x, c, ctx, c_ctx, norm1_w, norm2_w, w_ada, b_ada, w_in, w_out, na_rpb, ret_decay_fwd, ret_decay_bwd, ret_gn_w, pool_w, pool_scale, peer_wq, peer_keys, peer_u, peer_v, final_norm_w):
    b, s, d = x.shape
    depth = w_in.shape[0]
    clen = ctx.shape[1]
    cvec = jnp.concatenate([c, c_ctx[None, :], jnp.zeros((8 - b - 1, d), F32)], axis=0)
    mod = _modulation(cvec, w_ada, b_ada)
    cos_x, sin_x = _rope_tables(s)
    cos_c, sin_c = jnp.ones((clen, LANES), F32), jnp.zeros((clen, LANES), F32)
    avg = _block_diag(jnp.full((RET_HEADS, HEAD_DIM, HEAD_DIM), 1.0 / HEAD_DIM, BF16))
    zero_state = jnp.zeros((b, HEAD_PAIRS, LANES, LANES), F32)

    for l in range(depth):
        last = l == depth - 1
        mx = mod[l, :b].reshape(b, 1, 6, d)
        mc = jnp.broadcast_to(mod[l, b].reshape(1, 1, 6, d), (b, 1, 6, d))
        sh1, sc1, g1, sh2, sc2, g2 = [mx[:, :, i] for i in range(6)]
        csh1, csc1, cg1, csh2, csc2, cg2 = [mc[:, :, i] for i in range(6)]
        lg = jnp.stack([jax.nn.log_sigmoid(ret_decay_fwd[l].astype(F32)),
                        jax.nn.log_sigmoid(ret_decay_bwd[l].astype(F32))], axis=0)
        wi = w_in[l].astype(BF16)
        wo = w_out[l].astype(BF16)
        wpool = _block_diag(pool_w[l]).astype(BF16)
        bias8 = _na_bias_table(na_rpb[l])

        pc, pcp = _in_proj(ctx, norm1_w[l], csh1, csc1, wi)
        ycf, ycb, r_f, r_b = _retention(pc, lg, cos_c, sin_c, zero_state, zero_state)

        px, pxp = _in_proj(x, norm1_w[l], sh1, sc1, wi)
        na = _na_attention(px, pc, bias8)
        yf, yb, _, _ = _retention(px, lg, cos_x, sin_x, r_f, r_b)
        pool = _pool(pxp, wpool, pool_scale[l])
        x = _out_proj(na, yf, yb, px, pool, x, g1, ret_gn_w[l], avg, wo)
        wq = jnp.stack(_split_bf16(peer_wq[l]))
        keys = jnp.stack(_split_bf16(peer_keys[l]))
        u_tab = _pack_table(peer_u, l)
        v_tab = _pack_table(peer_v, l)
        x = _peer_residual(x, norm2_w[l], sh2, sc2, g2, wq, keys, u_tab, v_tab, final_norm_w, last)

        if not last:
            na_c = _ctx_attention(pc)
            pool_c = _pool(pcp, wpool, pool_scale[l])
            ctx = _out_proj(na_c, ycf, ycb, pc, pool_c, ctx, cg1, ret_gn_w[l], avg, wo)
            ctx = _peer_residual(ctx, norm2_w[l], csh2, csc2, cg2, wq, keys, u_tab, v_tab, final_norm_w, False)
    return x
```

```python
import functools

import jax
import jax.numpy as jnp
from jax import lax
from jax.experimental import pallas as pl
from jax.experimental.pallas import tpu as pltpu

D_MODEL = 1024
GRID_W = 64
HEAD_DIM = 64
NA_HEADS = 6
NA_WIN_R = 8
NA_WIN_C = 16
RET_HEADS = 6
RET_CHUNK = 128
POOL_WINDOWS = (2, 4, 8, 16)
POOL_GROUP = 64
NA_WIDTH = NA_HEADS * HEAD_DIM
RET_WIDTH = RET_HEADS * HEAD_DIM
POOL_WIDTH = POOL_GROUP * len(POOL_WINDOWS)
O_RET_G = 3 * NA_WIDTH + 3 * RET_WIDTH
O_POOL = O_RET_G + RET_WIDTH
D_PROJ = O_POOL + POOL_WIDTH
ROPE_BASE = 10000.0
ROPE_PAIRS = HEAD_DIM // 4
PEER_HEADS = 8
PEER_NKEYS = 128
PEER_KEY_DIM = 128
PEER_TOPK = 16
N_SEL = PEER_HEADS * PEER_TOPK
NORM_EPS = 1e-6
NEG_INF = -1e30

LANES = 128
VMEM_LIMIT_BYTES = 56 * 1024 * 1024

F32 = jnp.float32
BF16 = jnp.bfloat16
HIGHEST = lax.Precision.HIGHEST
HEAD_PAIRS = NA_HEADS // 2
TOK_BLOCK = 128
CHUNKS = D_MODEL // LANES
TABLE_ROWS_PER_EXPERT = CHUNKS // 2
PAIR_STRIDE = N_SEL // 4


def _cparams(*sem):
    return pltpu.CompilerParams(dimension_semantics=sem, vmem_limit_bytes=VMEM_LIMIT_BYTES)


def _dot_nt(a, b, precision=None):
    return lax.dot_general(a, b, (((1,), (1,)), ((), ())), precision=precision,
                           preferred_element_type=F32)


def _split_bf16(a):
    hi = a.astype(BF16)
    return hi, (a - hi.astype(F32)).astype(BF16)


def _rms_mod(x, nw, shift, scale):
    y = x * lax.rsqrt(jnp.mean(x * x, axis=-1, keepdims=True) + NORM_EPS)
    return (y * nw) * (1.0 + scale) + shift


def _mod_kernel(c_ref, w_ref, b_ref, o_ref):
    c = c_ref[...]
    a = c * jax.nn.sigmoid(c)
    o_ref[0] = jnp.dot(a, w_ref[0], precision=HIGHEST, preferred_element_type=F32) + b_ref[0]


def _modulation(cvec, w_ada, b_ada):
    depth, d, n = w_ada.shape
    tn = 1536
    return pl.pallas_call(
        _mod_kernel,
        grid=(depth, n // tn),
        in_specs=[pl.BlockSpec((8, d), lambda l, j: (0, 0)),
                  pl.BlockSpec((1, d, tn), lambda l, j: (l, 0, j)),
                  pl.BlockSpec((1, 1, tn), lambda l, j: (l, 0, j))],
        out_specs=pl.BlockSpec((1, 8, tn), lambda l, j: (l, 0, j)),
        out_shape=jax.ShapeDtypeStruct((depth, 8, n), F32),
        compiler_params=_cparams("parallel", "parallel"),
        name="adaln_mod",
    )(cvec, w_ada, b_ada.reshape(depth, 1, n))


def _inproj_kernel(x_ref, nw_ref, sh_ref, sc_ref, w_ref, o_ref, p_ref):
    h = _rms_mod(x_ref[0], nw_ref[...], sh_ref[0], sc_ref[0])
    r = jnp.dot(h.astype(BF16), w_ref[...], preferred_element_type=F32)
    o_ref[0] = r[:, :O_POOL]
    p_ref[0] = r[:, O_POOL:]


def _in_proj(x, nw, shift, scale, w_bf16):
    b, t, d = x.shape
    tm = min(512, t)
    return pl.pallas_call(
        _inproj_kernel,
        grid=(b, t // tm),
        in_specs=[pl.BlockSpec((1, tm, d), lambda i, j: (i, j, 0)),
                  pl.BlockSpec((1, d), lambda i, j: (0, 0)),
                  pl.BlockSpec((1, 1, d), lambda i, j: (i, 0, 0)),
                  pl.BlockSpec((1, 1, d), lambda i, j: (i, 0, 0)),
                  pl.BlockSpec((d, D_PROJ), lambda i, j: (0, 0))],
        out_specs=[pl.BlockSpec((1, tm, O_POOL), lambda i, j: (i, j, 0)),
                   pl.BlockSpec((1, tm, POOL_WIDTH), lambda i, j: (i, j, 0))],
        out_shape=[jax.ShapeDtypeStruct((b, t, O_POOL), F32),
                   jax.ShapeDtypeStruct((b, t, POOL_WIDTH), F32)],
        compiler_params=_cparams("parallel", "parallel"),
        name="in_proj",
    )(x, nw.reshape(1, d), shift, scale, w_bf16)


def _softmax_pv(s_list, v_list):
    m = s_list[0].max(axis=-1, keepdims=True)
    for s in s_list[1:]:
        m = jnp.maximum(m, s.max(axis=-1, keepdims=True))
    num = None
    den = None
    for s, v in zip(s_list, v_list):
        p = jnp.exp(s - m)
        pv = jnp.dot(p.astype(BF16), v.astype(BF16), preferred_element_type=F32)
        ps = p.sum(axis=-1, keepdims=True)
        num = pv if num is None else num + pv
        den = ps if den is None else den + ps
    return num / den


NA_ROWS_PER_TRIP = 4


def _na_kernel(q_ref, k_ref, v_ref, kc_ref, vc_ref, bias_ref, o_ref, *, rows, rb):
    i = pl.program_id(2)
    lane = lax.broadcasted_iota(jnp.int32, (1, LANES), 1)
    first = lane < HEAD_DIM
    kc = kc_ref[0].astype(BF16)
    vc = vc_ref[0].astype(BF16)
    scale = HEAD_DIM ** -0.5
    nk = NA_WIN_R * GRID_W

    def body(it, carry):
        for u in range(NA_ROWS_PER_TRIP):
            one_row(it * NA_ROWS_PER_TRIP + u)
        return carry

    def one_row(rr):
        r = i * rb + rr
        rs = jnp.clip(r - NA_WIN_R // 2, 0, rows - NA_WIN_R)
        delta = rs - r + (NA_WIN_R - 1)
        q = q_ref[0, pl.ds(pl.multiple_of(rr * GRID_W, GRID_W), GRID_W), :]
        k = k_ref[0, pl.ds(pl.multiple_of(rs * GRID_W, GRID_W), nk), :].astype(BF16)
        v = v_ref[0, pl.ds(pl.multiple_of(rs * GRID_W, GRID_W), nk), :].astype(BF16)
        q2 = jnp.concatenate([jnp.where(first, q, 0.0), jnp.where(first, 0.0, q)], axis=0).astype(BF16)
        s = _dot_nt(q2, k) * scale + bias_ref[0, delta]
        sc = _dot_nt(q2, kc) * scale
        o2 = _softmax_pv([s, sc], [v, vc])
        o_ref[0, pl.ds(pl.multiple_of(rr * GRID_W, GRID_W), GRID_W), :] = jnp.where(
            first, o2[0:GRID_W], o2[GRID_W:2 * GRID_W])

    lax.fori_loop(0, rb // NA_ROWS_PER_TRIP, body, 0)


def _na_attention(px, pc, bias8):
    b, s, _ = px.shape
    c = pc.shape[1]
    rows = s // GRID_W
    assert rows >= NA_WIN_R and s % GRID_W == 0
    rb = min(16, rows)
    nq, nkb, nvb = 0, HEAD_PAIRS, 2 * HEAD_PAIRS
    return pl.pallas_call(
        functools.partial(_na_kernel, rows=rows, rb=rb),
        grid=(b, HEAD_PAIRS, rows // rb),
        in_specs=[pl.BlockSpec((1, rb * GRID_W, LANES), lambda bi, hp, i: (bi, i, nq + hp)),
                  pl.BlockSpec((1, s, LANES), lambda bi, hp, i: (bi, 0, nkb + hp)),
                  pl.BlockSpec((1, s, LANES), lambda bi, hp, i: (bi, 0, nvb + hp)),
                  pl.BlockSpec((1, c, LANES), lambda bi, hp, i: (bi, 0, nkb + hp)),
                  pl.BlockSpec((1, c, LANES), lambda bi, hp, i: (bi, 0, nvb + hp)),
                  pl.BlockSpec((1, NA_WIN_R, 2 * GRID_W, NA_WIN_R * GRID_W),
                               lambda bi, hp, i: (hp, 0, 0, 0))],
        out_specs=pl.BlockSpec((1, rb * GRID_W, LANES), lambda bi, hp, i: (bi, i, hp)),
        out_shape=jax.ShapeDtypeStruct((b, s, NA_WIDTH), F32),
        compiler_params=_cparams("parallel", "parallel", "arbitrary"),
        name="na_attention",
    )(px, px, px, pc, pc, bias8)


def _ctx_attn_kernel(q_ref, k_ref, v_ref, o_ref):
    lane = lax.broadcasted_iota(jnp.int32, (1, LANES), 1)
    first = lane < HEAD_DIM
    q = q_ref[0]
    k = k_ref[0]
    v = v_ref[0]
    outs = []
    for h in range(2):
        qm = jnp.where(first if h == 0 else jnp.logical_not(first), q, 0.0)
        s = _dot_nt(qm, k) * HEAD_DIM ** -0.5
        outs.append(_softmax_pv([s], [v]))
    o_ref[0] = jnp.where(first, outs[0], outs[1])


def _ctx_attention(pc):
    b, c, _ = pc.shape
    return pl.pallas_call(
        _ctx_attn_kernel,
        grid=(b, HEAD_PAIRS),
        in_specs=[pl.BlockSpec((1, c, LANES), lambda bi, hp: (bi, 0, hp)),
                  pl.BlockSpec((1, c, LANES), lambda bi, hp: (bi, 0, HEAD_PAIRS + hp)),
                  pl.BlockSpec((1, c, LANES), lambda bi, hp: (bi, 0, 2 * HEAD_PAIRS + hp))],
        out_specs=pl.BlockSpec((1, c, LANES), lambda bi, hp: (bi, 0, hp)),
        out_shape=jax.ShapeDtypeStruct((b, c, NA_WIDTH), F32),
        compiler_params=_cparams("parallel", "parallel"),
        name="ctx_attention",
    )(pc, pc, pc)


RET_CHUNKS_PER_STEP = 8


def _ret_kernel(lg_ref, qf_ref, kf_ref, vf_ref, qb_ref, kb_ref, vb_ref,
                cf_ref, sf_ref, cb_ref, sb_ref, rf0_ref, rb0_ref,
                yf_ref, yb_ref, rfo_ref, rbo_ref, rf_scr, rb_scr, *, cpb):
    hp = pl.program_id(1)
    c = pl.program_id(2)
    nc = pl.num_programs(2)
    cs = RET_CHUNK

    @pl.when(c == 0)
    def _():
        rf_scr[...] = rf0_ref[0, 0]
        rb_scr[...] = rb0_ref[0, 0]

    lane = lax.broadcasted_iota(jnp.int32, (1, LANES), 1)
    first = lane < HEAD_DIM
    low = (lane % (2 * ROPE_PAIRS)) < ROPE_PAIRS
    pos = lax.broadcasted_iota(jnp.int32, (cs, 1), 0).astype(F32)
    ii = lax.broadcasted_iota(jnp.int32, (cs, cs), 0)
    jj = lax.broadcasted_iota(jnp.int32, (cs, cs), 1)
    diff = (ii - jj).astype(F32)
    same_head = (ii < HEAD_DIM) == (jj < HEAD_DIM)
    scale = HEAD_DIM ** -0.5

    def rope(x, cos, sin):
        swapped = jnp.where(low, pltpu.roll(x, LANES - ROPE_PAIRS, 1), pltpu.roll(x, ROPE_PAIRS, 1))
        return x * cos + swapped * sin

    def direction(d, q_ref, k_ref, v_ref, cos_ref, sin_ref, r_scr, y_ref):
        lg0 = lg_ref[d, 2 * hp]
        lg1 = lg_ref[d, 2 * hp + 1]
        lgv = jnp.where(first, lg0, lg1)
        decs = []
        for lg in (lg0, lg1):
            if d == 0:
                decs.append(jnp.where(diff >= 0, jnp.exp(jnp.maximum(diff, 0.0) * lg), 0.0))
            else:
                decs.append(jnp.where(diff <= 0, jnp.exp(jnp.maximum(-diff, 0.0) * lg), 0.0))
        dec2 = jnp.concatenate(decs, axis=0)
        if d == 0:
            xi = jnp.exp((pos + 1.0) * lgv)
            zeta = jnp.exp((cs - 1.0 - pos) * lgv)
        else:
            xi = jnp.exp((cs - pos) * lgv)
            zeta = jnp.exp(pos * lgv)
        chunk_decay = jnp.exp(cs * lgv)
        r = r_scr[...]
        for j in (range(cpb) if d == 0 else reversed(range(cpb))):
            rows = pl.ds(j * cs, cs)
            cos = cos_ref[rows, :]
            sin = sin_ref[rows, :]
            q = rope(q_ref[0, rows, :], cos, sin)
            k = rope(k_ref[0, rows, :], cos, sin) * scale
            v = v_ref[0, rows, :].astype(BF16)
            q2 = jnp.concatenate([jnp.where(first, q, 0.0), jnp.where(first, 0.0, q)], axis=0).astype(BF16)
            s2 = _dot_nt(q2, k.astype(BF16)) * dec2
            o2 = jnp.dot(s2.astype(BF16), v, preferred_element_type=F32)
            inner = jnp.where(first, o2[0:cs], o2[cs:2 * cs])
            y_ref[0, rows, :] = inner + jnp.dot((q * xi).astype(BF16), r.astype(BF16),
                                                preferred_element_type=F32)
            kv = lax.dot_general((k * zeta).astype(BF16), v, (((0,), (0,)), ((), ())),
                                 preferred_element_type=F32)
            r = chunk_decay * r + jnp.where(same_head, kv, 0.0)
        r_scr[...] = r

    direction(0, qf_ref, kf_ref, vf_ref, cf_ref, sf_ref, rf_scr, yf_ref)
    direction(1, qb_ref, kb_ref, vb_ref, cb_ref, sb_ref, rb_scr, yb_ref)

    @pl.when(c == nc - 1)
    def _():
        rfo_ref[0, 0] = rf_scr[...]
        rbo_ref[0, 0] = rb_scr[...]


def _retention(p, lg, cos_t, sin_t, rf0, rb0):
    b, t, _ = p.shape
    cpb = min(RET_CHUNKS_PER_STEP, t // RET_CHUNK)
    nc = t // (RET_CHUNK * cpb)
    qo, ko, vo = 3 * HEAD_PAIRS, 4 * HEAD_PAIRS, 5 * HEAD_PAIRS
    blk = (1, RET_CHUNK * cpb, LANES)
    fwd = lambda o: pl.BlockSpec(blk, lambda bi, hp, c: (bi, c, o + hp))
    bwd = lambda o: pl.BlockSpec(blk, lambda bi, hp, c: (bi, nc - 1 - c, o + hp))
    tab_f = pl.BlockSpec((RET_CHUNK * cpb, LANES), lambda bi, hp, c: (c, 0))
    tab_b = pl.BlockSpec((RET_CHUNK * cpb, LANES), lambda bi, hp, c: (nc - 1 - c, 0))
    st = pl.BlockSpec((1, 1, LANES, LANES), lambda bi, hp, c: (bi, hp, 0, 0))
    return pl.pallas_call(
        functools.partial(_ret_kernel, cpb=cpb),
        grid=(b, HEAD_PAIRS, nc),
        in_specs=[pl.BlockSpec(memory_space=pltpu.SMEM),
                  fwd(qo), fwd(ko), fwd(vo), bwd(qo), bwd(ko), bwd(vo),
                  tab_f, tab_f, tab_b, tab_b, st, st],
        out_specs=[pl.BlockSpec(blk, lambda bi, hp, c: (bi, c, hp)),
                   pl.BlockSpec(blk, lambda bi, hp, c: (bi, nc - 1 - c, hp)),
                   st, st],
        out_shape=[jax.ShapeDtypeStruct((b, t, RET_WIDTH), F32),
                   jax.ShapeDtypeStruct((b, t, RET_WIDTH), F32),
                   jax.ShapeDtypeStruct((b, HEAD_PAIRS, LANES, LANES), F32),
                   jax.ShapeDtypeStruct((b, HEAD_PAIRS, LANES, LANES), F32)],
        scratch_shapes=[pltpu.VMEM((LANES, LANES), F32), pltpu.VMEM((LANES, LANES), F32)],
        compiler_params=_cparams("parallel", "parallel", "arbitrary"),
        name="retention",
    )(lg, p, p, p, p, p, p, cos_t, sin_t, cos_t, sin_t, rf0, rb0)


def _pool_kernel(prev_ref, cur_ref, next_ref, w_ref, s_ref, o_ref, scr, *, t_total, tp):
    i = pl.program_id(1)
    n = pl.num_programs(1)
    halo = POOL_WINDOWS[-1] // 2
    x = cur_ref[0]
    scr[0:halo, :] = jnp.where(i > 0, prev_ref[0], 0.0)
    scr[halo:halo + tp, :] = x
    scr[halo + tp:2 * halo + tp, :] = jnp.where(i < n - 1, next_ref[0], 0.0)
    t = i * tp + lax.broadcasted_iota(jnp.int32, (tp, 1), 0)
    lane = lax.broadcasted_iota(jnp.int32, (1, POOL_WIDTH), 1)

    def shifted(s):
        return scr[halo + s:halo + s + tp, :]

    acc = None
    mean = None
    done = 0
    for g, w in enumerate(POOL_WINDOWS):
        half = w // 2
        for s in list(range(-half, -done)) + list(range(done, half)):
            sh = x if s == 0 else shifted(s)
            acc = sh if acc is None else acc + sh
        done = half
        cnt = (jnp.minimum(t + half, t_total) - jnp.maximum(t - half, 0)).astype(F32)
        mg = acc / cnt
        mean = mg if mean is None else jnp.where(lane >= g * POOL_GROUP, mg, mean)
    dlt = (mean - x).astype(BF16)
    o_ref[0] = jnp.dot(dlt, w_ref[...], preferred_element_type=F32) * s_ref[...]


def _pool(pin, w_bd_bf16, scale):
    b, t, _ = pin.shape
    tp = min(1024, t)
    halo = POOL_WINDOWS[-1] // 2
    nh = tp // halo
    last = t // halo - 1
    return pl.pallas_call(
        functools.partial(_pool_kernel, t_total=t, tp=tp),
        grid=(b, t // tp),
        in_specs=[pl.BlockSpec((1, halo, POOL_WIDTH), lambda bi, i: (bi, jnp.maximum(i * nh - 1, 0), 0)),
                  pl.BlockSpec((1, tp, POOL_WIDTH), lambda bi, i: (bi, i, 0)),
                  pl.BlockSpec((1, halo, POOL_WIDTH), lambda bi, i: (bi, jnp.minimum((i + 1) * nh, last), 0)),
                  pl.BlockSpec((POOL_WIDTH, POOL_WIDTH), lambda bi, i: (0, 0)),
                  pl.BlockSpec((1, POOL_WIDTH), lambda bi, i: (0, 0))],
        out_specs=pl.BlockSpec((1, tp, POOL_WIDTH), lambda bi, i: (bi, i, 0)),
        out_shape=jax.ShapeDtypeStruct((b, t, POOL_WIDTH), F32),
        scratch_shapes=[pltpu.VMEM((tp + 2 * halo, POOL_WIDTH), F32)],
        compiler_params=_cparams("parallel", "parallel"),
        name="multiscale_pool",
    )(pin, pin, pin, w_bd_bf16, scale.reshape(1, POOL_WIDTH))


def _out_kernel(na_ref, yf_ref, yb_ref, g_ref, pool_ref, x_ref, g1_ref, gnw_ref, avg_ref, wo_ref, o_ref):
    y = yf_ref[0] + yb_ref[0]
    avg = avg_ref[...]

    def group_mean(a):
        hi = a.astype(BF16)
        rest = a - hi.astype(F32)
        mid = rest.astype(BF16)
        lo = (rest - mid.astype(F32)).astype(BF16)
        return (jnp.dot(hi, avg, preferred_element_type=F32) + jnp.dot(mid, avg, preferred_element_type=F32)
                + jnp.dot(lo, avg, preferred_element_type=F32))

    mu = group_mean(y)
    d = y - mu
    var = group_mean(d * d)
    yn = d * lax.rsqrt(var + NORM_EPS) * gnw_ref[...]
    g = g_ref[0]
    ret = yn * (g * jax.nn.sigmoid(g))
    mix = jnp.dot(na_ref[0].astype(BF16), wo_ref[0:NA_WIDTH, :], preferred_element_type=F32)
    mix += jnp.dot(ret.astype(BF16), wo_ref[NA_WIDTH:NA_WIDTH + RET_WIDTH, :], preferred_element_type=F32)
    mix += jnp.dot(pool_ref[0].astype(BF16), wo_ref[NA_WIDTH + RET_WIDTH:, :], preferred_element_type=F32)
    o_ref[0] = x_ref[0] + g1_ref[0] * mix


def _out_proj(na, yf, yb, p, pool, x, g1, gn_w, avg, wo_bf16):
    b, t, d = x.shape
    tm = min(512, t)
    row = lambda w: pl.BlockSpec((1, tm, w), lambda i, j: (i, j, 0))
    return pl.pallas_call(
        _out_kernel,
        grid=(b, t // tm),
        in_specs=[row(NA_WIDTH), row(RET_WIDTH), row(RET_WIDTH),
                  pl.BlockSpec((1, tm, RET_WIDTH), lambda i, j: (i, j, O_RET_G // RET_WIDTH)),
                  row(POOL_WIDTH), row(d),
                  pl.BlockSpec((1, 1, d), lambda i, j: (i, 0, 0)),
                  pl.BlockSpec((1, RET_WIDTH), lambda i, j: (0, 0)),
                  pl.BlockSpec((RET_WIDTH, RET_WIDTH), lambda i, j: (0, 0)),
                  pl.BlockSpec((d, d), lambda i, j: (0, 0))],
        out_specs=row(d),
        out_shape=jax.ShapeDtypeStruct((b, t, d), F32),
        compiler_params=_cparams("parallel", "parallel"),
        name="out_proj",
    )(na, yf, yb, p, pool, x, g1, gn_w.reshape(1, RET_WIDTH), avg, wo_bf16)


def _topk_rows(s, order=None, payload=None):
    n, m = s.shape
    if order is None:
        order = lax.broadcasted_iota(jnp.int32, (n, m), 0)
    krow = lax.broadcasted_iota(jnp.int32, (PEER_TOPK, m), 0)
    vals = jnp.zeros((PEER_TOPK, m), F32)
    idxs = jnp.zeros((PEER_TOPK, m), jnp.int32)
    for k in range(PEER_TOPK):
        mx = jnp.max(s, axis=0, keepdims=True)
        am = jnp.min(jnp.where(s == mx, order, jnp.iinfo(jnp.int32).max), axis=0, keepdims=True)
        sel = order == am
        pick = am if payload is None else jnp.sum(jnp.where(sel, payload, 0), axis=0, keepdims=True)
        vals = jnp.where(krow == k, mx, vals)
        idxs = jnp.where(krow == k, pick, idxs)
        s = jnp.where(sel, -jnp.inf, s)
    return vals, idxs


def _product_candidates(va, ia, vb, ib):
    k = PEER_TOPK
    sub = 8
    m = va.shape[1]
    row = lax.broadcasted_iota(jnp.int32, (sub, m), 0)
    cand, flat, eid = [], [], []

    def add(v, f, e, nvalid):
        cand.append(v if nvalid >= sub else jnp.where(row < nvalid, v, -jnp.inf))
        flat.append(f)
        eid.append(e)

    for i in range(sub):
        add(va[i:i + 1, :] + vb[0:sub, :], i * k + row, ia[i:i + 1, :] * PEER_NKEYS + ib[0:sub, :], k // (i + 1))
    add(va[0:1, :] + vb[sub:k, :], sub + row, ia[0:1, :] * PEER_NKEYS + ib[sub:k, :], sub)
    add(va[sub:k, :] + vb[0:1, :], (sub + row) * k, ia[sub:k, :] * PEER_NKEYS + ib[0:1, :], sub)
    return jnp.concatenate(cand, axis=0), jnp.concatenate(flat, axis=0), jnp.concatenate(eid, axis=0)


def _route_kernel(x_ref, nw_ref, sh_ref, sc_ref, wq_ref, keys_ref, idx_ref, gate_ref, q_scr, g_scr, i_scr):
    h_hi, h_lo = _split_bf16(_rms_mod(x_ref[0], nw_ref[...], sh_ref[0], sc_ref[0]))
    q = (jnp.dot(h_hi, wq_ref[0], preferred_element_type=F32)
         + jnp.dot(h_hi, wq_ref[1], preferred_element_type=F32)
         + jnp.dot(h_lo, wq_ref[0], preferred_element_type=F32))
    for j in range(2 * PEER_HEADS):
        q_scr[j] = q[:, j * PEER_KEY_DIM:(j + 1) * PEER_KEY_DIM]

    def scores(half, hh):
        q_hi, q_lo = _split_bf16(q_scr[2 * hh + half])
        k_hi = keys_ref[0, half, hh]
        return _dot_nt(k_hi, q_hi) + _dot_nt(k_hi, q_lo) + _dot_nt(keys_ref[1, half, hh], q_hi)

    def head(hh, carry):
        sa = scores(0, hh)
        sb = scores(1, hh)
        va, ia = _topk_rows(sa)
        vb, ib = _topk_rows(sb)
        cand, flat, eid = _product_candidates(va, ia, vb, ib)
        sc, ei = _topk_rows(cand, order=flat, payload=eid)
        e = jnp.exp(sc - sc[0:1, :])
        off = pl.multiple_of(hh * PEER_TOPK, PEER_TOPK)
        g_scr[pl.ds(off, PEER_TOPK), :] = e / jnp.sum(e, axis=0, keepdims=True)
        i_scr[pl.ds(off, PEER_TOPK), :] = ei
        return carry

    def head_pair(i, carry):
        head(2 * i, carry)
        return head(2 * i + 1, carry)

    lax.fori_loop(0, PEER_HEADS // 2, head_pair, 0)
    gate_ref[0] = g_scr[...].T
    idx_ref[0] = i_scr[...].T * TABLE_ROWS_PER_EXPERT


def _route(x, nw, shift, scale, wq, keys):
    b, t, d = x.shape
    tm = min(1024, t)
    nt = t // tm
    nq = wq.shape[2]
    return pl.pallas_call(
        _route_kernel,
        grid=(b, nt),
        in_specs=[pl.BlockSpec((1, tm, d), lambda i, j: (i, j, 0)),
                  pl.BlockSpec((1, d), lambda i, j: (0, 0)),
                  pl.BlockSpec((1, 1, d), lambda i, j: (i, 0, 0)),
                  pl.BlockSpec((1, 1, d), lambda i, j: (i, 0, 0)),
                  pl.BlockSpec((2, d, nq), lambda i, j: (0, 0, 0)),
                  pl.BlockSpec((2, 2, PEER_HEADS, PEER_NKEYS, PEER_KEY_DIM), lambda i, j: (0, 0, 0, 0, 0))],
        out_specs=[pl.BlockSpec((1, tm, N_SEL), lambda i, j: (i, j, 0)),
                   pl.BlockSpec((1, tm, N_SEL), lambda i, j: (i, j, 0))],
        out_shape=[jax.ShapeDtypeStruct((b, t, N_SEL), jnp.int32),
                   jax.ShapeDtypeStruct((b, t, N_SEL), F32)],
        scratch_shapes=[pltpu.VMEM((2 * PEER_HEADS, tm, PEER_KEY_DIM), F32),
                        pltpu.VMEM((N_SEL, tm), F32),
                        pltpu.VMEM((N_SEL, tm), jnp.int32)],
        compiler_params=_cparams("parallel", "parallel"),
        name="peer_route",
    )(x, nw.reshape(1, d), shift, scale, wq, keys)


def _pack_kernel(t_ref, o_ref, *, te):
    x = t_ref[0]
    d = x.shape[1]

    def bf16_bits(a):
        return lax.bitcast_convert_type(a.astype(BF16).astype(F32), jnp.uint32)

    words = (bf16_bits(x[:, :d // 2]) >> 16) | bf16_bits(x[:, d // 2:])
    for q in range(TABLE_ROWS_PER_EXPERT):
        o_ref[pl.ds(q, te, stride=TABLE_ROWS_PER_EXPERT), :] = words[:, q * LANES:(q + 1) * LANES]


def _pack_table(tabs, layer):
    _, n, d = tabs.shape
    te = 512
    return pl.pallas_call(
        functools.partial(_pack_kernel, te=te),
        grid=(n // te,),
        in_specs=[pl.BlockSpec((1, te, d), lambda i: (layer, i, 0))],
        out_specs=pl.BlockSpec((te * TABLE_ROWS_PER_EXPERT, LANES), lambda i: (i, 0)),
        out_shape=jax.ShapeDtypeStruct((n * TABLE_ROWS_PER_EXPERT, LANES), jnp.uint32),
        compiler_params=_cparams("parallel"),
        name="pack_table",
    )(tabs)


def _token_loop(tb, token, per_trip):
    def trip(i, carry):
        for j in range(per_trip):
            token(i * per_trip + j, j)
        return carry

    lax.fori_loop(0, tb // per_trip, trip, 0)


def _unpack(slab):
    lo = lax.bitcast_convert_type(slab << 16, F32)
    hi = lax.bitcast_convert_type(slab & jnp.uint32(0xFFFF0000), F32)
    return lo, hi


def _peer_act_kernel(idx_ref, tab_ref, x_ref, nw_ref, sh_ref, sc_ref, gate_ref, w_ref, h_scr, act_scr, *, tb):
    h = _rms_mod(x_ref[...], nw_ref[...], sh_ref[0], sc_ref[0])
    half = CHUNKS // 2
    for c in range(CHUNKS):
        h_scr[pl.ds(c, tb, stride=CHUNKS), :] = h[:, c * LANES:(c + 1) * LANES]
    sub = lax.broadcasted_iota(jnp.int32, (CHUNKS, N_SEL), 0)
    lane = lax.broadcasted_iota(jnp.int32, (CHUNKS, N_SEL), 1)
    slot = lane - PAIR_STRIDE * (sub >= half).astype(jnp.int32)

    def token(t, _):
        ht = h_scr[pl.ds(pl.multiple_of(t * CHUNKS, CHUNKS), CHUNKS), :]
        hlo = jnp.concatenate([ht[0:half], ht[0:half]], axis=0)
        hhi = jnp.concatenate([ht[half:CHUNKS], ht[half:CHUNKS]], axis=0)
        acc = jnp.zeros((CHUNKS, N_SEL), F32)
        for a, slab in _slab_pairs(tab_ref, idx_ref, t):
            lo, hi = _unpack(slab)
            dot = jnp.sum(lo * hlo + hi * hhi, axis=-1, keepdims=True)
            acc = jnp.where(slot == a, dot, acc)
        act_scr[pl.ds(t, 1), :] = jnp.sum(acc, axis=0, keepdims=True)

    _token_loop(tb, token, 32)
    a = act_scr[...]
    w_ref[...] = gate_ref[...] * (0.5 * a * (1.0 + lax.erf(a * (2.0 ** -0.5))))


def _slab_pairs(tab_ref, idx_ref, t):
    views = [idx_ref.at[pl.ds(t * N_SEL + q * PAIR_STRIDE, PAIR_STRIDE)] for q in range(4)]
    for k in range(PAIR_STRIDE):
        rows = [pl.multiple_of(v[k], TABLE_ROWS_PER_EXPERT) for v in views]
        for q in (0, 2):
            yield q * PAIR_STRIDE + k, jnp.concatenate(
                [tab_ref[pl.ds(rows[q], TABLE_ROWS_PER_EXPERT), :],
                 tab_ref[pl.ds(rows[q + 1], TABLE_ROWS_PER_EXPERT), :]], axis=0)


def _peer_out_kernel(pk_ref, tab_ref, x_ref, g2_ref, fw_ref, o_ref, p_scr, *, tb, final):
    half = CHUNKS // 2
    upper = lax.broadcasted_iota(jnp.int32, (CHUNKS, LANES), 0) >= half

    def token(t, _):
        views = [pk_ref.at[pl.ds(t * N_SEL + q * PAIR_STRIDE, PAIR_STRIDE)] for q in range(4)]
        accs = [jnp.zeros((CHUNKS, LANES), F32) for _ in range(4)]
        for k in range(PAIR_STRIDE):
            words = [v[k] for v in views]
            for q in (0, 2):
                rows = [pl.multiple_of(words[q + i] & 0xFFFF, TABLE_ROWS_PER_EXPERT) for i in range(2)]
                slab = jnp.concatenate([tab_ref[pl.ds(r, TABLE_ROWS_PER_EXPERT), :] for r in rows], axis=0)
                lo, hi = _unpack(slab)
                both = jnp.where(upper, words[q + 1], words[q])
                wv = lax.bitcast_convert_type(both & jnp.int32(-65536), F32)
                accs[q] = accs[q] + wv * lo
                accs[q + 1] = accs[q + 1] + wv * hi
        alo = accs[0] + accs[2]
        ahi = accs[1] + accs[3]
        off = pl.multiple_of(t * CHUNKS, CHUNKS)
        p_scr[pl.ds(off, half), :] = alo[0:half] + alo[half:CHUNKS]
        p_scr[pl.ds(pl.multiple_of(off + half, half), half), :] = ahi[0:half] + ahi[half:CHUNKS]

    _token_loop(tb, token, 2)
    peer = jnp.concatenate([p_scr[pl.ds(c, tb, stride=CHUNKS), :] for c in range(CHUNKS)], axis=1)
    y = x_ref[...] + g2_ref[0] * peer
    if final:
        y = y * lax.rsqrt(jnp.mean(y * y, axis=-1, keepdims=True) + NORM_EPS) * fw_ref[...]
    o_ref[...] = y


def _peer_residual(x, nw, shift, scale, g2, wq, keys, u, v, final_w, final):
    b, t, d = x.shape
    assert d == CHUNKS * LANES
    idx, gate = _route(x, nw, shift, scale, wq, keys)
    n = b * t
    tb = TOK_BLOCK
    per_batch = t // tb
    x2 = x.reshape(n, d)
    idx1 = idx.reshape(n * N_SEL)
    smem_blk = pl.BlockSpec((tb * N_SEL,), lambda i: (i,), memory_space=pltpu.SMEM)
    table = pl.BlockSpec(u.shape, lambda i: (0, 0), pipeline_mode=pl.Buffered(1))
    rows = pl.BlockSpec((tb, d), lambda i: (i, 0))
    sel = pl.BlockSpec((tb, N_SEL), lambda i: (i, 0))
    vec = pl.BlockSpec((1, 1, d), lambda i: (i // per_batch, 0, 0))
    const = pl.BlockSpec((1, d), lambda i: (0, 0))
    w = pl.pallas_call(
        functools.partial(_peer_act_kernel, tb=tb),
        grid=(n // tb,),
        in_specs=[smem_blk, table, rows, const, vec, vec, sel],
        out_specs=sel,
        out_shape=jax.ShapeDtypeStruct((n, N_SEL), F32),
        scratch_shapes=[pltpu.VMEM((tb * CHUNKS, LANES), F32), pltpu.VMEM((tb, N_SEL), F32)],
        compiler_params=_cparams("arbitrary"),
        name="peer_act",
    )(idx1, u, x2, nw.reshape(1, d), shift, scale, gate.reshape(n, N_SEL))
    w_bits = lax.bitcast_convert_type(w.astype(BF16).astype(F32), jnp.int32) & jnp.int32(-65536)
    packed = (w_bits | idx.reshape(n, N_SEL)).reshape(n * N_SEL)
    y = pl.pallas_call(
        functools.partial(_peer_out_kernel, tb=tb, final=final),
        grid=(n // tb,),
        in_specs=[smem_blk, table, rows, vec, const],
        out_specs=rows,
        out_shape=jax.ShapeDtypeStruct((n, d), F32),
        scratch_shapes=[pltpu.VMEM((tb * CHUNKS, LANES), F32)],
        compiler_params=_cparams("arbitrary"),
        name="peer_out",
    )(packed, v, x2, g2, final_w.reshape(1, d))
    return y.reshape(b, t, d)


def _na_bias_table(rpb):
    cq = jnp.arange(GRID_W)
    coff = jnp.clip(cq[None, :] - cq[:, None] + (NA_WIN_C - 1), 0, 2 * NA_WIN_C - 2)
    c_start = jnp.clip(cq - NA_WIN_C // 2, 0, GRID_W - NA_WIN_C)
    ok = (cq[None, :] >= c_start[:, None]) & (cq[None, :] < c_start[:, None] + NA_WIN_C)
    roff = jnp.arange(NA_WIN_R)[:, None] + jnp.arange(NA_WIN_R)[None, :]
    bias = rpb.astype(F32)[:, :, coff][:, roff]
    bias = jnp.where(ok[None, None, None], bias, NEG_INF)
    bias = bias.transpose(0, 1, 3, 2, 4).reshape(HEAD_PAIRS, 2, NA_WIN_R, GRID_W, NA_WIN_R * GRID_W)
    return bias.transpose(0, 2, 1, 3, 4).reshape(HEAD_PAIRS, NA_WIN_R, 2 * GRID_W, NA_WIN_R * GRID_W)


def _rope_tables(s):
    t = jnp.arange(s)
    row = (t // GRID_W).astype(F32)
    col = (t % GRID_W).astype(F32)
    inv = ROPE_BASE ** (-jnp.arange(ROPE_PAIRS, dtype=F32) / ROPE_PAIRS)
    cr, sr = jnp.cos(row[:, None] * inv), jnp.sin(row[:, None] * inv)
    cc, sc = jnp.cos(col[:, None] * inv), jnp.sin(col[:, None] * inv)
    cos = jnp.concatenate([cr, cr, cc, cc], axis=-1)
    sin = jnp.concatenate([-sr, sr, -sc, sc], axis=-1)
    return jnp.tile(cos, (1, 2)), jnp.tile(sin, (1, 2))


def _block_diag(blocks):
    n, a, bb = blocks.shape
    eye = jnp.eye(n, dtype=blocks.dtype)
    return (eye[:, None, :, None] * blocks[:, :, None, :]).reshape(n * a, n * bb)


def kernel(x, c, ctx, c_ctx, norm1_w, norm2_w, w_ada, b_ada, w_in, w_out, na_rpb, ret_decay_fwd, ret_decay_bwd, ret_gn_w, pool_w, pool_scale, peer_wq, peer_keys, peer_u, peer_v, final_norm_w):
    b, s, d = x.shape
    depth = w_in.shape[0]
    clen = ctx.shape[1]
    cvec = jnp.concatenate([c, c_ctx[None, :], jnp.zeros((8 - b - 1, d), F32)], axis=0)
    mod = _modulation(cvec, w_ada, b_ada)
    cos_x, sin_x = _rope_tables(s)
    cos_c, sin_c = jnp.ones((clen, LANES), F32), jnp.zeros((clen, LANES), F32)
    avg = _block_diag(jnp.full((RET_HEADS, HEAD_DIM, HEAD_DIM), 1.0 / HEAD_DIM, BF16))
    zero_state = jnp.zeros((b, HEAD_PAIRS, LANES, LANES), F32)

    for l in range(depth):
        last = l == depth - 1
        mx = mod[l, :b].reshape(b, 1, 6, d)
        mc = jnp.broadcast_to(mod[l, b].reshape(1, 1, 6, d), (b, 1, 6, d))
        sh1, sc1, g1, sh2, sc2, g2 = [mx[:, :, i] for i in range(6)]
        csh1, csc1, cg1, csh2, csc2, cg2 = [mc[:, :, i] for i in range(6)]
        lg = jnp.stack([jax.nn.log_sigmoid(ret_decay_fwd[l].astype(F32)),
                        jax.nn.log_sigmoid(ret_decay_bwd[l].astype(F32))], axis=0)
        wi = w_in[l].astype(BF16)
        wo = w_out[l].astype(BF16)
        wpool = _block_diag(pool_w[l]).astype(BF16)
        bias8 = _na_bias_table(na_rpb[l])

        pc, pcp = _in_proj(ctx, norm1_w[l], csh1, csc1, wi)
        ycf, ycb, r_f, r_b = _retention(pc, lg, cos_c, sin_c, zero_state, zero_state)

        px, pxp = _in_proj(x, norm1_w[l], sh1, sc1, wi)
        na = _na_attention(px, pc, bias8)
        yf, yb, _, _ = _retention(px, lg, cos_x, sin_x, r_f, r_b)
        pool = _pool(pxp, wpool, pool_scale[l])
        x = _out_proj(na, yf, yb, px, pool, x, g1, ret_gn_w[l], avg, wo)
        wq = jnp.stack(_split_bf16(peer_wq[l]))
        keys = jnp.stack(_split_bf16(peer_keys[l]))
        u_tab = _pack_table(peer_u, l)
        v_tab = _pack_table(peer_v, l)
        x = _peer_residual(x, norm2_w[l], sh2, sc2, g2, wq, keys, u_tab, v_tab, final_norm_w, last)

        if not last:
            na_c = _ctx_attention(pc)
            pool_c = _pool(pcp, wpool, pool_scale[l])
            ctx = _out_proj(na_c, ycf, ycb, pc, pool_c, ctx, cg1, ret_gn_w[l], avg, wo)
            ctx = _peer_residual(ctx, norm2_w[l], csh2, csc2, cg2, wq, keys, u_tab, v_tab, final_norm_w, False)
    return x
```

```python
import functools

import jax
import jax.numpy as jnp
from jax import lax
from jax.experimental import pallas as pl
from jax.experimental.pallas import tpu as pltpu

D_MODEL = 1024
GRID_W = 64
HEAD_DIM = 64
NA_HEADS = 6
NA_WIN_R = 8
NA_WIN_C = 16
RET_HEADS = 6
RET_CHUNK = 128
POOL_WINDOWS = (2, 4, 8, 16)
POOL_GROUP = 64
NA_WIDTH = NA_HEADS * HEAD_DIM
RET_WIDTH = RET_HEADS * HEAD_DIM
POOL_WIDTH = POOL_GROUP * len(POOL_WINDOWS)
O_RET_G = 3 * NA_WIDTH + 3 * RET_WIDTH
O_POOL = O_RET_G + RET_WIDTH
D_PROJ = O_POOL + POOL_WIDTH
ROPE_BASE = 10000.0
ROPE_PAIRS = HEAD_DIM // 4
PEER_HEADS = 8
PEER_NKEYS = 128
PEER_KEY_DIM = 128
PEER_TOPK = 16
N_SEL = PEER_HEADS * PEER_TOPK
NORM_EPS = 1e-6
NEG_INF = -1e30

LANES = 128
VMEM_LIMIT_BYTES = 56 * 1024 * 1024

F32 = jnp.float32
BF16 = jnp.bfloat16
HIGHEST = lax.Precision.HIGHEST
HEAD_PAIRS = NA_HEADS // 2
TOK_BLOCK = 256
CHUNKS = D_MODEL // LANES
TABLE_ROWS_PER_EXPERT = CHUNKS // 2
PAIR_STRIDE = N_SEL // 4


def _cparams(*sem):
    return pltpu.CompilerParams(dimension_semantics=sem, vmem_limit_bytes=VMEM_LIMIT_BYTES)


def _dot_nt(a, b, precision=None):
    return lax.dot_general(a, b, (((1,), (1,)), ((), ())), precision=precision,
                           preferred_element_type=F32)


def _split_bf16(a):
    hi = a.astype(BF16)
    return hi, (a - hi.astype(F32)).astype(BF16)


def _rms_mod(x, nw, shift, scale):
    y = x * lax.rsqrt(jnp.mean(x * x, axis=-1, keepdims=True) + NORM_EPS)
    return (y * nw) * (1.0 + scale) + shift


def _mod_kernel(c_ref, w_ref, b_ref, o_ref):
    c = c_ref[...]
    a = c * jax.nn.sigmoid(c)
    o_ref[0] = jnp.dot(a, w_ref[0], precision=HIGHEST, preferred_element_type=F32) + b_ref[0]


def _modulation(cvec, w_ada, b_ada):
    depth, d, n = w_ada.shape
    tn = 1536
    return pl.pallas_call(
        _mod_kernel,
        grid=(depth, n // tn),
        in_specs=[pl.BlockSpec((8, d), lambda l, j: (0, 0)),
                  pl.BlockSpec((1, d, tn), lambda l, j: (l, 0, j)),
                  pl.BlockSpec((1, 1, tn), lambda l, j: (l, 0, j))],
        out_specs=pl.BlockSpec((1, 8, tn), lambda l, j: (l, 0, j)),
        out_shape=jax.ShapeDtypeStruct((depth, 8, n), F32),
        compiler_params=_cparams("parallel", "parallel"),
        name="adaln_mod",
    )(cvec, w_ada, b_ada.reshape(depth, 1, n))


def _inproj_kernel(x_ref, nw_ref, sh_ref, sc_ref, w_ref, o_ref, p_ref):
    h = _rms_mod(x_ref[0], nw_ref[...], sh_ref[0], sc_ref[0])
    r = jnp.dot(h.astype(BF16), w_ref[...], preferred_element_type=F32)
    o_ref[0] = r[:, :O_POOL]
    p_ref[0] = r[:, O_POOL:]


def _in_proj(x, nw, shift, scale, w_bf16):
    b, t, d = x.shape
    tm = min(512, t)
    return pl.pallas_call(
        _inproj_kernel,
        grid=(b, t // tm),
        in_specs=[pl.BlockSpec((1, tm, d), lambda i, j: (i, j, 0)),
                  pl.BlockSpec((1, d), lambda i, j: (0, 0)),
                  pl.BlockSpec((1, 1, d), lambda i, j: (i, 0, 0)),
                  pl.BlockSpec((1, 1, d), lambda i, j: (i, 0, 0)),
                  pl.BlockSpec((d, D_PROJ), lambda i, j: (0, 0))],
        out_specs=[pl.BlockSpec((1, tm, O_POOL), lambda i, j: (i, j, 0)),
                   pl.BlockSpec((1, tm, POOL_WIDTH), lambda i, j: (i, j, 0))],
        out_shape=[jax.ShapeDtypeStruct((b, t, O_POOL), F32),
                   jax.ShapeDtypeStruct((b, t, POOL_WIDTH), F32)],
        compiler_params=_cparams("parallel", "parallel"),
        name="in_proj",
    )(x, nw.reshape(1, d), shift, scale, w_bf16)


def _softmax_pv(s_list, v_list):
    m = s_list[0].max(axis=-1, keepdims=True)
    for s in s_list[1:]:
        m = jnp.maximum(m, s.max(axis=-1, keepdims=True))
    num = None
    den = None
    for s, v in zip(s_list, v_list):
        p = jnp.exp(s - m)
        pv = jnp.dot(p.astype(BF16), v.astype(BF16), preferred_element_type=F32)
        ps = p.sum(axis=-1, keepdims=True)
        num = pv if num is None else num + pv
        den = ps if den is None else den + ps
    return num / den


NA_ROWS_PER_TRIP = 4


def _na_kernel(q_ref, k_ref, v_ref, kc_ref, vc_ref, bias_ref, o_ref, *, rows, rb):
    i = pl.program_id(2)
    lane = lax.broadcasted_iota(jnp.int32, (1, LANES), 1)
    first = lane < HEAD_DIM
    kc = kc_ref[0].astype(BF16)
    vc = vc_ref[0].astype(BF16)
    scale = HEAD_DIM ** -0.5
    nk = NA_WIN_R * GRID_W

    def body(it, carry):
        for u in range(NA_ROWS_PER_TRIP):
            one_row(it * NA_ROWS_PER_TRIP + u)
        return carry

    def one_row(rr):
        r = i * rb + rr
        rs = jnp.clip(r - NA_WIN_R // 2, 0, rows - NA_WIN_R)
        delta = rs - r + (NA_WIN_R - 1)
        q = q_ref[0, pl.ds(pl.multiple_of(rr * GRID_W, GRID_W), GRID_W), :]
        k = k_ref[0, pl.ds(pl.multiple_of(rs * GRID_W, GRID_W), nk), :].astype(BF16)
        v = v_ref[0, pl.ds(pl.multiple_of(rs * GRID_W, GRID_W), nk), :].astype(BF16)
        q2 = jnp.concatenate([jnp.where(first, q, 0.0), jnp.where(first, 0.0, q)], axis=0).astype(BF16)
        s = _dot_nt(q2, k) * scale + bias_ref[0, delta]
        sc = _dot_nt(q2, kc) * scale
        o2 = _softmax_pv([s, sc], [v, vc])
        o_ref[0, pl.ds(pl.multiple_of(rr * GRID_W, GRID_W), GRID_W), :] = jnp.where(
            first, o2[0:GRID_W], o2[GRID_W:2 * GRID_W])

    lax.fori_loop(0, rb // NA_ROWS_PER_TRIP, body, 0)


def _na_attention(px, pc, bias8):
    b, s, _ = px.shape
    c = pc.shape[1]
    rows = s // GRID_W
    assert rows >= NA_WIN_R and s % GRID_W == 0
    rb = min(16, rows)
    nq, nkb, nvb = 0, HEAD_PAIRS, 2 * HEAD_PAIRS
    return pl.pallas_call(
        functools.partial(_na_kernel, rows=rows, rb=rb),
        grid=(b, HEAD_PAIRS, rows // rb),
        in_specs=[pl.BlockSpec((1, rb * GRID_W, LANES), lambda bi, hp, i: (bi, i, nq + hp)),
                  pl.BlockSpec((1, s, LANES), lambda bi, hp, i: (bi, 0, nkb + hp)),
                  pl.BlockSpec((1, s, LANES), lambda bi, hp, i: (bi, 0, nvb + hp)),
                  pl.BlockSpec((1, c, LANES), lambda bi, hp, i: (bi, 0, nkb + hp)),
                  pl.BlockSpec((1, c, LANES), lambda bi, hp, i: (bi, 0, nvb + hp)),
                  pl.BlockSpec((1, NA_WIN_R, 2 * GRID_W, NA_WIN_R * GRID_W),
                               lambda bi, hp, i: (hp, 0, 0, 0))],
        out_specs=pl.BlockSpec((1, rb * GRID_W, LANES), lambda bi, hp, i: (bi, i, hp)),
        out_shape=jax.ShapeDtypeStruct((b, s, NA_WIDTH), F32),
        compiler_params=_cparams("parallel", "parallel", "arbitrary"),
        name="na_attention",
    )(px, px, px, pc, pc, bias8)


def _ctx_attn_kernel(q_ref, k_ref, v_ref, o_ref):
    lane = lax.broadcasted_iota(jnp.int32, (1, LANES), 1)
    first = lane < HEAD_DIM
    q = q_ref[0]
    k = k_ref[0]
    v = v_ref[0]
    outs = []
    for h in range(2):
        qm = jnp.where(first if h == 0 else jnp.logical_not(first), q, 0.0)
        s = _dot_nt(qm, k) * HEAD_DIM ** -0.5
        outs.append(_softmax_pv([s], [v]))
    o_ref[0] = jnp.where(first, outs[0], outs[1])


def _ctx_attention(pc):
    b, c, _ = pc.shape
    return pl.pallas_call(
        _ctx_attn_kernel,
        grid=(b, HEAD_PAIRS),
        in_specs=[pl.BlockSpec((1, c, LANES), lambda bi, hp: (bi, 0, hp)),
                  pl.BlockSpec((1, c, LANES), lambda bi, hp: (bi, 0, HEAD_PAIRS + hp)),
                  pl.BlockSpec((1, c, LANES), lambda bi, hp: (bi, 0, 2 * HEAD_PAIRS + hp))],
        out_specs=pl.BlockSpec((1, c, LANES), lambda bi, hp: (bi, 0, hp)),
        out_shape=jax.ShapeDtypeStruct((b, c, NA_WIDTH), F32),
        compiler_params=_cparams("parallel", "parallel"),
        name="ctx_attention",
    )(pc, pc, pc)


RET_CHUNKS_PER_STEP = 8


def _ret_kernel(lg_ref, qf_ref, kf_ref, vf_ref, qb_ref, kb_ref, vb_ref,
                cf_ref, sf_ref, cb_ref, sb_ref, rf0_ref, rb0_ref,
                yf_ref, yb_ref, rfo_ref, rbo_ref, rf_scr, rb_scr, *, cpb):
    hp = pl.program_id(1)
    c = pl.program_id(2)
    nc = pl.num_programs(2)
    cs = RET_CHUNK

    @pl.when(c == 0)
    def _():
        rf_scr[...] = rf0_ref[0, 0]
        rb_scr[...] = rb0_ref[0, 0]

    lane = lax.broadcasted_iota(jnp.int32, (1, LANES), 1)
    first = lane < HEAD_DIM
    low = (lane % (2 * ROPE_PAIRS)) < ROPE_PAIRS
    pos = lax.broadcasted_iota(jnp.int32, (cs, 1), 0).astype(F32)
    ii = lax.broadcasted_iota(jnp.int32, (cs, cs), 0)
    jj = lax.broadcasted_iota(jnp.int32, (cs, cs), 1)
    diff = (ii - jj).astype(F32)
    same_head = (ii < HEAD_DIM) == (jj < HEAD_DIM)
    scale = HEAD_DIM ** -0.5

    def rope(x, cos, sin):
        swapped = jnp.where(low, pltpu.roll(x, LANES - ROPE_PAIRS, 1), pltpu.roll(x, ROPE_PAIRS, 1))
        return x * cos + swapped * sin

    def direction(d, q_ref, k_ref, v_ref, cos_ref, sin_ref, r_scr, y_ref):
        lg0 = lg_ref[d, 2 * hp]
        lg1 = lg_ref[d, 2 * hp + 1]
        lgv = jnp.where(first, lg0, lg1)
        decs = []
        for lg in (lg0, lg1):
            if d == 0:
                decs.append(jnp.where(diff >= 0, jnp.exp(jnp.maximum(diff, 0.0) * lg), 0.0))
            else:
                decs.append(jnp.where(diff <= 0, jnp.exp(jnp.maximum(-diff, 0.0) * lg), 0.0))
        dec2 = jnp.concatenate(decs, axis=0)
        if d == 0:
            xi = jnp.exp((pos + 1.0) * lgv)
            zeta = jnp.exp((cs - 1.0 - pos) * lgv)
        else:
            xi = jnp.exp((cs - pos) * lgv)
            zeta = jnp.exp(pos * lgv)
        chunk_decay = jnp.exp(cs * lgv)
        r = r_scr[...]
        for j in (range(cpb) if d == 0 else reversed(range(cpb))):
            rows = pl.ds(j * cs, cs)
            cos = cos_ref[rows, :]
            sin = sin_ref[rows, :]
            q = rope(q_ref[0, rows, :], cos, sin)
            k = rope(k_ref[0, rows, :], cos, sin) * scale
            v = v_ref[0, rows, :].astype(BF16)
            q2 = jnp.concatenate([jnp.where(first, q, 0.0), jnp.where(first, 0.0, q)], axis=0).astype(BF16)
            s2 = _dot_nt(q2, k.astype(BF16)) * dec2
            o2 = jnp.dot(s2.astype(BF16), v, preferred_element_type=F32)
            inner = jnp.where(first, o2[0:cs], o2[cs:2 * cs])
            y_ref[0, rows, :] = inner + jnp.dot((q * xi).astype(BF16), r.astype(BF16),
                                                preferred_element_type=F32)
            kv = lax.dot_general((k * zeta).astype(BF16), v, (((0,), (0,)), ((), ())),
                                 preferred_element_type=F32)
            r = chunk_decay * r + jnp.where(same_head, kv, 0.0)
        r_scr[...] = r

    direction(0, qf_ref, kf_ref, vf_ref, cf_ref, sf_ref, rf_scr, yf_ref)
    direction(1, qb_ref, kb_ref, vb_ref, cb_ref, sb_ref, rb_scr, yb_ref)

    @pl.when(c == nc - 1)
    def _():
        rfo_ref[0, 0] = rf_scr[...]
        rbo_ref[0, 0] = rb_scr[...]


def _retention(p, lg, cos_t, sin_t, rf0, rb0):
    b, t, _ = p.shape
    cpb = min(RET_CHUNKS_PER_STEP, t // RET_CHUNK)
    nc = t // (RET_CHUNK * cpb)
    qo, ko, vo = 3 * HEAD_PAIRS, 4 * HEAD_PAIRS, 5 * HEAD_PAIRS
    blk = (1, RET_CHUNK * cpb, LANES)
    fwd = lambda o: pl.BlockSpec(blk, lambda bi, hp, c: (bi, c, o + hp))
    bwd = lambda o: pl.BlockSpec(blk, lambda bi, hp, c: (bi, nc - 1 - c, o + hp))
    tab_f = pl.BlockSpec((RET_CHUNK * cpb, LANES), lambda bi, hp, c: (c, 0))
    tab_b = pl.BlockSpec((RET_CHUNK * cpb, LANES), lambda bi, hp, c: (nc - 1 - c, 0))
    st = pl.BlockSpec((1, 1, LANES, LANES), lambda bi, hp, c: (bi, hp, 0, 0))
    return pl.pallas_call(
        functools.partial(_ret_kernel, cpb=cpb),
        grid=(b, HEAD_PAIRS, nc),
        in_specs=[pl.BlockSpec(memory_space=pltpu.SMEM),
                  fwd(qo), fwd(ko), fwd(vo), bwd(qo), bwd(ko), bwd(vo),
                  tab_f, tab_f, tab_b, tab_b, st, st],
        out_specs=[pl.BlockSpec(blk, lambda bi, hp, c: (bi, c, hp)),
                   pl.BlockSpec(blk, lambda bi, hp, c: (bi, nc - 1 - c, hp)),
                   st, st],
        out_shape=[jax.ShapeDtypeStruct((b, t, RET_WIDTH), F32),
                   jax.ShapeDtypeStruct((b, t, RET_WIDTH), F32),
                   jax.ShapeDtypeStruct((b, HEAD_PAIRS, LANES, LANES), F32),
                   jax.ShapeDtypeStruct((b, HEAD_PAIRS, LANES, LANES), F32)],
        scratch_shapes=[pltpu.VMEM((LANES, LANES), F32), pltpu.VMEM((LANES, LANES), F32)],
        compiler_params=_cparams("parallel", "parallel", "arbitrary"),
        name="retention",
    )(lg, p, p, p, p, p, p, cos_t, sin_t, cos_t, sin_t, rf0, rb0)


def _pool_kernel(prev_ref, cur_ref, next_ref, w_ref, s_ref, o_ref, scr, *, t_total, tp):
    i = pl.program_id(1)
    n = pl.num_programs(1)
    halo = POOL_WINDOWS[-1] // 2
    x = cur_ref[0]
    scr[0:halo, :] = jnp.where(i > 0, prev_ref[0], 0.0)
    scr[halo:halo + tp, :] = x
    scr[halo + tp:2 * halo + tp, :] = jnp.where(i < n - 1, next_ref[0], 0.0)
    t = i * tp + lax.broadcasted_iota(jnp.int32, (tp, 1), 0)
    lane = lax.broadcasted_iota(jnp.int32, (1, POOL_WIDTH), 1)

    def shifted(s):
        return scr[halo + s:halo + s + tp, :]

    acc = None
    mean = None
    done = 0
    for g, w in enumerate(POOL_WINDOWS):
        half = w // 2
        for s in list(range(-half, -done)) + list(range(done, half)):
            sh = x if s == 0 else shifted(s)
            acc = sh if acc is None else acc + sh
        done = half
        cnt = (jnp.minimum(t + half, t_total) - jnp.maximum(t - half, 0)).astype(F32)
        mg = acc / cnt
        mean = mg if mean is None else jnp.where(lane >= g * POOL_GROUP, mg, mean)
    dlt = (mean - x).astype(BF16)
    o_ref[0] = jnp.dot(dlt, w_ref[...], preferred_element_type=F32) * s_ref[...]


def _pool(pin, w_bd_bf16, scale):
    b, t, _ = pin.shape
    tp = min(1024, t)
    halo = POOL_WINDOWS[-1] // 2
    nh = tp // halo
    last = t // halo - 1
    return pl.pallas_call(
        functools.partial(_pool_kernel, t_total=t, tp=tp),
        grid=(b, t // tp),
        in_specs=[pl.BlockSpec((1, halo, POOL_WIDTH), lambda bi, i: (bi, jnp.maximum(i * nh - 1, 0), 0)),
                  pl.BlockSpec((1, tp, POOL_WIDTH), lambda bi, i: (bi, i, 0)),
                  pl.BlockSpec((1, halo, POOL_WIDTH), lambda bi, i: (bi, jnp.minimum((i + 1) * nh, last), 0)),
                  pl.BlockSpec((POOL_WIDTH, POOL_WIDTH), lambda bi, i: (0, 0)),
                  pl.BlockSpec((1, POOL_WIDTH), lambda bi, i: (0, 0))],
        out_specs=pl.BlockSpec((1, tp, POOL_WIDTH), lambda bi, i: (bi, i, 0)),
        out_shape=jax.ShapeDtypeStruct((b, t, POOL_WIDTH), F32),
        scratch_shapes=[pltpu.VMEM((tp + 2 * halo, POOL_WIDTH), F32)],
        compiler_params=_cparams("parallel", "parallel"),
        name="multiscale_pool",
    )(pin, pin, pin, w_bd_bf16, scale.reshape(1, POOL_WIDTH))


def _out_kernel(na_ref, yf_ref, yb_ref, g_ref, pool_ref, x_ref, g1_ref, gnw_ref, avg_ref, wo_ref, o_ref):
    y = yf_ref[0] + yb_ref[0]
    avg = avg_ref[...]

    def group_mean(a):
        hi = a.astype(BF16)
        rest = a - hi.astype(F32)
        mid = rest.astype(BF16)
        lo = (rest - mid.astype(F32)).astype(BF16)
        return (jnp.dot(hi, avg, preferred_element_type=F32) + jnp.dot(mid, avg, preferred_element_type=F32)
                + jnp.dot(lo, avg, preferred_element_type=F32))

    mu = group_mean(y)
    d = y - mu
    var = group_mean(d * d)
    yn = d * lax.rsqrt(var + NORM_EPS) * gnw_ref[...]
    g = g_ref[0]
    ret = yn * (g * jax.nn.sigmoid(g))
    mix = jnp.dot(na_ref[0].astype(BF16), wo_ref[0:NA_WIDTH, :], preferred_element_type=F32)
    mix += jnp.dot(ret.astype(BF16), wo_ref[NA_WIDTH:NA_WIDTH + RET_WIDTH, :], preferred_element_type=F32)
    mix += jnp.dot(pool_ref[0].astype(BF16), wo_ref[NA_WIDTH + RET_WIDTH:, :], preferred_element_type=F32)
    o_ref[0] = x_ref[0] + g1_ref[0] * mix


def _out_proj(na, yf, yb, p, pool, x, g1, gn_w, avg, wo_bf16):
    b, t, d = x.shape
    tm = min(512, t)
    row = lambda w: pl.BlockSpec((1, tm, w), lambda i, j: (i, j, 0))
    return pl.pallas_call(
        _out_kernel,
        grid=(b, t // tm),
        in_specs=[row(NA_WIDTH), row(RET_WIDTH), row(RET_WIDTH),
                  pl.BlockSpec((1, tm, RET_WIDTH), lambda i, j: (i, j, O_RET_G // RET_WIDTH)),
                  row(POOL_WIDTH), row(d),
                  pl.BlockSpec((1, 1, d), lambda i, j: (i, 0, 0)),
                  pl.BlockSpec((1, RET_WIDTH), lambda i, j: (0, 0)),
                  pl.BlockSpec((RET_WIDTH, RET_WIDTH), lambda i, j: (0, 0)),
                  pl.BlockSpec((d, d), lambda i, j: (0, 0))],
        out_specs=row(d),
        out_shape=jax.ShapeDtypeStruct((b, t, d), F32),
        compiler_params=_cparams("parallel", "parallel"),
        name="out_proj",
    )(na, yf, yb, p, pool, x, g1, gn_w.reshape(1, RET_WIDTH), avg, wo_bf16)


def _topk_rows(s, order=None, payload=None):
    n, m = s.shape
    if order is None:
        order = lax.broadcasted_iota(jnp.int32, (n, m), 0)
    krow = lax.broadcasted_iota(jnp.int32, (PEER_TOPK, m), 0)
    vals = jnp.zeros((PEER_TOPK, m), F32)
    idxs = jnp.zeros((PEER_TOPK, m), jnp.int32)
    for k in range(PEER_TOPK):
        mx = jnp.max(s, axis=0, keepdims=True)
        am = jnp.min(jnp.where(s == mx, order, jnp.iinfo(jnp.int32).max), axis=0, keepdims=True)
        sel = order == am
        pick = am if payload is None else jnp.sum(jnp.where(sel, payload, 0), axis=0, keepdims=True)
        vals = jnp.where(krow == k, mx, vals)
        idxs = jnp.where(krow == k, pick, idxs)
        s = jnp.where(sel, -jnp.inf, s)
    return vals, idxs


def _product_candidates(va, ia, vb, ib):
    k = PEER_TOPK
    sub = 8
    m = va.shape[1]
    row = lax.broadcasted_iota(jnp.int32, (sub, m), 0)
    cand, flat, eid = [], [], []

    def add(v, f, e, nvalid):
        cand.append(v if nvalid >= sub else jnp.where(row < nvalid, v, -jnp.inf))
        flat.append(f)
        eid.append(e)

    for i in range(sub):
        add(va[i:i + 1, :] + vb[0:sub, :], i * k + row, ia[i:i + 1, :] * PEER_NKEYS + ib[0:sub, :], k // (i + 1))
    add(va[0:1, :] + vb[sub:k, :], sub + row, ia[0:1, :] * PEER_NKEYS + ib[sub:k, :], sub)
    add(va[sub:k, :] + vb[0:1, :], (sub + row) * k, ia[sub:k, :] * PEER_NKEYS + ib[0:1, :], sub)
    return jnp.concatenate(cand, axis=0), jnp.concatenate(flat, axis=0), jnp.concatenate(eid, axis=0)


def _route_kernel(x_ref, nw_ref, sh_ref, sc_ref, wq_ref, keys_ref, idx_ref, gate_ref, q_scr, g_scr, i_scr):
    h_hi, h_lo = _split_bf16(_rms_mod(x_ref[0], nw_ref[...], sh_ref[0], sc_ref[0]))
    q = (jnp.dot(h_hi, wq_ref[0], preferred_element_type=F32)
         + jnp.dot(h_hi, wq_ref[1], preferred_element_type=F32)
         + jnp.dot(h_lo, wq_ref[0], preferred_element_type=F32))
    for j in range(2 * PEER_HEADS):
        q_scr[j] = q[:, j * PEER_KEY_DIM:(j + 1) * PEER_KEY_DIM]

    def scores(half, hh):
        q_hi, q_lo = _split_bf16(q_scr[2 * hh + half])
        k_hi = keys_ref[0, half, hh]
        return _dot_nt(k_hi, q_hi) + _dot_nt(k_hi, q_lo) + _dot_nt(keys_ref[1, half, hh], q_hi)

    def head(hh, carry):
        sa = scores(0, hh)
        sb = scores(1, hh)
        va, ia = _topk_rows(sa)
        vb, ib = _topk_rows(sb)
        cand, flat, eid = _product_candidates(va, ia, vb, ib)
        sc, ei = _topk_rows(cand, order=flat, payload=eid)
        e = jnp.exp(sc - sc[0:1, :])
        off = pl.multiple_of(hh * PEER_TOPK, PEER_TOPK)
        g_scr[pl.ds(off, PEER_TOPK), :] = e / jnp.sum(e, axis=0, keepdims=True)
        i_scr[pl.ds(off, PEER_TOPK), :] = ei
        return carry

    def head_pair(i, carry):
        head(2 * i, carry)
        return head(2 * i + 1, carry)

    lax.fori_loop(0, PEER_HEADS // 2, head_pair, 0)
    gate_ref[0] = g_scr[...].T
    idx_ref[0] = i_scr[...].T * TABLE_ROWS_PER_EXPERT


def _route(x, nw, shift, scale, wq, keys):
    b, t, d = x.shape
    tm = min(1024, t)
    nt = t // tm
    nq = wq.shape[2]
    return pl.pallas_call(
        _route_kernel,
        grid=(b, nt),
        in_specs=[pl.BlockSpec((1, tm, d), lambda i, j: (i, j, 0)),
                  pl.BlockSpec((1, d), lambda i, j: (0, 0)),
                  pl.BlockSpec((1, 1, d), lambda i, j: (i, 0, 0)),
                  pl.BlockSpec((1, 1, d), lambda i, j: (i, 0, 0)),
                  pl.BlockSpec((2, d, nq), lambda i, j: (0, 0, 0)),
                  pl.BlockSpec((2, 2, PEER_HEADS, PEER_NKEYS, PEER_KEY_DIM), lambda i, j: (0, 0, 0, 0, 0))],
        out_specs=[pl.BlockSpec((1, tm, N_SEL), lambda i, j: (i, j, 0)),
                   pl.BlockSpec((1, tm, N_SEL), lambda i, j: (i, j, 0))],
        out_shape=[jax.ShapeDtypeStruct((b, t, N_SEL), jnp.int32),
                   jax.ShapeDtypeStruct((b, t, N_SEL), F32)],
        scratch_shapes=[pltpu.VMEM((2 * PEER_HEADS, tm, PEER_KEY_DIM), F32),
                        pltpu.VMEM((N_SEL, tm), F32),
                        pltpu.VMEM((N_SEL, tm), jnp.int32)],
        compiler_params=_cparams("parallel", "parallel"),
        name="peer_route",
    )(x, nw.reshape(1, d), shift, scale, wq, keys)


def _pack_kernel(t_ref, o_ref, *, te):
    x = t_ref[0]
    d = x.shape[1]

    def bf16_bits(a):
        return lax.bitcast_convert_type(a.astype(BF16).astype(F32), jnp.uint32)

    words = (bf16_bits(x[:, :d // 2]) >> 16) | bf16_bits(x[:, d // 2:])
    for q in range(TABLE_ROWS_PER_EXPERT):
        o_ref[pl.ds(q, te, stride=TABLE_ROWS_PER_EXPERT), :] = words[:, q * LANES:(q + 1) * LANES]


def _pack_table(tabs, layer):
    _, n, d = tabs.shape
    te = 512
    return pl.pallas_call(
        functools.partial(_pack_kernel, te=te),
        grid=(n // te,),
        in_specs=[pl.BlockSpec((1, te, d), lambda i: (layer, i, 0))],
        out_specs=pl.BlockSpec((te * TABLE_ROWS_PER_EXPERT, LANES), lambda i: (i, 0)),
        out_shape=jax.ShapeDtypeStruct((n * TABLE_ROWS_PER_EXPERT, LANES), jnp.uint32),
        compiler_params=_cparams("parallel"),
        name="pack_table",
    )(tabs)


def _token_loop(tb, token, per_trip):
    def trip(i, carry):
        for j in range(per_trip):
            token(i * per_trip + j, j)
        return carry

    lax.fori_loop(0, tb // per_trip, trip, 0)


def _unpack(slab):
    lo = lax.bitcast_convert_type(slab << 16, F32)
    hi = lax.bitcast_convert_type(slab & jnp.uint32(0xFFFF0000), F32)
    return lo, hi


def _peer_act_kernel(idx_ref, tab_ref, x_ref, nw_ref, sh_ref, sc_ref, gate_ref, w_ref, h_scr, act_scr, *, tb):
    h = _rms_mod(x_ref[...], nw_ref[...], sh_ref[0], sc_ref[0])
    half = CHUNKS // 2
    for c in range(CHUNKS):
        h_scr[pl.ds(c, tb, stride=CHUNKS), :] = h[:, c * LANES:(c + 1) * LANES]
    sub = lax.broadcasted_iota(jnp.int32, (CHUNKS, N_SEL), 0)
    lane = lax.broadcasted_iota(jnp.int32, (CHUNKS, N_SEL), 1)
    slot = lane - PAIR_STRIDE * (sub >= half).astype(jnp.int32)

    def token(t, _):
        ht = h_scr[pl.ds(pl.multiple_of(t * CHUNKS, CHUNKS), CHUNKS), :]
        hlo = jnp.concatenate([ht[0:half], ht[0:half]], axis=0)
        hhi = jnp.concatenate([ht[half:CHUNKS], ht[half:CHUNKS]], axis=0)
        acc = jnp.zeros((CHUNKS, N_SEL), F32)
        for a, slab in _slab_pairs(tab_ref, idx_ref, t):
            lo, hi = _unpack(slab)
            dot = jnp.sum(lo * hlo + hi * hhi, axis=-1, keepdims=True)
            acc = jnp.where(slot == a, dot, acc)
        act_scr[pl.ds(t, 1), :] = jnp.sum(acc, axis=0, keepdims=True)

    _token_loop(tb, token, 32)
    a = act_scr[...]
    w_ref[...] = gate_ref[...] * (0.5 * a * (1.0 + lax.erf(a * (2.0 ** -0.5))))


def _slab_pairs(tab_ref, idx_ref, t):
    views = [idx_ref.at[pl.ds(t * N_SEL + q * PAIR_STRIDE, PAIR_STRIDE)] for q in range(4)]
    for k in range(PAIR_STRIDE):
        rows = [pl.multiple_of(v[k], TABLE_ROWS_PER_EXPERT) for v in views]
        for q in (0, 2):
            yield q * PAIR_STRIDE + k, jnp.concatenate(
                [tab_ref[pl.ds(rows[q], TABLE_ROWS_PER_EXPERT), :],
                 tab_ref[pl.ds(rows[q + 1], TABLE_ROWS_PER_EXPERT), :]], axis=0)


def _peer_out_kernel(pk_ref, tab_ref, x_ref, g2_ref, fw_ref, o_ref, p_scr, *, tb, final):
    half = CHUNKS // 2
    upper = lax.broadcasted_iota(jnp.int32, (CHUNKS, LANES), 0) >= half

    def token(t, _):
        views = [pk_ref.at[pl.ds(t * N_SEL + q * PAIR_STRIDE, PAIR_STRIDE)] for q in range(4)]
        accs = [jnp.zeros((CHUNKS, LANES), F32) for _ in range(4)]
        for k in range(PAIR_STRIDE):
            words = [v[k] for v in views]
            for q in (0, 2):
                rows = [pl.multiple_of(words[q + i] & 0xFFFF, TABLE_ROWS_PER_EXPERT) for i in range(2)]
                slab = jnp.concatenate([tab_ref[pl.ds(r, TABLE_ROWS_PER_EXPERT), :] for r in rows], axis=0)
                lo, hi = _unpack(slab)
                both = jnp.where(upper, words[q + 1], words[q])
                wv = lax.bitcast_convert_type(both & jnp.int32(-65536), F32)
                accs[q] = accs[q] + wv * lo
                accs[q + 1] = accs[q + 1] + wv * hi
        alo = accs[0] + accs[2]
        ahi = accs[1] + accs[3]
        off = pl.multiple_of(t * CHUNKS, CHUNKS)
        p_scr[pl.ds(off, half), :] = alo[0:half] + alo[half:CHUNKS]
        p_scr[pl.ds(pl.multiple_of(off + half, half), half), :] = ahi[0:half] + ahi[half:CHUNKS]

    _token_loop(tb, token, 2)
    peer = jnp.concatenate([p_scr[pl.ds(c, tb, stride=CHUNKS), :] for c in range(CHUNKS)], axis=1)
    y = x_ref[...] + g2_ref[0] * peer
    if final:
        y = y * lax.rsqrt(jnp.mean(y * y, axis=-1, keepdims=True) + NORM_EPS) * fw_ref[...]
    o_ref[...] = y


def _peer_residual(x, nw, shift, scale, g2, wq, keys, u, v, final_w, final):
    b, t, d = x.shape
    assert d == CHUNKS * LANES
    idx, gate = _route(x, nw, shift, scale, wq, keys)
    n = b * t
    tb = TOK_BLOCK
    per_batch = t // tb
    x2 = x.reshape(n, d)
    idx1 = idx.reshape(n * N_SEL)
    smem_blk = pl.BlockSpec((tb * N_SEL,), lambda i: (i,), memory_space=pltpu.SMEM)
    table = pl.BlockSpec(u.shape, lambda i: (0, 0), pipeline_mode=pl.Buffered(1))
    rows = pl.BlockSpec((tb, d), lambda i: (i, 0))
    sel = pl.BlockSpec((tb, N_SEL), lambda i: (i, 0))
    vec = pl.BlockSpec((1, 1, d), lambda i: (i // per_batch, 0, 0))
    const = pl.BlockSpec((1, d), lambda i: (0, 0))
    w = pl.pallas_call(
        functools.partial(_peer_act_kernel, tb=tb),
        grid=(n // tb,),
        in_specs=[smem_blk, table, rows, const, vec, vec, sel],
        out_specs=sel,
        out_shape=jax.ShapeDtypeStruct((n, N_SEL), F32),
        scratch_shapes=[pltpu.VMEM((tb * CHUNKS, LANES), F32), pltpu.VMEM((tb, N_SEL), F32)],
        compiler_params=_cparams("arbitrary"),
        name="peer_act",
    )(idx1, u, x2, nw.reshape(1, d), shift, scale, gate.reshape(n, N_SEL))
    w_bits = lax.bitcast_convert_type(w.astype(BF16).astype(F32), jnp.int32) & jnp.int32(-65536)
    packed = (w_bits | idx.reshape(n, N_SEL)).reshape(n * N_SEL)
    y = pl.pallas_call(
        functools.partial(_peer_out_kernel, tb=tb, final=final),
        grid=(n // tb,),
        in_specs=[smem_blk, table, rows, vec, const],
        out_specs=rows,
        out_shape=jax.ShapeDtypeStruct((n, d), F32),
        scratch_shapes=[pltpu.VMEM((tb * CHUNKS, LANES), F32)],
        compiler_params=_cparams("arbitrary"),
        name="peer_out",
    )(packed, v, x2, g2, final_w.reshape(1, d))
    return y.reshape(b, t, d)


def _na_bias_table(rpb):
    cq = jnp.arange(GRID_W)
    coff = jnp.clip(cq[None, :] - cq[:, None] + (NA_WIN_C - 1), 0, 2 * NA_WIN_C - 2)
    c_start = jnp.clip(cq - NA_WIN_C // 2, 0, GRID_W - NA_WIN_C)
    ok = (cq[None, :] >= c_start[:, None]) & (cq[None, :] < c_start[:, None] + NA_WIN_C)
    roff = jnp.arange(NA_WIN_R)[:, None] + jnp.arange(NA_WIN_R)[None, :]
    bias = rpb.astype(F32)[:, :, coff][:, roff]
    bias = jnp.where(ok[None, None, None], bias, NEG_INF)
    bias = bias.transpose(0, 1, 3, 2, 4).reshape(HEAD_PAIRS, 2, NA_WIN_R, GRID_W, NA_WIN_R * GRID_W)
    return bias.transpose(0, 2, 1, 3, 4).reshape(HEAD_PAIRS, NA_WIN_R, 2 * GRID_W, NA_WIN_R * GRID_W)


def _rope_tables(s):
    t = jnp.arange(s)
    row = (t // GRID_W).astype(F32)
    col = (t % GRID_W).astype(F32)
    inv = ROPE_BASE ** (-jnp.arange(ROPE_PAIRS, dtype=F32) / ROPE_PAIRS)
    cr, sr = jnp.cos(row[:, None] * inv), jnp.sin(row[:, None] * inv)
    cc, sc = jnp.cos(col[:, None] * inv), jnp.sin(col[:, None] * inv)
    cos = jnp.concatenate([cr, cr, cc, cc], axis=-1)
    sin = jnp.concatenate([-sr, sr, -sc, sc], axis=-1)
    return jnp.tile(cos, (1, 2)), jnp.tile(sin, (1, 2))


def _block_diag(blocks):
    n, a, bb = blocks.shape
    eye = jnp.eye(n, dtype=blocks.dtype)
    return (eye[:, None, :, None] * blocks[:, :, None, :]).reshape(n * a, n * bb)


def kernel(x, c, ctx, c_ctx, norm1_w, norm2_w, w_ada, b_ada, w_in, w_out, na_rpb, ret_decay_fwd, ret_decay_bwd, ret_gn_w, pool_w, pool_scale, peer_wq, peer_keys, peer_u, peer_v, final_norm_w):
    b, s, d = x.shape
    depth = w_in.shape[0]
    clen = ctx.shape[1]
    cvec = jnp.concatenate([c, c_ctx[None, :], jnp.zeros((8 - b - 1, d), F32)], axis=0)
    mod = _modulation(cvec, w_ada, b_ada)
    cos_x, sin_x = _rope_tables(s)
    cos_c, sin_c = jnp.ones((clen, LANES), F32), jnp.zeros((clen, LANES), F32)
    avg = _block_diag(jnp.full((RET_HEADS, HEAD_DIM, HEAD_DIM), 1.0 / HEAD_DIM, BF16))
    zero_state = jnp.zeros((b, HEAD_PAIRS, LANES, LANES), F32)

    for l in range(depth):
        last = l == depth - 1
        mx = mod[l, :b].reshape(b, 1, 6, d)
        mc = jnp.broadcast_to(mod[l, b].reshape(1, 1, 6, d), (b, 1, 6, d))
        sh1, sc1, g1, sh2, sc2, g2 = [mx[:, :, i] for i in range(6)]
        csh1, csc1, cg1, csh2, csc2, cg2 = [mc[:, :, i] for i in range(6)]
        lg = jnp.stack([jax.nn.log_sigmoid(ret_decay_fwd[l].astype(F32)),
                        jax.nn.log_sigmoid(ret_decay_bwd[l].astype(F32))], axis=0)
        wi = w_in[l].astype(BF16)
        wo = w_out[l].astype(BF16)
        wpool = _block_diag(pool_w[l]).astype(BF16)
        bias8 = _na_bias_table(na_rpb[l])

        pc, pcp = _in_proj(ctx, norm1_w[l], csh1, csc1, wi)
        ycf, ycb, r_f, r_b = _retention(pc, lg, cos_c, sin_c, zero_state, zero_state)

        px, pxp = _in_proj(x, norm1_w[l], sh1, sc1, wi)
        na = _na_attention(px, pc, bias8)
        yf, yb, _, _ = _retention(px, lg, cos_x, sin_x, r_f, r_b)
        pool = _pool(pxp, wpool, pool_scale[l])
        x = _out_proj(na, yf, yb, px, pool, x, g1, ret_gn_w[l], avg, wo)
        wq = jnp.stack(_split_bf16(peer_wq[l]))
        keys = jnp.stack(_split_bf16(peer_keys[l]))
        u_tab = _pack_table(peer_u, l)
        v_tab = _pack_table(peer_v, l)
        x = _peer_residual(x, norm2_w[l], sh2, sc2, g2, wq, keys, u_tab, v_tab, final_norm_w, last)

        if not last:
            na_c = _ctx_attention(pc)
            pool_c = _pool(pcp, wpool, pool_scale[l])
            ctx = _out_proj(na_c, ycf, ycb, pc, pool_c, ctx, cg1, ret_gn_w[l], avg, wo)
            ctx = _peer_residual(ctx, norm2_w[l], csh2, csc2, cg2, wq, keys, u_tab, v_tab, final_norm_w, False)
    return x
```
